```python
import math
import jax
import jax.numpy as jnp
from jax import lax
import numpy as np

D_MODEL = 1024
BATCH = 4
SEQ = 8192
DEPTH = 2
DEC_BATCH = 32
DEC_SEQ = 1
PAST_LEN = 16384
PAGE_SIZE = 128

H_A = 4
DK_A = 128
DV_A = 128
CHUNK_A = 128
SWA_GROUPS = ((128, 1), (512, 4), (2048, 16))
N_GROUPS_B = len(SWA_GROUPS)
H_B = 4
DH_B = 64
SWA_BAND = 128
H_C = 8
DH_C = 64
MOBA_BLOCK = 256
MOBA_TOPK = 3
MOBA_QCHUNK = 64
H_D = 4
DK_D = 128
DV_D = 128
CHUNK_D = 64
D_FF = ((8 * D_MODEL + 3 * 256 - 1) // (3 * 256)) * 256

EVEN_COLS = (H_A * DK_A, H_A * DK_A, H_A * DV_A, H_A * DV_A, H_A, H_A,
             N_GROUPS_B * H_B * DH_B, N_GROUPS_B * H_B * DH_B, N_GROUPS_B * H_B * DH_B)
OUT_EVEN = H_A * DV_A + H_B * DH_B
ODD_COLS = (H_C * DH_C, H_C * DH_C, H_C * DH_C, H_D * DK_D, H_D * DK_D, H_D * DV_D, H_D * DV_D)
OUT_ODD = H_C * DH_C + H_D * DV_D

EPS = 1e-6
NEG = -1e30

kernel_name = "hybrid_mlstm_dilswa_moba_hgrn2_step"


def rms_norm(x, g):
    xf = x.astype(jnp.float32)
    y = xf * lax.rsqrt(jnp.mean(xf * xf, axis=-1, keepdims=True) + EPS)
    return (y * g.astype(jnp.float32)).astype(x.dtype)


def split_cols(h, sizes):
    return jnp.split(h, np.cumsum(sizes)[:-1].tolist(), axis=-1)


def mlstm_scan(q, k, v, ig, lf, C0, n0, m0, chunk):
    B, T, H, _ = q.shape
    nc = T // chunk
    def to_chunks(a):
        return jnp.swapaxes(a.reshape((B, nc, chunk) + a.shape[2:]), 0, 1)
    xs = tuple(to_chunks(a) for a in (q, k, v, ig, lf))
    causal = jnp.tril(jnp.ones((chunk, chunk), bool))

    def step(carry, xc):
        C, n, m = carry
        qc, kc, vc, ic, fc = xc
        b = jnp.cumsum(fc, axis=1)
        dmat = b[:, :, None, :] - b[:, None, :, :] + ic[:, None, :, :]
        dmat = jnp.where(causal[None, :, :, None], dmat, NEG)
        inter = b + m[:, None, :]
        mt = jnp.maximum(inter, jnp.max(dmat, axis=2))
        wmat = jnp.exp(dmat - mt[:, :, None, :])
        winter = jnp.exp(inter - mt)
        s = jnp.einsum('blhd,bshd->blsh', qc, kc) * wmat
        num = jnp.einsum('blsh,bshv->blhv', s, vc) + winter[..., None] * jnp.einsum('blhd,bhdv->blhv', qc, C)
        den = jnp.sum(s, axis=2) + winter * jnp.einsum('blhd,bhd->blh', qc, n)
        h = num / jnp.maximum(jnp.abs(den), jnp.exp(-mt))[..., None]
        bl = b[:, -1, :]
        gl = bl[:, None, :] - b + ic
        m_new = jnp.maximum(bl + m, jnp.max(gl, axis=1))
        ws = jnp.exp(gl - m_new[:, None, :])
        wc = jnp.exp(bl + m - m_new)
        C_new = wc[..., None, None] * C + jnp.einsum('bsh,bshd,bshv->bhdv', ws, kc, vc)
        n_new = wc[..., None] * n + jnp.einsum('bsh,bshd->bhd', ws, kc)
        return (C_new, n_new, m_new), h

    (C, n, m), hs = lax.scan(step, (C0, n0, m0), xs)
    return jnp.swapaxes(hs, 0, 1).reshape(B, T, H, -1), C, n, m


def hgrn2_scan(q, k, v, lf, S0, chunk):
    B, T, H, _ = q.shape
    nc = T // chunk
    def to_chunks(a):
        return jnp.swapaxes(a.reshape((B, nc, chunk) + a.shape[2:]), 0, 1)
    xs = tuple(to_chunks(a) for a in (q, k, v, lf))
    causal = jnp.tril(jnp.ones((chunk, chunk), bool))

    def step(S, xc):
        qc, kc, vc, fc = xc
        b = jnp.cumsum(fc, axis=1)
        diff = jnp.where(causal[None, :, :, None, None], b[:, :, None] - b[:, None, :], NEG)
        a = jnp.einsum('blhd,bshd,blshd->blsh', qc, kc, jnp.exp(diff))
        o = jnp.einsum('blsh,bshv->blhv', a, vc) + jnp.einsum('blhd,bhdv->blhv', qc * jnp.exp(b), S)
        bl = b[:, -1]
        S_new = jnp.exp(bl)[..., None] * S + jnp.einsum('bshd,bshv->bhdv', kc * jnp.exp(bl[:, None] - b), vc)
        return S_new, o

    S, os_ = lax.scan(step, S0, xs)
    return jnp.swapaxes(os_, 0, 1).reshape(B, T, H, -1), S


def dilated_band_attn(q, k, v, window, dil):
    B, T, H, dh = q.shape
    unit = SWA_BAND * dil
    tp = -(-T // unit) * unit
    ls = tp // dil
    nb = ls // SWA_BAND
    def sub(a):
        a = jnp.pad(a, ((0, 0), (0, tp - T), (0, 0), (0, 0)))
        a = jnp.moveaxis(a.reshape(B, ls, dil, H, dh), 2, 1)
        return a.reshape(B, dil, nb, SWA_BAND, H, dh)
    def with_prev(a):
        prev = jnp.pad(a[:, :, :-1], ((0, 0), (0, 0), (1, 0), (0, 0), (0, 0), (0, 0)))
        return jnp.concatenate([prev, a], axis=3)
    qs, ks, vs = sub(q), with_prev(sub(k)), with_prev(sub(v))
    qi = jnp.arange(SWA_BAND)[:, None]
    kj = jnp.arange(2 * SWA_BAND)[None, :]
    dist = SWA_BAND + qi - kj
    band = (dist >= 0) & (dist <= window // dil)
    real = (jnp.arange(nb)[:, None, None] > 0) | (kj[None] >= SWA_BAND)
    mask = band[None] & real
    s = jnp.einsum('brnqhd,brnkhd->brnhqk', qs, ks) * dh ** -0.5
    s = jnp.where(mask[None, None, :, None], s, NEG)
    mx = jnp.max(s, axis=-1, keepdims=True)
    p = jnp.exp(s - mx)
    den = jnp.sum(p, axis=-1)
    o = jnp.einsum('brnhqk,brnkhd->brnqhd', p, vs) / jnp.swapaxes(den, -1, -2)[..., None]
    lse = jnp.swapaxes(mx[..., 0] + jnp.log(den), -1, -2)
    def unsub(a):
        a = jnp.moveaxis(a.reshape((B, dil, ls) + a.shape[4:]), 1, 2)
        return a.reshape((B, tp) + a.shape[3:])[:, :T]
    return unsub(o), unsub(lse)


def dilated_attn_step(q, k, v, buf, window, dil):
    B, S, H, dh = q.shape
    wb = buf.shape[1]
    kv = jnp.concatenate([buf.astype(jnp.float32), jnp.stack([k, v], axis=2)], axis=1)
    idx = wb + jnp.arange(S)[:, None] - dil * jnp.arange(window // dil + 1)[None, :]
    ok = idx >= 0
    g = kv[:, jnp.maximum(idx, 0)]
    s = jnp.einsum('bshd,bsjhd->bshj', q, g[:, :, :, 0]) * dh ** -0.5
    s = jnp.where(ok[None, :, None, :], s, NEG)
    mx = jnp.max(s, axis=-1, keepdims=True)
    p = jnp.exp(s - mx)
    den = jnp.sum(p, axis=-1)
    o = jnp.einsum('bshj,bsjhd->bshd', p, g[:, :, :, 1]) / den[..., None]
    lse = mx[..., 0] + jnp.log(den)
    return o, lse, kv[:, S:].astype(buf.dtype)


def moba_attend(q, q_pos, kb, vb):
    B, T, H, dh = q.shape
    nblk = kb.shape[1]
    kmean = jnp.mean(kb, axis=2)
    kbh = jnp.moveaxis(kb, 3, 1)
    vbh = jnp.moveaxis(vb, 3, 1)
    qc = min(MOBA_QCHUNK, T)
    tp = -(-T // qc) * qc
    nc = tp // qc
    q_chunks = jnp.moveaxis(jnp.pad(q, ((0, 0), (0, tp - T), (0, 0), (0, 0))).reshape(B, nc, qc, H, dh), 1, 0)
    p_chunks = jnp.pad(q_pos, (0, tp - T), mode='edge').reshape(nc, qc)
    bi = jnp.arange(B)[:, None, None, None]
    hi = jnp.arange(H)[None, None, :, None]
    n_sel = max(MOBA_TOPK, nblk)

    def one(args):
        qq, pos = args
        own = pos // MOBA_BLOCK
        gate = jnp.einsum('bqhd,bnhd->bqhn', qq, kmean)
        past = jnp.arange(nblk)[None, None, None, :] < own[None, :, None, None]
        gate = jnp.pad(jnp.where(past, gate, NEG), ((0, 0), (0, 0), (0, 0), (0, n_sel - nblk)),
                       constant_values=NEG)
        _, sel = lax.top_k(gate, MOBA_TOPK)
        sel_ok = sel < own[None, :, None, None]
        own_b = jnp.broadcast_to(own[None, :, None, None], sel.shape[:-1] + (1,))
        blocks = jnp.concatenate([jnp.minimum(sel, nblk - 1), own_b], axis=-1)
        kg = kbh[bi, hi, blocks]
        vg = vbh[bi, hi, blocks]
        s = jnp.einsum('bqhd,bqhnkd->bqhnk', qq, kg) * dh ** -0.5
        own_pos = own[:, None] * MOBA_BLOCK + jnp.arange(MOBA_BLOCK)[None, :]
        own_ok = (own_pos <= pos[:, None])[None, :, None, None, :]
        ok = jnp.concatenate([jnp.broadcast_to(sel_ok[..., None], sel_ok.shape + (MOBA_BLOCK,)),
                              jnp.broadcast_to(own_ok, (B, qc, H, 1, MOBA_BLOCK))], axis=3)
        s = jnp.where(ok, s, NEG)
        p = jax.nn.softmax(s.reshape(B, qc, H, -1), axis=-1).reshape(s.shape)
        return jnp.einsum('bqhnk,bqhnkd->bqhd', p, vg)

    out = lax.map(one, (q_chunks, p_chunks))
    return jnp.moveaxis(out, 0, 1).reshape(B, tp, H, dh)[:, :T]


def even_mixer(h, w_in, b_gate_a, w_out, state, prompt):
    f32 = jnp.float32
    B, T, _ = h.shape
    qa, ka, va, oa, ia, fa, qb, kb, vb = split_cols(h @ w_in, EVEN_COLS)
    qa = qa.reshape(B, T, H_A, DK_A).astype(f32)
    ka = ka.reshape(B, T, H_A, DK_A).astype(f32) * DK_A ** -0.5
    va = va.reshape(B, T, H_A, DV_A).astype(f32)
    ig = ia.astype(f32) + b_gate_a[0].astype(f32)
    lf = jax.nn.log_sigmoid(fa.astype(f32) + b_gate_a[1].astype(f32))
    if prompt:
        C0 = jnp.zeros((B, H_A, DK_A, DV_A), f32)
        n0 = jnp.zeros((B, H_A, DK_A), f32)
        m0 = jnp.zeros((B, H_A), f32)
        chunk = CHUNK_A
    else:
        C0, n0, m0 = (a.astype(f32) for a in state[:3])
        chunk = T
    ha, C, n, m = mlstm_scan(qa, ka, va, ig, lf, C0, n0, m0, chunk)
    ha = jax.nn.sigmoid(oa.astype(f32)) * ha.reshape(B, T, -1)
    qb = qb.reshape(B, T, N_GROUPS_B, H_B, DH_B).astype(f32)
    kb = kb.reshape(B, T, N_GROUPS_B, H_B, DH_B)
    vb = vb.reshape(B, T, N_GROUPS_B, H_B, DH_B)
    outs, lses, bufs = [], [], []
    for g, (win, dil) in enumerate(SWA_GROUPS):
        if prompt:
            o, lse = dilated_band_attn(qb[:, :, g], kb[:, :, g].astype(f32), vb[:, :, g].astype(f32), win, dil)
            wb = min(win, T)
            buf = jnp.stack([kb[:, T - wb:, g], vb[:, T - wb:, g]], axis=2)
        else:
            o, lse, buf = dilated_attn_step(qb[:, :, g], kb[:, :, g].astype(f32), vb[:, :, g].astype(f32),
                                            state[3 + g], win, dil)
        outs.append(o)
        lses.append(lse)
        bufs.append(buf)
    wts = jax.nn.softmax(jnp.stack(lses), axis=0)
    hb = jnp.sum(wts[..., None] * jnp.stack(outs), axis=0).reshape(B, T, -1)
    y = jnp.concatenate([ha, hb], axis=-1).astype(h.dtype) @ w_out
    return y, (C, n, m, bufs[0], bufs[1], bufs[2])


def odd_mixer(h, w_in, w_lb, g_out_d, w_out, layer, state, prompt):
    f32 = jnp.float32
    B, T, _ = h.shape
    qc, kc, vc, qd, fd, vd, gd = split_cols(h @ w_in, ODD_COLS)
    kv_new = jnp.stack([kc.reshape(B, T, H_C, DH_C), vc.reshape(B, T, H_C, DH_C)], axis=2)
    if prompt:
        kv_all = kv_new
        pos = jnp.arange(T)
    else:
        pool, page_table, _ = state
        kv_past = pool[page_table].reshape(B, PAST_LEN, 2, H_C, DH_C)
        kv_all = jnp.concatenate([kv_past, kv_new.astype(pool.dtype)], axis=1)
        pos = PAST_LEN + jnp.arange(T)
    L = kv_all.shape[1]
    nblk = -(-L // MOBA_BLOCK)
    kv_all = jnp.pad(kv_all, ((0, 0), (0, nblk * MOBA_BLOCK - L), (0, 0), (0, 0), (0, 0)))
    kv_all = kv_all.reshape(B, nblk, MOBA_BLOCK, 2, H_C, DH_C).astype(f32)
    hc = moba_attend(qc.reshape(B, T, H_C, DH_C).astype(f32), pos, kv_all[:, :, :, 0], kv_all[:, :, :, 1])
    hc = hc.reshape(B, T, -1)
    lbs = jnp.cumsum(jax.nn.softmax(w_lb.astype(f32), axis=0), axis=0)
    lb = (lbs - lbs[0])[layer]
    sg = jax.nn.sigmoid(fd.astype(f32))
    f = lb + (1.0 - lb) * sg
    lf = jnp.log(f).reshape(B, T, H_D, DK_D)
    kd = ((1.0 - lb) * (1.0 - sg)).reshape(B, T, H_D, DK_D)
    S0 = jnp.zeros((B, H_D, DK_D, DV_D), f32) if prompt else state[2].astype(f32)
    od, S = hgrn2_scan(qd.reshape(B, T, H_D, DK_D).astype(f32), kd,
                       vd.reshape(B, T, H_D, DV_D).astype(f32), lf, S0, CHUNK_D if prompt else T)
    od = rms_norm(od, g_out_d).reshape(B, T, -1) * jax.nn.silu(gd.astype(f32))
    y = jnp.concatenate([hc, od], axis=-1).astype(h.dtype) @ w_out
    return y, (kv_new, S)


def trunk(x, c, prompt, st_even, st_odd, P):
    new_even, new_odd = None, None
    for l in range(DEPTH):
        ada = (jax.nn.silu(c) @ P['w_ada'][l] + P['b_ada'][l])[:, None, :]
        sh1, sc1, g1, sh2, sc2, g2 = jnp.split(ada, 6, axis=-1)
        h = (rms_norm(x, P['g_pre_mix'][l]) * (1 + sc1) + sh1).astype(x.dtype)
        if l % 2 == 0:
            out, new_even = even_mixer(h, P['w_in_even'], P['b_gate_a'], P['w_out_even'], st_even, prompt)
        else:
            out, new_odd = odd_mixer(h, P['w_in_odd'], P['w_lb'], P['g_out_d'], P['w_out_odd'], l, st_odd, prompt)
        x = (x + g1 * rms_norm(out, P['g_post_mix'][l])).astype(x.dtype)
        h = (rms_norm(x, P['g_pre_ffn'][l]) * (1 + sc2) + sh2).astype(x.dtype)
        f = (jax.nn.silu(h @ P['w_ffn_gate'][l]) * (h @ P['w_ffn_up'][l])) @ P['w_ffn_down'][l]
        x = (x + g2 * rms_norm(f, P['g_post_ffn'][l])).astype(x.dtype)
    return x, new_even, new_odd


def setup_inputs(seed: int = 0) -> dict:
    key = jax.random.key(seed)
    ks = jax.random.split(key, 40)
    f32 = jnp.float32
    def nrm(i, shape, scale=1.0):
        return jax.random.normal(ks[i], shape, f32) * scale
    n_pages = PAST_LEN // PAGE_SIZE
    n_pool = (5 * DEC_BATCH * n_pages + 3) // 4
    page_table = jax.random.permutation(ks[10], n_pool)[:DEC_BATCH * n_pages]
    page_table = page_table.reshape(DEC_BATCH, n_pages).astype(jnp.int32)
    w0, w1, w2 = (w for w, _ in SWA_GROUPS)
    b_gate_a = jnp.stack([nrm(20, (H_A,), 0.1), jnp.linspace(3.0, 6.0, H_A, dtype=f32) + nrm(21, (H_A,), 0.1)])
    return {
        'x_prompt': nrm(0, (BATCH, SEQ, D_MODEL)),
        'x_sample': nrm(1, (DEC_BATCH, DEC_SEQ, D_MODEL)),
        'state_mlstm_C': nrm(2, (DEC_BATCH, H_A, DK_A, DV_A), 0.1),
        'state_mlstm_n': nrm(3, (DEC_BATCH, H_A, DK_A), 0.1),
        'state_mlstm_m': nrm(4, (DEC_BATCH, H_A)),
        'cache_swa_w128': nrm(5, (DEC_BATCH, min(w0, PAST_LEN), 2, H_B, DH_B)),
        'cache_swa_w512': nrm(6, (DEC_BATCH, min(w1, PAST_LEN), 2, H_B, DH_B)),
        'cache_swa_w2048': nrm(7, (DEC_BATCH, min(w2, PAST_LEN), 2, H_B, DH_B)),
        'cache_moba_kv': nrm(8, (n_pool, PAGE_SIZE, 2, H_C, DH_C)),
        'state_hgrn_S': nrm(9, (DEC_BATCH, H_D, DK_D, DV_D)),
        'page_table': page_table,
        'c_prompt': nrm(11, (BATCH, D_MODEL)),
        'c_sample': nrm(12, (DEC_BATCH, D_MODEL)),
        'w_ada': nrm(13, (DEPTH, D_MODEL, 6 * D_MODEL), D_MODEL ** -0.5),
        'b_ada': nrm(14, (DEPTH, 6 * D_MODEL), 0.01),
        'g_pre_mix': 1.0 + nrm(15, (DEPTH, D_MODEL), 0.05),
        'g_post_mix': 1.0 + nrm(16, (DEPTH, D_MODEL), 0.05),
        'g_pre_ffn': 1.0 + nrm(17, (DEPTH, D_MODEL), 0.05),
        'g_post_ffn': 1.0 + nrm(18, (DEPTH, D_MODEL), 0.05),
        'w_in_even': nrm(19, (D_MODEL, sum(EVEN_COLS)), D_MODEL ** -0.5),
        'b_gate_a': b_gate_a,
        'w_out_even': nrm(22, (OUT_EVEN, D_MODEL), OUT_EVEN ** -0.5),
        'w_in_odd': nrm(23, (D_MODEL, sum(ODD_COLS)), D_MODEL ** -0.5),
        'w_lb': nrm(24, (DEPTH, H_D * DK_D)),
        'g_out_d': 1.0 + nrm(25, (DV_D,), 0.05),
        'w_out_odd': nrm(26, (OUT_ODD, D_MODEL), OUT_ODD ** -0.5),
        'w_ffn_gate': nrm(27, (DEPTH, D_MODEL, D_FF), D_MODEL ** -0.5),
        'w_ffn_up': nrm(28, (DEPTH, D_MODEL, D_FF), D_MODEL ** -0.5),
        'w_ffn_down': nrm(29, (DEPTH, D_FF, D_MODEL), D_FF ** -0.5),
    }


def reference(x_prompt, x_sample, state_mlstm_C, state_mlstm_n, state_mlstm_m, cache_swa_w128,
              cache_swa_w512, cache_swa_w2048, cache_moba_kv, state_hgrn_S, page_table, c_prompt, c_sample,
              w_ada, b_ada, g_pre_mix, g_post_mix, g_pre_ffn, g_post_ffn, w_in_even, b_gate_a, w_out_even,
              w_in_odd, w_lb, g_out_d, w_out_odd, w_ffn_gate, w_ffn_up, w_ffn_down):
    P = {'w_ada': w_ada, 'b_ada': b_ada, 'g_pre_mix': g_pre_mix, 'g_post_mix': g_post_mix,
         'g_pre_ffn': g_pre_ffn, 'g_post_ffn': g_post_ffn, 'w_in_even': w_in_even, 'b_gate_a': b_gate_a,
         'w_out_even': w_out_even, 'w_in_odd': w_in_odd, 'w_lb': w_lb, 'g_out_d': g_out_d,
         'w_out_odd': w_out_odd, 'w_ffn_gate': w_ffn_gate, 'w_ffn_up': w_ffn_up, 'w_ffn_down': w_ffn_down}
    y_prompt, ev_p, od_p = trunk(x_prompt, c_prompt, True, None, None, P)
    y_sample, ev_s, od_s = trunk(
        x_sample, c_sample, False,
        (state_mlstm_C, state_mlstm_n, state_mlstm_m, cache_swa_w128, cache_swa_w512, cache_swa_w2048),
        (cache_moba_kv, page_table, state_hgrn_S), P)
    mlstm_C_p, mlstm_n_p, mlstm_m_p, swa_w128_p, swa_w512_p, swa_w2048_p = ev_p
    mlstm_C_s, mlstm_n_s, mlstm_m_s, swa_w128_s, swa_w512_s, swa_w2048_s = ev_s
    moba_kv_p, hgrn_S_p = od_p
    moba_kv_s, hgrn_S_s = od_s
    return (y_prompt, y_sample, mlstm_C_p, mlstm_n_p, mlstm_m_p, mlstm_C_s, mlstm_n_s, mlstm_m_s,
            swa_w128_p, swa_w512_p, swa_w2048_p, swa_w128_s, swa_w512_s, swa_w2048_s,
            moba_kv_p, moba_kv_s, hgrn_S_p, hgrn_S_s)
```

```python
import functools

import jax
import jax.numpy as jnp
from jax import lax
from jax.experimental import pallas as pl
from jax.experimental.pallas import tpu as pltpu

F32 = jnp.float32
BF16 = jnp.bfloat16
HI = lax.Precision.HIGHEST

EPS = 1e-6
NEG = -1e30

H_A, DK_A, DV_A, CHUNK_A = 4, 128, 128, 128
SWA_GROUPS = ((128, 1), (512, 4), (2048, 16))
H_B, DH_B, SWA_BAND = 4, 64, 128
GW_B = H_B * DH_B
H_C, DH_C, MOBA_BLOCK, MOBA_TOPK = 8, 64, 256, 3
H_D, DK_D, DV_D, CHUNK_D, SUB_D = 4, 128, 128, 64, 16

LANES = 128
VMEM_LIMIT = 56 << 20


def _params(sem, big=False):
    return pltpu.CompilerParams(dimension_semantics=sem,
                                vmem_limit_bytes=VMEM_LIMIT if big else None)


def _dot(a, b, hi=False):
    return jnp.dot(a, b, preferred_element_type=F32, precision=HI if hi else None)


def _dot_nt(a, b, hi=False):
    return lax.dot_general(a, b, (((1,), (1,)), ((), ())), preferred_element_type=F32,
                           precision=HI if hi else None)


def _dot_tn(a, b, hi=False):
    return lax.dot_general(a, b, (((0,), (0,)), ((), ())), preferred_element_type=F32,
                           precision=HI if hi else None)


def _rms(x, g):
    return x * lax.rsqrt(jnp.mean(x * x, axis=-1, keepdims=True) + EPS) * g


def _sigmoid(x):
    return 1.0 / (1.0 + jnp.exp(-x))


def _silu(x):
    return x * _sigmoid(x)


def _log_sigmoid(x):
    return jnp.minimum(x, 0.0) - jnp.log(1.0 + jnp.exp(-jnp.abs(x)))


def _col_of_row(row):
    n = row.shape[-1]
    eye = lax.broadcasted_iota(jnp.int32, (n, n), 0) == lax.broadcasted_iota(jnp.int32, (n, n), 1)
    return jnp.sum(jnp.where(eye, row, 0.0), axis=1, keepdims=True)


def _ada_body(c_ref, w_ref, b_ref, o_ref):
    o_ref[0] = _dot(_silu(c_ref[...]), w_ref[0], hi=True) + b_ref[0]


def ada_all(c, w_ada, b_ada, tn=768):
    depth, d, n = w_ada.shape
    r = c.shape[0]
    return pl.pallas_call(
        _ada_body,
        grid=(depth, n // tn),
        in_specs=[pl.BlockSpec((r, d), lambda l, j: (0, 0)),
                  pl.BlockSpec((1, d, tn), lambda l, j: (l, 0, j)),
                  pl.BlockSpec((1, 1, tn), lambda l, j: (l, 0, j))],
        out_specs=pl.BlockSpec((1, r, tn), lambda l, j: (l, 0, j)),
        out_shape=jax.ShapeDtypeStruct((depth, r, n), F32),
        compiler_params=_params(("parallel", "parallel")),
        name="ada",
    )(c, w_ada, b_ada.reshape(depth, 1, n))


def _proj_body(x_ref, g_ref, sc_ref, sh_ref, w_ref, *o_refs, splits, hi, cw):
    h = _rms(x_ref[0], g_ref[...]) * (1.0 + sc_ref[0]) + sh_ref[0]
    hc = h if hi else h.astype(BF16)
    off = 0
    for o_ref, n in zip(o_refs, splits):
        for c0 in range(0, n, cw):
            c1 = min(c0 + cw, n)
            o_ref[0, :, c0:c1] = _dot(hc, w_ref[:, off + c0:off + c1], hi)
        off += n


def _mod_spec(mod, tm):
    if mod.shape[1] == 1:
        return pl.BlockSpec((1, 1, mod.shape[2]), lambda gi, ri: (gi, 0, 0))
    return pl.BlockSpec((1, tm, mod.shape[2]), lambda gi, ri: (gi, ri, 0))


def norm_mod_proj(x, g, sc, sh, w, splits, hi, tm):
    G, R, D = x.shape
    tm = min(tm, R)
    n_all = sum(splits)
    return pl.pallas_call(
        functools.partial(_proj_body, splits=tuple(splits), hi=hi, cw=512),
        grid=(G, R // tm),
        in_specs=[pl.BlockSpec((1, tm, D), lambda gi, ri: (gi, ri, 0)),
                  pl.BlockSpec((1, D), lambda gi, ri: (0, 0)),
                  _mod_spec(sc, tm), _mod_spec(sh, tm),
                  pl.BlockSpec((D, n_all), lambda gi, ri: (0, 0))],
        out_specs=[pl.BlockSpec((1, tm, n), lambda gi, ri: (gi, ri, 0)) for n in splits],
        out_shape=[jax.ShapeDtypeStruct((G, R, n), F32) for n in splits],
        compiler_params=_params(("parallel", "parallel"), big=True),
        name="proj",
    )(x, g.reshape(1, D), sc, sh, w)


def _mixout_body(*refs, n_parts, swa, hi):
    x_ref, g1_ref, gp_ref = refs[:3]
    part_refs = refs[3:3 + n_parts]
    pos = 3 + n_parts
    if swa:
        o_refs = refs[pos:pos + 3]
        l_refs = refs[pos + 3:pos + 6]
        pos += 6
    w_ref, out_ref = refs[pos], refs[pos + 1]

    def cast(a):
        return a if hi else a.astype(BF16)

    acc = None
    off = 0
    for p_ref in part_refs:
        a = p_ref[0]
        k = a.shape[-1]
        t = _dot(cast(a), w_ref[off:off + k, :], hi)
        acc = t if acc is None else acc + t
        off += k
    if swa:
        ls = [l_ref[0] for l_ref in l_refs]
        mx = jnp.maximum(jnp.maximum(ls[0], ls[1]), ls[2])
        es = [jnp.exp(l - mx) for l in ls]
        hb = (es[0] * o_refs[0][0] + es[1] * o_refs[1][0] + es[2] * o_refs[2][0]) / (es[0] + es[1] + es[2])
        acc = acc + _dot(cast(hb), w_ref[off:off + hb.shape[-1], :], hi)
    out_ref[0] = x_ref[0] + g1_ref[0] * _rms(acc, gp_ref[...])


def mix_out(x, g1, gpost, parts, swa_parts, w, hi, tm):
    G, R, D = x.shape
    tm = min(tm, R)
    row = lambda a: pl.BlockSpec((1, tm, a.shape[2]), lambda gi, ri: (gi, ri, 0))
    ops = [x, g1, gpost.reshape(1, D)] + list(parts)
    specs = [row(x), _mod_spec(g1, tm), pl.BlockSpec((1, D), lambda gi, ri: (0, 0))] + [row(p) for p in parts]
    swa = swa_parts is not None
    if swa:
        ops += list(swa_parts[0]) + list(swa_parts[1])
        specs += [row(a) for a in ops[-6:]]
    ops.append(w)
    specs.append(pl.BlockSpec(w.shape, lambda gi, ri: (0, 0)))
    return pl.pallas_call(
        functools.partial(_mixout_body, n_parts=len(parts), swa=swa, hi=hi),
        grid=(G, R // tm),
        in_specs=specs,
        out_specs=row(x),
        out_shape=jax.ShapeDtypeStruct(x.shape, F32),
        compiler_params=_params(("parallel", "parallel"), big=True),
        name="mix_out",
    )(*ops)


def _ffn_body(x_ref, gpre_ref, sc_ref, sh_ref, g2_ref, gpost_ref, wg_ref, wu_ref, wd_ref, out_ref,
              h_scr, acc_scr, *, hi):
    k = pl.program_id(2)

    @pl.when(k == 0)
    def _():
        h = _rms(x_ref[0], gpre_ref[...]) * (1.0 + sc_ref[0]) + sh_ref[0]
        h_scr[...] = h.astype(h_scr.dtype)
        acc_scr[...] = jnp.zeros_like(acc_scr)

    h = h_scr[...]
    a = _silu(_dot(h, wg_ref[...], hi)) * _dot(h, wu_ref[...], hi)
    acc_scr[...] += _dot(a.astype(h.dtype), wd_ref[...], hi)

    @pl.when(k == pl.num_programs(2) - 1)
    def _():
        out_ref[0] = x_ref[0] + g2_ref[0] * _rms(acc_scr[...], gpost_ref[...])


def ffn(x, gpre, sc, sh, g2, gpost, wg, wu, wd, hi, tm, tf):
    G, R, D = x.shape
    FF = wg.shape[1]
    tm = min(tm, R)
    mspec = lambda m: (pl.BlockSpec((1, 1, D), lambda gi, ri, k: (gi, 0, 0)) if m.shape[1] == 1
                       else pl.BlockSpec((1, tm, D), lambda gi, ri, k: (gi, ri, 0)))
    vec = pl.BlockSpec((1, D), lambda gi, ri, k: (0, 0))
    return pl.pallas_call(
        functools.partial(_ffn_body, hi=hi),
        grid=(G, R // tm, FF // tf),
        in_specs=[pl.BlockSpec((1, tm, D), lambda gi, ri, k: (gi, ri, 0)), vec, mspec(sc), mspec(sh),
                  mspec(g2), vec,
                  pl.BlockSpec((D, tf), lambda gi, ri, k: (0, k)),
                  pl.BlockSpec((D, tf), lambda gi, ri, k: (0, k)),
                  pl.BlockSpec((tf, D), lambda gi, ri, k: (k, 0))],
        out_specs=pl.BlockSpec((1, tm, D), lambda gi, ri, k: (gi, ri, 0)),
        out_shape=jax.ShapeDtypeStruct(x.shape, F32),
        scratch_shapes=[pltpu.VMEM((tm, D), F32 if hi else BF16), pltpu.VMEM((tm, D), F32)],
        compiler_params=_params(("parallel", "parallel", "arbitrary"), big=True),
        name="ffn",
    )(x, gpre.reshape(1, D), sc, sh, g2, gpost.reshape(1, D), wg, wu, wd)


def _mlstm_body(a_ref, g_ref, bg_ref, h_ref, C_ref, n_ref, m_ref):
    L = CHUNK_A
    HD = H_A * DK_A

    @pl.when(pl.program_id(1) == 0)
    def _():
        C_ref[...] = jnp.zeros_like(C_ref)
        n_ref[...] = jnp.zeros_like(n_ref)
        m_ref[...] = jnp.zeros_like(m_ref)

    lane = lax.broadcasted_iota(jnp.int32, (L, LANES), 1)
    ri = lax.broadcasted_iota(jnp.int32, (L, L), 0)
    ci = lax.broadcasted_iota(jnp.int32, (L, L), 1)
    causal = ci <= ri
    pre = g_ref[0] + bg_ref[...]
    gates = jnp.where(lane >= H_A, _log_sigmoid(pre), pre)
    bcum = _dot(causal.astype(F32), gates, hi=True)
    gates_t = gates.T
    bcum_t = bcum.T
    m_all = m_ref[0]
    m_lane = lax.broadcasted_iota(jnp.int32, (1, LANES), 1)
    m_out = m_all
    for h in range(H_A):
        q = a_ref[0, :, h * DK_A:(h + 1) * DK_A]
        k = a_ref[0, :, HD + h * DK_A:HD + (h + 1) * DK_A] * (DK_A ** -0.5)
        v = a_ref[0, :, 2 * HD + h * DV_A:2 * HD + (h + 1) * DV_A]
        og = a_ref[0, :, 3 * HD + h * DV_A:3 * HD + (h + 1) * DV_A]
        qb, kb, vb = q.astype(BF16), k.astype(BF16), v.astype(BF16)
        ig_row = gates_t[h:h + 1, :]
        b_col = bcum[:, H_A + h:H_A + h + 1]
        b_row = bcum_t[H_A + h:H_A + h + 1, :]
        m = m_all[:, h:h + 1]
        C = C_ref[0, h]
        n_row = n_ref[0, h:h + 1, :]

        dmat = jnp.where(causal, b_col - b_row + ig_row, NEG)
        inter = b_col + m
        mt = jnp.maximum(inter, jnp.max(dmat, axis=1, keepdims=True))
        wmat = jnp.exp(dmat - mt)
        winter = jnp.exp(inter - mt)
        s = _dot_nt(qb, kb) * wmat
        num = _dot(s.astype(BF16), vb) + winter * _dot(qb, C.astype(BF16))
        den = jnp.sum(s, axis=1, keepdims=True) + winter * jnp.sum(q * n_row, axis=1, keepdims=True)
        hh = num / jnp.maximum(jnp.abs(den), jnp.exp(-mt))
        h_ref[0, :, h * DV_A:(h + 1) * DV_A] = _sigmoid(og) * hh

        bl = b_row[:, L - 1:L]
        gl = bl - b_row + ig_row
        m_new = jnp.maximum(bl + m, jnp.max(gl, axis=1, keepdims=True))
        ws = jnp.exp(gl - m_new)
        wc = jnp.exp(bl + m - m_new)
        C_ref[0, h] = wc * C + _dot((k.T * ws).astype(BF16), vb)
        n_ref[0, h:h + 1, :] = wc * n_row + _dot(jnp.broadcast_to(ws, (8, L)), k, hi=True)[0:1]
        m_out = jnp.where(m_lane == h, m_new, m_out)
    m_ref[0] = m_out


def mlstm_prompt(a, g, bg):
    B, T, _ = a.shape
    L = CHUNK_A
    return pl.pallas_call(
        _mlstm_body,
        grid=(B, T // L),
        in_specs=[pl.BlockSpec((1, L, a.shape[2]), lambda b, c: (b, c, 0)),
                  pl.BlockSpec((1, L, LANES), lambda b, c: (b, c, 0)),
                  pl.BlockSpec((1, LANES), lambda b, c: (0, 0))],
        out_specs=[pl.BlockSpec((1, L, H_A * DV_A), lambda b, c: (b, c, 0)),
                   pl.BlockSpec((1, H_A, DK_A, DV_A), lambda b, c: (b, 0, 0, 0)),
                   pl.BlockSpec((1, H_A, DK_A), lambda b, c: (b, 0, 0)),
                   pl.BlockSpec((1, 1, LANES), lambda b, c: (b, 0, 0))],
        out_shape=[jax.ShapeDtypeStruct((B, T, H_A * DV_A), F32),
                   jax.ShapeDtypeStruct((B, H_A, DK_A, DV_A), F32),
                   jax.ShapeDtypeStruct((B, H_A, DK_A), F32),
                   jax.ShapeDtypeStruct((B, 1, LANES), F32)],
        compiler_params=_params(("parallel", "arbitrary")),
        name="mlstm_prompt",
    )(a, g, bg)


def _mlstm_step_body(a_ref, g_ref, bg_ref, C0_ref, n0_ref, m0_ref, h_ref, C_ref, n_ref, m_ref):
    HD = H_A * DK_A
    pre = g_ref[0] + bg_ref[...]
    lane = lax.broadcasted_iota(jnp.int32, (1, LANES), 1)
    gates = jnp.where(lane >= H_A, _log_sigmoid(pre), pre)
    m_all = m0_ref[0]
    m_out = m_all
    for h in range(H_A):
        q = a_ref[0, :, h * DK_A:(h + 1) * DK_A]
        k = a_ref[0, :, HD + h * DK_A:HD + (h + 1) * DK_A] * (DK_A ** -0.5)
        v = a_ref[0, :, 2 * HD + h * DV_A:2 * HD + (h + 1) * DV_A]
        og = a_ref[0, :, 3 * HD + h * DV_A:3 * HD + (h + 1) * DV_A]
        ig = gates[:, h:h + 1]
        lf = gates[:, H_A + h:H_A + h + 1]
        m = m_all[:, h:h + 1]
        C = C0_ref[0, h]
        n_row = n0_ref[0, h:h + 1, :]
        inter = lf + m
        mt = jnp.maximum(inter, ig)
        s = jnp.sum(q * k, axis=1, keepdims=True) * jnp.exp(ig - mt)
        winter = jnp.exp(inter - mt)
        qC = _dot(jnp.broadcast_to(q, (8, DK_A)), C, hi=True)[0:1]
        num = s * v + winter * qC
        den = s + winter * jnp.sum(q * n_row, axis=1, keepdims=True)
        hh = num / jnp.maximum(jnp.abs(den), jnp.exp(-mt))
        h_ref[0, :, h * DV_A:(h + 1) * DV_A] = _sigmoid(og) * hh
        m_new = jnp.maximum(inter, ig)
        ws = jnp.exp(ig - m_new)
        wc = jnp.exp(inter - m_new)
        C_ref[0, h] = wc * C + (ws * _col_of_row(k)) * v
        n_ref[0, h:h + 1, :] = wc * n_row + ws * k
        m_out = jnp.where(lane == h, m_new, m_out)
    m_ref[0] = m_out


def mlstm_step(a, g, bg, C0, n0, m0):
    B = a.shape[0]
    r3 = lambda w: pl.BlockSpec((1, 1, w), lambda b: (b, 0, 0))
    cs = pl.BlockSpec((1, H_A, DK_A, DV_A), lambda b: (b, 0, 0, 0))
    ns = pl.BlockSpec((1, H_A, DK_A), lambda b: (b, 0, 0))
    return pl.pallas_call(
        _mlstm_step_body,
        grid=(B,),
        in_specs=[r3(a.shape[2]), r3(LANES), pl.BlockSpec((1, LANES), lambda b: (0, 0)), cs, ns, r3(LANES)],
        out_specs=[r3(H_A * DV_A), cs, ns, r3(LANES)],
        out_shape=[jax.ShapeDtypeStruct((B, 1, H_A * DV_A), F32),
                   jax.ShapeDtypeStruct(C0.shape, F32), jax.ShapeDtypeStruct(n0.shape, F32),
                   jax.ShapeDtypeStruct((B, 1, LANES), F32)],
        compiler_params=_params(("parallel",)),
        name="mlstm_step",
    )(a, g, bg, C0, n0, m0)


def _swa_body(q_ref, kc_ref, kp_ref, vc_ref, vp_ref, o_ref, l_ref):
    Q = SWA_BAND
    qi = lax.broadcasted_iota(jnp.int32, (Q, Q), 0)
    kj = lax.broadcasted_iota(jnp.int32, (Q, Q), 1)
    mask_c = kj <= qi
    mask_p = jnp.logical_and(kj >= qi, pl.program_id(2) > 0)
    outs, lses = [], []
    for h in range(H_B):
        sl = slice(h * DH_B, (h + 1) * DH_B)
        qh = (q_ref[0, :, sl] * (DH_B ** -0.5)).astype(BF16)
        sc = jnp.where(mask_c, _dot_nt(qh, kc_ref[0, :, sl].astype(BF16)), NEG)
        sp = jnp.where(mask_p, _dot_nt(qh, kp_ref[0, :, sl].astype(BF16)), NEG)
        mx = jnp.maximum(jnp.max(sc, axis=1, keepdims=True), jnp.max(sp, axis=1, keepdims=True))
        pc = jnp.exp(sc - mx)
        pp = jnp.exp(sp - mx)
        den = jnp.sum(pc, axis=1, keepdims=True) + jnp.sum(pp, axis=1, keepdims=True)
        o = _dot(pc.astype(BF16), vc_ref[0, :, sl].astype(BF16)) + _dot(pp.astype(BF16), vp_ref[0, :, sl].astype(BF16))
        outs.append(o / den)
        lses.append(jnp.broadcast_to(mx + jnp.log(den), (Q, DH_B)))
    o_ref[0] = jnp.concatenate(outs, axis=1)
    l_ref[0] = jnp.concatenate(lses, axis=1)


def swa_prompt(qkv, g, dil):
    B, T, W = qkv.shape
    nper = W // GW_B
    ng = nper // 3
    ls = T // dil
    nb = ls // SWA_BAND
    x = qkv.reshape(B, ls, dil * W)
    cur = lambda part: pl.BlockSpec((1, SWA_BAND, GW_B), lambda b, r, n: (b, n, r * nper + part * ng + g))
    prev = lambda part: pl.BlockSpec((1, SWA_BAND, GW_B),
                                     lambda b, r, n: (b, jnp.maximum(n - 1, 0), r * nper + part * ng + g))
    ospec = pl.BlockSpec((1, SWA_BAND, GW_B), lambda b, r, n: (b, n, r))
    o, l = pl.pallas_call(
        _swa_body,
        grid=(B, dil, nb),
        in_specs=[cur(0), cur(1), prev(1), cur(2), prev(2)],
        out_specs=[ospec, ospec],
        out_shape=[jax.ShapeDtypeStruct((B, ls, dil * GW_B), F32)] * 2,
        compiler_params=_params(("parallel", "parallel", "parallel")),
        name="swa_prompt_d%d" % dil,
    )(x, x, x, x, x)
    return o.reshape(B, T, GW_B), l.reshape(B, T, GW_B)


def _swa_step_body(q_ref, k_ref, v_ref, rows_ref, buf_ref, o_ref, l_ref, nb_ref):
    wb = buf_ref.shape[1]
    J = rows_ref.shape[1]
    q = q_ref[0]
    kn = k_ref[0]
    vn = v_ref[0]
    rows = rows_ref[0]
    kb = rows[:, :GW_B]
    vb = rows[:, GW_B:]
    lane = lax.broadcasted_iota(jnp.int32, (1, GW_B), 1)
    prod = kb * q
    prod_n = kn * q
    p_full = jnp.zeros((J, GW_B), F32)
    p_self_full = jnp.zeros((1, GW_B), F32)
    den_full = jnp.zeros((1, GW_B), F32)
    lse_full = jnp.zeros((1, GW_B), F32)
    for h in range(H_B):
        hm = jnp.logical_and(lane >= h * DH_B, lane < (h + 1) * DH_B)
        s = jnp.sum(jnp.where(hm, prod, 0.0), axis=1, keepdims=True) * (DH_B ** -0.5)
        s_self = jnp.sum(jnp.where(hm, prod_n, 0.0), axis=1, keepdims=True) * (DH_B ** -0.5)
        mx = jnp.maximum(jnp.max(s, axis=0, keepdims=True), s_self)
        p = jnp.exp(s - mx)
        p_self = jnp.exp(s_self - mx)
        den = jnp.sum(p, axis=0, keepdims=True) + p_self
        p_full = jnp.where(hm, p, p_full)
        p_self_full = jnp.where(hm, p_self, p_self_full)
        den_full = jnp.where(hm, den, den_full)
        lse_full = jnp.where(hm, mx + jnp.log(den), lse_full)
    o_ref[0] = (jnp.sum(p_full * vb, axis=0, keepdims=True) + p_self_full * vn) / den_full
    l_ref[0] = lse_full
    step = 128
    for r0 in range(0, wb, step):
        r1 = min(r0 + step, wb - 1)
        if r1 > r0:
            nb_ref[0, r0:r1, :] = buf_ref[0, r0 + 1:r1 + 1, :]
    nb_ref[0, wb - 1:wb, :] = jnp.concatenate([kn, vn], axis=1)


def swa_step(qkv, g, buf, dil):
    B, wb, W = buf.shape
    ng = qkv.shape[2] // GW_B // 3
    col = lambda part: pl.BlockSpec((1, 1, GW_B), lambda b: (b, 0, part * ng + g))
    o3 = pl.BlockSpec((1, 1, GW_B), lambda b: (b, 0, 0))
    bspec = pl.BlockSpec((1, wb, W), lambda b: (b, 0, 0))
    rspec = pl.BlockSpec((1, wb // dil, W), lambda b: (b, 0, 0))
    return pl.pallas_call(
        _swa_step_body,
        grid=(B,),
        in_specs=[col(0), col(1), col(2), rspec, bspec],
        out_specs=[o3, o3, bspec],
        out_shape=[jax.ShapeDtypeStruct((B, 1, GW_B), F32)] * 2 + [jax.ShapeDtypeStruct(buf.shape, F32)],
        compiler_params=_params(("parallel",), big=True),
        name="swa_step_d%d" % dil,
    )(qkv, qkv, qkv, buf.reshape(B, wb // dil, dil * W), buf)


AUG = 2 * DH_C


def _moba_kprep_body(k_ref, km_ref, ka_ref, *, nblk):
    k = k_ref[0]
    km_ref[0, 0] = jnp.mean(k, axis=0, keepdims=True)
    lane = lax.broadcasted_iota(jnp.int32, (MOBA_BLOCK, DH_C), 1)
    onehot = jnp.where(lane == pl.program_id(1), 1.0, 0.0).astype(BF16)
    pieces = []
    for h in range(H_C):
        pieces += [k[:, h * DH_C:(h + 1) * DH_C].astype(BF16), onehot]
    ka_ref[0] = jnp.concatenate(pieces, axis=1)


def _moba_gate_body(q_ref, km_ref, qa_ref, *, nblk):
    own = pl.program_id(1)
    lane = lax.broadcasted_iota(jnp.int32, (MOBA_BLOCK, nblk), 1)
    lane_f = lane.astype(F32)
    past = lane < own
    pieces = []
    for h in range(H_C):
        sl = slice(h * DH_C, (h + 1) * DH_C)
        qh = q_ref[0, :, sl]
        g = jnp.where(past, _dot_nt(qh, km_ref[0, :, sl], hi=True), NEG)
        sel = lane == own
        for _ in range(MOBA_TOPK):
            mx = jnp.max(g, axis=1, keepdims=True)
            first = jnp.min(jnp.where(g == mx, lane_f, float(nblk)), axis=1, keepdims=True)
            pick = lane_f == first
            sel = jnp.logical_or(sel, jnp.logical_and(pick, past))
            g = jnp.where(pick, -jnp.inf, g)
        bias = jnp.where(sel, 0.0, NEG)
        pieces += [(qh * (DH_C ** -0.5)).astype(BF16), bias.astype(BF16),
                   jnp.zeros((MOBA_BLOCK, AUG - DH_C - nblk), BF16)]
    qa_ref[0] = jnp.concatenate(pieces, axis=1)


def _moba_attn_body(qa_ref, ka_ref, v_ref, o_ref):
    i = pl.program_id(2)
    TQ = MOBA_BLOCK
    ri = lax.broadcasted_iota(jnp.int32, (TQ, TQ), 0)
    ci = lax.broadcasted_iota(jnp.int32, (TQ, TQ), 1)
    qs = [qa_ref[0, :, hh * AUG:(hh + 1) * AUG] for hh in range(2)]

    def block(n):
        r0 = pl.multiple_of(n * TQ, TQ)
        return ka_ref[0, pl.ds(r0, TQ), :], v_ref[0, pl.ds(r0, TQ), :].astype(BF16)

    kk, vv = block(i)
    carry = []
    for hh in range(2):
        s = jnp.where(ci <= ri, _dot_nt(qs[hh], kk[:, hh * AUG:(hh + 1) * AUG]), NEG)
        m = jnp.max(s, axis=1, keepdims=True)
        p = jnp.exp(s - m)
        carry += [m, jnp.sum(p, axis=1, keepdims=True), _dot(p.astype(BF16), vv)]

    def body(n, carry):
        kk, vv = block(n)
        out = []
        for hh in range(2):
            m, l, acc = carry[3 * hh:3 * hh + 3]
            s = _dot_nt(qs[hh], kk[:, hh * AUG:(hh + 1) * AUG])
            m_new = jnp.maximum(m, jnp.max(s, axis=1, keepdims=True))
            alpha = jnp.exp(m - m_new)
            p = jnp.exp(s - m_new)
            out += [m_new, alpha * l + jnp.sum(p, axis=1, keepdims=True), alpha * acc + _dot(p.astype(BF16), vv)]
        return tuple(out)

    m0, l0, a0, m1, l1, a1 = lax.fori_loop(0, i, body, tuple(carry))
    lane = lax.broadcasted_iota(jnp.int32, (TQ, 2 * DH_C), 1)
    o_ref[0] = jnp.where(lane < DH_C, a0 / l0, a1 / l1)


def moba_prompt(q, kv):
    B, T, W = q.shape
    nblk = T // MOBA_BLOCK
    assert nblk <= AUG - DH_C
    km, ka = pl.pallas_call(
        functools.partial(_moba_kprep_body, nblk=nblk),
        grid=(B, nblk),
        in_specs=[pl.BlockSpec((1, MOBA_BLOCK, W), lambda b, n: (b, n, 0))],
        out_specs=[pl.BlockSpec((1, 1, 1, W), lambda b, n: (b, n, 0, 0)),
                   pl.BlockSpec((1, MOBA_BLOCK, H_C * AUG), lambda b, n: (b, n, 0))],
        out_shape=[jax.ShapeDtypeStruct((B, nblk, 1, W), F32),
                   jax.ShapeDtypeStruct((B, T, H_C * AUG), BF16)],
        compiler_params=_params(("parallel", "parallel")),
        name="moba_kprep",
    )(kv)
    qa = pl.pallas_call(
        functools.partial(_moba_gate_body, nblk=nblk),
        grid=(B, nblk),
        in_specs=[pl.BlockSpec((1, MOBA_BLOCK, W), lambda b, i: (b, i, 0)),
                  pl.BlockSpec((1, nblk, W), lambda b, i: (b, 0, 0))],
        out_specs=pl.BlockSpec((1, MOBA_BLOCK, H_C * AUG), lambda b, i: (b, i, 0)),
        out_shape=jax.ShapeDtypeStruct((B, T, H_C * AUG), BF16),
        compiler_params=_params(("parallel", "parallel")),
        name="moba_gate",
    )(q, km.reshape(B, nblk, W))
    npair = H_C // 2
    return pl.pallas_call(
        _moba_attn_body,
        grid=(B, npair, nblk),
        in_specs=[pl.BlockSpec((1, MOBA_BLOCK, 2 * AUG), lambda b, p, i: (b, i, p)),
                  pl.BlockSpec((1, T, 2 * AUG), lambda b, p, i: (b, 0, p)),
                  pl.BlockSpec((1, T, 2 * DH_C), lambda b, p, i: (b, 0, npair + p))],
        out_specs=pl.BlockSpec((1, MOBA_BLOCK, 2 * DH_C), lambda b, p, i: (b, i, p)),
        out_shape=jax.ShapeDtypeStruct((B, T, W), F32),
        compiler_params=_params(("parallel", "parallel", "arbitrary"), big=True),
        name="moba_attn",
    )(qa, ka, kv)


PAGES_PER_STEP = 16


def _moba_step_gate_body(pt_ref, q_ref, *refs, pages_per_block, n_past):
    page_refs = refs[:PAGES_PER_STEP]
    sel_ref, g_scr = refs[PAGES_PER_STEP], refs[PAGES_PER_STEP + 1]
    s = pl.program_id(1)
    q = q_ref[0]
    W = q.shape[1]
    per_step = PAGES_PER_STEP // pages_per_block
    rows_per_block = pages_per_block * page_refs[0].shape[1]
    for j in range(per_step):
        tot = None
        for i in range(pages_per_block):
            t = jnp.sum(page_refs[j * pages_per_block + i][0], axis=0, keepdims=True)
            tot = t if tot is None else tot + t
        g_scr[pl.ds(s * per_step + j, 1), :] = q * (tot / rows_per_block)

    @pl.when(s == pl.num_programs(1) - 1)
    def _():
        seg = (lax.broadcasted_iota(jnp.int32, (W, LANES), 0) // DH_C ==
               lax.broadcasted_iota(jnp.int32, (W, LANES), 1)).astype(F32)
        g = _dot(g_scr[...], seg, hi=True)
        row_f = lax.broadcasted_iota(jnp.int32, (n_past, LANES), 0).astype(F32)
        picks = []
        for _ in range(MOBA_TOPK):
            mx = jnp.max(g, axis=0, keepdims=True)
            first = jnp.min(jnp.where(g == mx, row_f, float(n_past)), axis=0, keepdims=True)
            picks.append(first)
            g = jnp.where(row_f == first, -jnp.inf, g)
        picks.append(jnp.zeros((8 - MOBA_TOPK, LANES), F32))
        sel_ref[0] = jnp.concatenate(picks, axis=0).astype(jnp.int32)


def _moba_step_attn_body(pt_ref, sel_ref, q_ref, kn_ref, vn_ref, *refs, n_pages):
    k_refs = refs[:2 * n_pages]
    v_refs = refs[2 * n_pages:4 * n_pages]
    o_ref = refs[4 * n_pages]
    q = q_ref[0]
    kn = kn_ref[0]
    vn = vn_ref[0]
    rows = k_refs[0].shape[1]
    lane = lax.broadcasted_iota(jnp.int32, (1, 2 * DH_C), 1)
    scale = DH_C ** -0.5
    acc = jnp.zeros((1, 2 * DH_C), F32)
    den_full = jnp.zeros((1, 2 * DH_C), F32)
    for hh in range(2):
        hm = jnp.logical_and(lane >= hh * DH_C, lane < (hh + 1) * DH_C)
        ss = [jnp.sum(jnp.where(hm, k_refs[hh * n_pages + j][0] * q, 0.0), axis=1, keepdims=True) * scale
              for j in range(n_pages)]
        s_self = jnp.sum(jnp.where(hm, kn * q, 0.0), axis=1, keepdims=True) * scale
        mx = s_self
        for sj in ss:
            mx = jnp.maximum(mx, jnp.max(sj, axis=0, keepdims=True))
        p_self = jnp.exp(s_self - mx)
        den = p_self
        o = p_self * vn
        for j, sj in enumerate(ss):
            p = jnp.exp(sj - mx)
            den = den + jnp.sum(p, axis=0, keepdims=True)
            o = o + jnp.sum(p * v_refs[hh * n_pages + j][0], axis=0, keepdims=True)
        acc = jnp.where(hm, o, acc)
        den_full = jnp.where(hm, den, den_full)
    o_ref[0] = acc / den_full


def moba_step(q, kv_new, pool, page_table):
    B, _, W = q.shape
    n_pool, page, _ = pool.shape
    n_pt = page_table.shape[1]
    ppb = MOBA_BLOCK // page
    n_past = n_pt // ppb
    assert n_past >= MOBA_TOPK and PAGES_PER_STEP % ppb == 0 and n_pt % PAGES_PER_STEP == 0
    pt_flat = page_table.reshape(-1)
    sel = pl.pallas_call(
        functools.partial(_moba_step_gate_body, pages_per_block=ppb, n_past=n_past),
        grid_spec=pltpu.PrefetchScalarGridSpec(
            num_scalar_prefetch=1,
            grid=(B, n_pt // PAGES_PER_STEP),
            in_specs=[pl.BlockSpec((1, 1, W), lambda b, s, pt: (b, 0, 0))] +
                     [pl.BlockSpec((1, page, W),
                                   functools.partial(lambda b, s, pt, i: (pt[b * n_pt + s * PAGES_PER_STEP + i], 0, 0), i=i))
                      for i in range(PAGES_PER_STEP)],
            out_specs=pl.BlockSpec((1, 8, LANES), lambda b, s, pt: (b, 0, 0)),
            scratch_shapes=[pltpu.VMEM((n_past, W), F32)]),
        out_shape=jax.ShapeDtypeStruct((B, 8, LANES), jnp.int32),
        compiler_params=_params(("parallel", "arbitrary")),
        name="moba_step_gate",
    )(pt_flat, q, *([pool] * PAGES_PER_STEP))
    sel_flat = sel[:, :MOBA_TOPK, :H_C].reshape(-1)
    npair = H_C // 2
    n_pages = MOBA_TOPK * ppb

    def page_map(b, p, pt, sl, hh, r, i, off):
        blk = sl[(b * MOBA_TOPK + r) * H_C + 2 * p + hh]
        return (pt[b * n_pt + blk * ppb + i], 0, off + p)

    def page_specs(off):
        return [pl.BlockSpec((1, page, 2 * DH_C), functools.partial(page_map, hh=hh, r=r, i=i, off=off))
                for hh in range(2) for r in range(MOBA_TOPK) for i in range(ppb)]

    pair = lambda off: pl.BlockSpec((1, 1, 2 * DH_C), lambda b, p, pt, sl: (b, 0, off + p))
    return pl.pallas_call(
        functools.partial(_moba_step_attn_body, n_pages=n_pages),
        grid_spec=pltpu.PrefetchScalarGridSpec(
            num_scalar_prefetch=2,
            grid=(B, npair),
            in_specs=[pair(0), pair(0), pair(npair)] + page_specs(0) + page_specs(npair),
            out_specs=pair(0)),
        out_shape=jax.ShapeDtypeStruct((B, 1, W), F32),
        compiler_params=_params(("parallel", "parallel")),
        name="moba_step_attn",
    )(pt_flat, sel_flat, q, kv_new, kv_new, *([pool] * (4 * n_pages)))


def _hgrn_gates(f, wlb_ref, layer):
    w = wlb_ref[...]
    e = jnp.exp(w - jnp.max(w, axis=0, keepdims=True))
    sm = e / jnp.sum(e, axis=0, keepdims=True)
    lb = jnp.sum(sm[1:layer + 1], axis=0, keepdims=True)
    sg = _sigmoid(f)
    return jnp.log(lb + (1.0 - lb) * sg), (1.0 - lb) * (1.0 - sg)


def _hgrn_body(d_ref, wlb_ref, go_ref, o_ref, S_ref, *, layer):
    L, SB = CHUNK_D, SUB_D
    HD = H_D * DK_D

    @pl.when(pl.program_id(1) == 0)
    def _():
        S_ref[...] = jnp.zeros_like(S_ref)

    lf, kd = _hgrn_gates(d_ref[0, :, HD:2 * HD], wlb_ref, layer)
    ri = lax.broadcasted_iota(jnp.int32, (L, L), 0)
    ci = lax.broadcasted_iota(jnp.int32, (L, L), 1)
    b_all = _dot((ci <= ri).astype(F32), lf, hi=True)
    sub_row = lax.broadcasted_iota(jnp.int32, (SB, 1), 0)
    for h in range(H_D):
        sl = slice(h * DK_D, (h + 1) * DK_D)
        vsl = slice(2 * HD + h * DV_D, 2 * HD + (h + 1) * DV_D)
        q = d_ref[0, :, sl]
        v = d_ref[0, :, vsl]
        gd = d_ref[0, :, 3 * HD + h * DV_D:3 * HD + (h + 1) * DV_D]
        b = b_all[:, sl]
        k = kd[:, sl]
        S = S_ref[0, h]
        vb = v.astype(BF16)
        o_inter = _dot((q * jnp.exp(b)).astype(BF16), S.astype(BF16))
        o_blocks = []
        for r0 in range(0, L, SB):
            qI = q[r0:r0 + SB]
            bI = b[r0:r0 + SB]
            oI = o_inter[r0:r0 + SB]
            if r0 > 0:
                ref_b = b[r0 - 1:r0]
                qs = (qI * jnp.exp(bI - ref_b)).astype(BF16)
                ks = (k[:r0] * jnp.exp(ref_b - b[:r0])).astype(BF16)
                oI = oI + _dot(_dot_nt(qs, ks).astype(BF16), vb[:r0])

            for s in range(SB):
                row = r0 + s
                w = jnp.exp(jnp.minimum(bI - b[row:row + 1], 0.0))
                a = jnp.sum(qI * k[row:row + 1] * w, axis=1, keepdims=True)
                oI = oI + jnp.where(sub_row >= s, a, 0.0) * v[row:row + 1]
            o_blocks.append(oI)
        o = jnp.concatenate(o_blocks, axis=0)
        o_ref[0, :, h * DV_D:(h + 1) * DV_D] = _rms(o, go_ref[...]) * _silu(gd)
        bl = b[L - 1:L]
        kw = (k * jnp.exp(bl - b)).astype(BF16)
        S_ref[0, h] = _col_of_row(jnp.exp(bl)) * S + _dot_tn(kw, vb)


def hgrn_prompt(d, w_lb, g_out, layer):
    B, T, W = d.shape
    L = CHUNK_D
    HD = H_D * DK_D
    return pl.pallas_call(
        functools.partial(_hgrn_body, layer=layer),
        grid=(B, T // L),
        in_specs=[pl.BlockSpec((1, L, W), lambda b, c: (b, c, 0)),
                  pl.BlockSpec(w_lb.shape, lambda b, c: (0, 0)),
                  pl.BlockSpec((1, DV_D), lambda b, c: (0, 0))],
        out_specs=[pl.BlockSpec((1, L, H_D * DV_D), lambda b, c: (b, c, 0)),
                   pl.BlockSpec((1, H_D, DK_D, DV_D), lambda b, c: (b, 0, 0, 0))],
        out_shape=[jax.ShapeDtypeStruct((B, T, H_D * DV_D), F32),
                   jax.ShapeDtypeStruct((B, H_D, DK_D, DV_D), F32)],
        compiler_params=_params(("parallel", "arbitrary")),
        name="hgrn_prompt",
    )(d, w_lb, g_out.reshape(1, DV_D))


def _hgrn_step_body(d_ref, wlb_ref, go_ref, S0_ref, o_ref, S_ref, *, layer):
    HD = H_D * DK_D
    lf, kd = _hgrn_gates(d_ref[0, :, HD:2 * HD], wlb_ref, layer)
    eb = jnp.exp(lf)
    for h in range(H_D):
        sl = slice(h * DK_D, (h + 1) * DK_D)
        q = d_ref[0, :, sl]
        v = d_ref[0, :, 2 * HD + h * DV_D:2 * HD + (h + 1) * DV_D]
        gd = d_ref[0, :, 3 * HD + h * DV_D:3 * HD + (h + 1) * DV_D]
        k = kd[:, sl]
        e = eb[:, sl]
        S = S0_ref[0, h]
        a = jnp.sum(q * k, axis=1, keepdims=True)
        o = a * v + _dot(jnp.broadcast_to(q * e, (8, DK_D)), S, hi=True)[0:1]
        o_ref[0, :, h * DV_D:(h + 1) * DV_D] = _rms(o, go_ref[...]) * _silu(gd)
        S_ref[0, h] = _col_of_row(e) * S + _col_of_row(k) * v


def hgrn_step(d, w_lb, g_out, S0, layer):
    B, _, W = d.shape
    ss = pl.BlockSpec((1, H_D, DK_D, DV_D), lambda b: (b, 0, 0, 0))
    return pl.pallas_call(
        functools.partial(_hgrn_step_body, layer=layer),
        grid=(B,),
        in_specs=[pl.BlockSpec((1, 1, W), lambda b: (b, 0, 0)),
                  pl.BlockSpec(w_lb.shape, lambda b: (0, 0)),
                  pl.BlockSpec((1, DV_D), lambda b: (0, 0)), ss],
        out_specs=[pl.BlockSpec((1, 1, H_D * DV_D), lambda b: (b, 0, 0)), ss],
        out_shape=[jax.ShapeDtypeStruct((B, 1, H_D * DV_D), F32), jax.ShapeDtypeStruct(S0.shape, F32)],
        compiler_params=_params(("parallel",)),
        name="hgrn_step",
    )(d, w_lb, g_out.reshape(1, DV_D), S0)


N_A = 4 * H_A * DK_A
N_B = 3 * len(SWA_GROUPS) * GW_B
N_C = H_C * DH_C
N_D = 4 * H_D * DK_D


def _even_weights(w_in):
    gates = jnp.pad(w_in[:, N_A:N_A + 2 * H_A], ((0, 0), (0, LANES - 2 * H_A)))
    return jnp.concatenate([w_in[:, :N_A], w_in[:, N_A + 2 * H_A:], gates], axis=1)


def _trunk(x, ada, prompt, st_even, st_odd, P):
    hi = not prompt
    G, R, D = x.shape
    wdt = F32 if hi else BF16
    tm_lin = 256
    new_even = new_odd = None
    for l in range(2):
        sh1, sc1, g1, sh2, sc2, g2 = [ada[l][..., i * D:(i + 1) * D] for i in range(6)]
        if l == 0:
            w = _even_weights(P['w_in_even']).astype(wdt)
            splits = (N_A, N_B, LANES)
            if hi:
                outs, off = [], 0
                for n in splits:
                    outs += norm_mod_proj(x, P['g_pre_mix'][l], sc1, sh1, w[:, off:off + n], (n,), hi, tm_lin)
                    off += n
                a, qkv, gg = outs
            else:
                a, qkv, gg = norm_mod_proj(x, P['g_pre_mix'][l], sc1, sh1, w, splits, hi, tm_lin)
            bg = jnp.pad(P['b_gate_a'].reshape(1, 2 * H_A), ((0, 0), (0, LANES - 2 * H_A)))
            if prompt:
                ha, C, n, m = mlstm_prompt(a, gg, bg)
                swa = [swa_prompt(qkv, g, dil) for g, (_, dil) in enumerate(SWA_GROUPS)]
                T = R
                bufs = []
                for g, (win, _) in enumerate(SWA_GROUPS):
                    wb = min(win, T)
                    kb = qkv[:, T - wb:, (3 + g) * GW_B:(4 + g) * GW_B]
                    vb = qkv[:, T - wb:, (6 + g) * GW_B:(7 + g) * GW_B]
                    bufs.append(jnp.concatenate([kb, vb], axis=-1).reshape(G, wb, 2, H_B, DH_B))
                new_even = (C, n, m[:, 0, :H_A], bufs[0], bufs[1], bufs[2])
            else:
                B = R
                C0, n0, m0 = st_even[:3]
                m0p = jnp.pad(m0, ((0, 0), (0, LANES - H_A))).reshape(B, 1, LANES)
                tok = lambda t: t.reshape(B, 1, t.shape[-1])
                ha, C, n, m = mlstm_step(tok(a), tok(gg), bg, C0, n0, m0p)
                swa, bufs = [], []
                for g, (_, dil) in enumerate(SWA_GROUPS):
                    buf = st_even[3 + g]
                    o, lse, nb = swa_step(tok(qkv), g, buf.reshape(B, buf.shape[1], -1), dil)
                    swa.append((o.reshape(1, B, -1), lse.reshape(1, B, -1)))
                    bufs.append(nb.reshape(buf.shape))
                ha = ha.reshape(1, B, -1)
                new_even = (C, n, m[:, 0, :H_A], bufs[0], bufs[1], bufs[2])
            x = mix_out(x, g1, P['g_post_mix'][l], [ha], ([s[0] for s in swa], [s[1] for s in swa]),
                        P['w_out_even'].astype(wdt), hi, tm_lin)
        else:
            w = P['w_in_odd'].astype(wdt)
            splits = (N_C, 2 * N_C, N_D)
            if hi:
                outs, off = [], 0
                for n in splits:
                    outs += norm_mod_proj(x, P['g_pre_mix'][l], sc1, sh1, w[:, off:off + n], (n,), hi, tm_lin)
                    off += n
                qc, kvc, d = outs
            else:
                qc, kvc, d = norm_mod_proj(x, P['g_pre_mix'][l], sc1, sh1, w, splits, hi, tm_lin)
            if prompt:
                hc = moba_prompt(qc, kvc)
                od, S = hgrn_prompt(d, P['w_lb'], P['g_out_d'], l)
                new_odd = (kvc.reshape(G, R, 2, H_C, DH_C), S)
            else:
                B = R
                pool, page_table, S0 = st_odd
                tok = lambda t: t.reshape(B, 1, t.shape[-1])
                hc = moba_step(tok(qc), tok(kvc), pool.reshape(pool.shape[0], pool.shape[1], -1), page_table)
                od, S = hgrn_step(tok(d), P['w_lb'], P['g_out_d'], S0, l)
                hc = hc.reshape(1, B, -1)
                od = od.reshape(1, B, -1)
                new_odd = (kvc.reshape(B, 1, 2, H_C, DH_C), S)
            x = mix_out(x, g1, P['g_post_mix'][l], [hc, od], None, P['w_out_odd'].astype(wdt), hi, tm_lin)
        x = ffn(x, P['g_pre_ffn'][l], sc2, sh2, g2, P['g_post_ffn'][l],
                P['w_ffn_gate'][l].astype(wdt), P['w_ffn_up'][l].astype(wdt), P['w_ffn_down'][l].astype(wdt),
                hi, tm=512, tf=256 if hi else 1408)
    return x, new_even, new_odd


def kernel(x_prompt, x_sample, state_mlstm_C, state_mlstm_n, state_mlstm_m, cache_swa_w128, cache_swa_w512, cache_swa_w2048, cache_moba_kv, state_hgrn_S, page_table, c_prompt, c_sample, w_ada, b_ada, g_pre_mix, g_post_mix, g_pre_ffn, g_post_ffn, w_in_even, b_gate_a, w_out_even, w_in_odd, w_lb, g_out_d, w_out_odd, w_ffn_gate, w_ffn_up, w_ffn_down):
    P = {'g_pre_mix': g_pre_mix, 'g_post_mix': g_post_mix, 'g_pre_ffn': g_pre_ffn, 'g_post_ffn': g_post_ffn,
         'w_in_even': w_in_even, 'b_gate_a': b_gate_a, 'w_out_even': w_out_even, 'w_in_odd': w_in_odd,
         'w_lb': w_lb, 'g_out_d': g_out_d, 'w_out_odd': w_out_odd, 'w_ffn_gate': w_ffn_gate,
         'w_ffn_up': w_ffn_up, 'w_ffn_down': w_ffn_down}
    Bp = x_prompt.shape[0]
    Bs = x_sample.shape[0]
    D = x_prompt.shape[-1]
    c_all = jnp.concatenate([c_prompt, c_sample], axis=0)
    c_all = jnp.pad(c_all, ((0, -(Bp + Bs) % 8), (0, 0)))
    ada = ada_all(c_all, w_ada, b_ada)[:, :Bp + Bs]
    ada_p = ada[:, :Bp].reshape(ada.shape[0], Bp, 1, 6 * D)
    ada_s = ada[:, Bp:].reshape(ada.shape[0], 1, Bs, 6 * D)
    y_p, ev_p, od_p = _trunk(x_prompt, ada_p, True, None, None, P)
    y_s, ev_s, od_s = _trunk(
        x_sample.reshape(1, Bs, D), ada_s, False,
        (state_mlstm_C, state_mlstm_n, state_mlstm_m, cache_swa_w128, cache_swa_w512, cache_swa_w2048),
        (cache_moba_kv, page_table, state_hgrn_S), P)
    return ((y_p, y_s.reshape(x_sample.shape)) + tuple(ev_p[:3]) + tuple(ev_s[:3]) + tuple(ev_p[3:])
            + tuple(ev_s[3:]) + (od_p[0], od_s[0], od_p[1], od_s[1]))
```

```python
import functools

import jax
import jax.numpy as jnp
from jax import lax
from jax.experimental import pallas as pl
from jax.experimental.pallas import tpu as pltpu

F32 = jnp.float32
BF16 = jnp.bfloat16
HI = lax.Precision.HIGHEST

EPS = 1e-6
NEG = -1e30

H_A, DK_A, DV_A, CHUNK_A = 4, 128, 128, 128
SWA_GROUPS = ((128, 1), (512, 4), (2048, 16))
H_B, DH_B, SWA_BAND = 4, 64, 128
GW_B = H_B * DH_B
H_C, DH_C, MOBA_BLOCK, MOBA_TOPK = 8, 64, 256, 3
H_D, DK_D, DV_D, CHUNK_D, SUB_D = 4, 128, 128, 64, 16

LANES = 128
VMEM_LIMIT = 56 << 20


def _params(sem, big=False):
    return pltpu.CompilerParams(dimension_semantics=sem,
                                vmem_limit_bytes=VMEM_LIMIT if big else None)


def _dot(a, b, hi=False):
    return jnp.dot(a, b, preferred_element_type=F32, precision=HI if hi else None)


def _dot_nt(a, b, hi=False):
    return lax.dot_general(a, b, (((1,), (1,)), ((), ())), preferred_element_type=F32,
                           precision=HI if hi else None)


def _dot_tn(a, b, hi=False):
    return lax.dot_general(a, b, (((0,), (0,)), ((), ())), preferred_element_type=F32,
                           precision=HI if hi else None)


def _rms(x, g):
    return x * lax.rsqrt(jnp.mean(x * x, axis=-1, keepdims=True) + EPS) * g


def _sigmoid(x):
    return 1.0 / (1.0 + jnp.exp(-x))


def _silu(x):
    return x * _sigmoid(x)


def _log_sigmoid(x):
    return jnp.minimum(x, 0.0) - jnp.log(1.0 + jnp.exp(-jnp.abs(x)))


def _col_of_row(row):
    n = row.shape[-1]
    eye = lax.broadcasted_iota(jnp.int32, (n, n), 0) == lax.broadcasted_iota(jnp.int32, (n, n), 1)
    return jnp.sum(jnp.where(eye, row, 0.0), axis=1, keepdims=True)


def _ada_body(c_ref, w_ref, b_ref, o_ref):
    o_ref[0] = _dot(_silu(c_ref[...]), w_ref[0], hi=True) + b_ref[0]


def ada_all(c, w_ada, b_ada, tn=768):
    depth, d, n = w_ada.shape
    r = c.shape[0]
    return pl.pallas_call(
        _ada_body,
        grid=(depth, n // tn),
        in_specs=[pl.BlockSpec((r, d), lambda l, j: (0, 0)),
                  pl.BlockSpec((1, d, tn), lambda l, j: (l, 0, j)),
                  pl.BlockSpec((1, 1, tn), lambda l, j: (l, 0, j))],
        out_specs=pl.BlockSpec((1, r, tn), lambda l, j: (l, 0, j)),
        out_shape=jax.ShapeDtypeStruct((depth, r, n), F32),
        compiler_params=_params(("parallel", "parallel")),
        name="ada",
    )(c, w_ada, b_ada.reshape(depth, 1, n))


def _proj_body(x_ref, g_ref, sc_ref, sh_ref, w_ref, *refs, splits, hi, cw, n_t):
    h = _rms(x_ref[0], g_ref[...]) * (1.0 + sc_ref[0]) + sh_ref[0]
    hc = h if hi else h.astype(BF16)
    o_refs = refs[1:] if n_t else refs
    off = 0
    for o_ref, n in zip(o_refs, splits):
        for c0 in range(0, n, cw):
            c1 = min(c0 + cw, n)
            o_ref[0, :, c0:c1] = _dot(hc, w_ref[:, off + c0:off + c1], hi)
        off += n
    if n_t:
        wt_ref, ot_ref = refs[0], refs[-1]
        for r0 in range(0, n_t, cw):
            ot_ref[0, r0:r0 + cw, :] = _dot_nt(wt_ref[r0:r0 + cw, :], hc, hi)


def _mod_spec(mod, tm):
    if mod.shape[1] == 1:
        return pl.BlockSpec((1, 1, mod.shape[2]), lambda gi, ri: (gi, 0, 0))
    return pl.BlockSpec((1, tm, mod.shape[2]), lambda gi, ri: (gi, ri, 0))


def norm_mod_proj(x, g, sc, sh, w, splits, hi, tm, wt=None):
    G, R, D = x.shape
    tm = min(tm, R)
    n_all = sum(splits)
    n_t = 0 if wt is None else wt.shape[0]
    ops = [x, g.reshape(1, D), sc, sh, w]
    in_specs = [pl.BlockSpec((1, tm, D), lambda gi, ri: (gi, ri, 0)),
                pl.BlockSpec((1, D), lambda gi, ri: (0, 0)),
                _mod_spec(sc, tm), _mod_spec(sh, tm),
                pl.BlockSpec((D, n_all), lambda gi, ri: (0, 0))]
    out_specs = [pl.BlockSpec((1, tm, n), lambda gi, ri: (gi, ri, 0)) for n in splits]
    out_shape = [jax.ShapeDtypeStruct((G, R, n), F32) for n in splits]
    if n_t:
        ops.append(wt)
        in_specs.append(pl.BlockSpec((n_t, D), lambda gi, ri: (0, 0)))
        out_specs.append(pl.BlockSpec((1, n_t, tm), lambda gi, ri: (gi, 0, ri)))
        out_shape.append(jax.ShapeDtypeStruct((G, n_t, R), F32))
    return pl.pallas_call(
        functools.partial(_proj_body, splits=tuple(splits), hi=hi, cw=512, n_t=n_t),
        grid=(G, R // tm),
        in_specs=in_specs,
        out_specs=out_specs,
        out_shape=out_shape,
        compiler_params=_params(("parallel", "parallel"), big=True),
        name="proj",
    )(*ops)


def _mixout_body(*refs, n_parts, swa, hi):
    x_ref, g1_ref, gp_ref = refs[:3]
    part_refs = refs[3:3 + n_parts]
    pos = 3 + n_parts
    ng = len(SWA_GROUPS)
    o_refs = refs[pos:pos + swa * ng]
    l_refs = refs[pos + swa * ng:pos + 2 * swa * ng]
    pos += 2 * swa * ng
    w_ref, out_ref = refs[pos], refs[pos + 1]

    def cast(a):
        return a if hi else a.astype(BF16)

    acc = None
    off = 0
    for p_ref in part_refs:
        a = p_ref[0]
        k = a.shape[-1]
        t = _dot(cast(a), w_ref[off:off + k, :], hi)
        acc = t if acc is None else acc + t
        off += k
    for c in range(swa):
        ls = [l_ref[0] for l_ref in l_refs[c * ng:(c + 1) * ng]]
        os_ = [o_ref[0] for o_ref in o_refs[c * ng:(c + 1) * ng]]
        mx = functools.reduce(jnp.maximum, ls)
        es = [jnp.exp(l - mx) for l in ls]
        hb = sum(e * o for e, o in zip(es, os_)) / sum(es)
        acc = acc + _dot(cast(hb), w_ref[off:off + hb.shape[-1], :], hi)
        off += hb.shape[-1]
    out_ref[0] = x_ref[0] + g1_ref[0] * _rms(acc, gp_ref[...])


def mix_out(x, g1, gpost, parts, swa_parts, w, hi, tm):
    G, R, D = x.shape
    tm = min(tm, R)
    row = lambda a: pl.BlockSpec((1, tm, a.shape[2]), lambda gi, ri: (gi, ri, 0))
    ops = [x, g1, gpost.reshape(1, D)] + list(parts)
    specs = [row(x), _mod_spec(g1, tm), pl.BlockSpec((1, D), lambda gi, ri: (0, 0))] + [row(p) for p in parts]
    swa = 0
    if swa_parts is not None:
        swa = len(swa_parts[0]) // len(SWA_GROUPS)
        ops += list(swa_parts[0]) + list(swa_parts[1])
        specs += [row(a) for a in ops[-2 * len(swa_parts[0]):]]
    ops.append(w)
    specs.append(pl.BlockSpec(w.shape, lambda gi, ri: (0, 0)))
    return pl.pallas_call(
        functools.partial(_mixout_body, n_parts=len(parts), swa=swa, hi=hi),
        grid=(G, R // tm),
        in_specs=specs,
        out_specs=row(x),
        out_shape=jax.ShapeDtypeStruct(x.shape, F32),
        compiler_params=_params(("parallel", "parallel"), big=True),
        name="mix_out",
    )(*ops)


def _ffn_body(x_ref, gpre_ref, sc_ref, sh_ref, g2_ref, gpost_ref, wg_ref, wu_ref, wd_ref, out_ref,
              h_scr, acc_scr, *, hi):
    k = pl.program_id(2)

    @pl.when(k == 0)
    def _():
        h = _rms(x_ref[0], gpre_ref[...]) * (1.0 + sc_ref[0]) + sh_ref[0]
        h_scr[...] = h.astype(h_scr.dtype)
        acc_scr[...] = jnp.zeros_like(acc_scr)

    h = h_scr[...]
    a = _silu(_dot(h, wg_ref[...], hi)) * _dot(h, wu_ref[...], hi)
    acc_scr[...] += _dot(a.astype(h.dtype), wd_ref[...], hi)

    @pl.when(k == pl.num_programs(2) - 1)
    def _():
        out_ref[0] = x_ref[0] + g2_ref[0] * _rms(acc_scr[...], gpost_ref[...])


def ffn(x, gpre, sc, sh, g2, gpost, wg, wu, wd, hi, tm, tf):
    G, R, D = x.shape
    FF = wg.shape[1]
    tm = min(tm, R)
    mspec = lambda m: (pl.BlockSpec((1, 1, D), lambda gi, ri, k: (gi, 0, 0)) if m.shape[1] == 1
                       else pl.BlockSpec((1, tm, D), lambda gi, ri, k: (gi, ri, 0)))
    vec = pl.BlockSpec((1, D), lambda gi, ri, k: (0, 0))
    return pl.pallas_call(
        functools.partial(_ffn_body, hi=hi),
        grid=(G, R // tm, FF // tf),
        in_specs=[pl.BlockSpec((1, tm, D), lambda gi, ri, k: (gi, ri, 0)), vec, mspec(sc), mspec(sh),
                  mspec(g2), vec,
                  pl.BlockSpec((D, tf), lambda gi, ri, k: (0, k)),
                  pl.BlockSpec((D, tf), lambda gi, ri, k: (0, k)),
                  pl.BlockSpec((tf, D), lambda gi, ri, k: (k, 0))],
        out_specs=pl.BlockSpec((1, tm, D), lambda gi, ri, k: (gi, ri, 0)),
        out_shape=jax.ShapeDtypeStruct(x.shape, F32),
        scratch_shapes=[pltpu.VMEM((tm, D), F32 if hi else BF16), pltpu.VMEM((tm, D), F32)],
        compiler_params=_params(("parallel", "parallel", "arbitrary"), big=True),
        name="ffn",
    )(x, gpre.reshape(1, D), sc, sh, g2, gpost.reshape(1, D), wg, wu, wd)


def _mlstm_body(a_ref, g_ref, bg_ref, h_ref, C_ref, n_ref, m_ref):
    L = CHUNK_A
    HD = H_A * DK_A

    @pl.when(pl.program_id(1) == 0)
    def _():
        C_ref[...] = jnp.zeros_like(C_ref)
        n_ref[...] = jnp.zeros_like(n_ref)
        m_ref[...] = jnp.zeros_like(m_ref)

    lane = lax.broadcasted_iota(jnp.int32, (L, LANES), 1)
    ri = lax.broadcasted_iota(jnp.int32, (L, L), 0)
    ci = lax.broadcasted_iota(jnp.int32, (L, L), 1)
    causal = ci <= ri
    pre = g_ref[0] + bg_ref[...]
    gates = jnp.where(lane >= H_A, _log_sigmoid(pre), pre)
    bcum = _dot(causal.astype(F32), gates, hi=True)
    gates_t = gates.T
    bcum_t = bcum.T
    m_all = m_ref[0]
    m_lane = lax.broadcasted_iota(jnp.int32, (1, LANES), 1)
    m_out = m_all
    for h in range(H_A):
        q = a_ref[0, :, h * DK_A:(h + 1) * DK_A]
        k = a_ref[0, :, HD + h * DK_A:HD + (h + 1) * DK_A] * (DK_A ** -0.5)
        v = a_ref[0, :, 2 * HD + h * DV_A:2 * HD + (h + 1) * DV_A]
        og = a_ref[0, :, 3 * HD + h * DV_A:3 * HD + (h + 1) * DV_A]
        qb, kb, vb = q.astype(BF16), k.astype(BF16), v.astype(BF16)
        ig_row = gates_t[h:h + 1, :]
        b_col = bcum[:, H_A + h:H_A + h + 1]
        b_row = bcum_t[H_A + h:H_A + h + 1, :]
        m = m_all[:, h:h + 1]
        C = C_ref[0, h]
        n_row = n_ref[0, h:h + 1, :]

        dmat = jnp.where(causal, b_col - b_row + ig_row, NEG)
        inter = b_col + m
        mt = jnp.maximum(inter, jnp.max(dmat, axis=1, keepdims=True))
        wmat = jnp.exp(dmat - mt)
        winter = jnp.exp(inter - mt)
        s = _dot_nt(qb, kb) * wmat
        num = _dot(s.astype(BF16), vb) + winter * _dot(qb, C.astype(BF16))
        den = jnp.sum(s, axis=1, keepdims=True) + winter * jnp.sum(q * n_row, axis=1, keepdims=True)
        hh = num / jnp.maximum(jnp.abs(den), jnp.exp(-mt))
        h_ref[0, :, h * DV_A:(h + 1) * DV_A] = _sigmoid(og) * hh

        bl = b_row[:, L - 1:L]
        gl = bl - b_row + ig_row
        m_new = jnp.maximum(bl + m, jnp.max(gl, axis=1, keepdims=True))
        ws = jnp.exp(gl - m_new)
        wc = jnp.exp(bl + m - m_new)
        C_ref[0, h] = wc * C + _dot((k.T * ws).astype(BF16), vb)
        n_ref[0, h:h + 1, :] = wc * n_row + _dot(jnp.broadcast_to(ws, (8, L)), k, hi=True)[0:1]
        m_out = jnp.where(m_lane == h, m_new, m_out)
    m_ref[0] = m_out


def mlstm_prompt(a, g, bg):
    B, T, _ = a.shape
    L = CHUNK_A
    return pl.pallas_call(
        _mlstm_body,
        grid=(B, T // L),
        in_specs=[pl.BlockSpec((1, L, a.shape[2]), lambda b, c: (b, c, 0)),
                  pl.BlockSpec((1, L, LANES), lambda b, c: (b, c, 0)),
                  pl.BlockSpec((1, LANES), lambda b, c: (0, 0))],
        out_specs=[pl.BlockSpec((1, L, H_A * DV_A), lambda b, c: (b, c, 0)),
                   pl.BlockSpec((1, H_A, DK_A, DV_A), lambda b, c: (b, 0, 0, 0)),
                   pl.BlockSpec((1, H_A, DK_A), lambda b, c: (b, 0, 0)),
                   pl.BlockSpec((1, 1, LANES), lambda b, c: (b, 0, 0))],
        out_shape=[jax.ShapeDtypeStruct((B, T, H_A * DV_A), F32),
                   jax.ShapeDtypeStruct((B, H_A, DK_A, DV_A), F32),
                   jax.ShapeDtypeStruct((B, H_A, DK_A), F32),
                   jax.ShapeDtypeStruct((B, 1, LANES), F32)],
        compiler_params=_params(("parallel", "arbitrary")),
        name="mlstm_prompt",
    )(a, g, bg)


def _mlstm_step_body(a_ref, g_ref, bg_ref, C0_ref, n0_ref, m0_ref, h_ref, C_ref, n_ref, m_ref):
    HD = H_A * DK_A
    pre = g_ref[0] + bg_ref[...]
    lane = lax.broadcasted_iota(jnp.int32, (1, LANES), 1)
    gates = jnp.where(lane >= H_A, _log_sigmoid(pre), pre)
    m_all = m0_ref[0]
    m_out = m_all
    for h in range(H_A):
        q = a_ref[0, :, h * DK_A:(h + 1) * DK_A]
        k = a_ref[0, :, HD + h * DK_A:HD + (h + 1) * DK_A] * (DK_A ** -0.5)
        v = a_ref[0, :, 2 * HD + h * DV_A:2 * HD + (h + 1) * DV_A]
        og = a_ref[0, :, 3 * HD + h * DV_A:3 * HD + (h + 1) * DV_A]
        ig = gates[:, h:h + 1]
        lf = gates[:, H_A + h:H_A + h + 1]
        m = m_all[:, h:h + 1]
        C = C0_ref[0, h]
        n_row = n0_ref[0, h:h + 1, :]
        inter = lf + m
        mt = jnp.maximum(inter, ig)
        s = jnp.sum(q * k, axis=1, keepdims=True) * jnp.exp(ig - mt)
        winter = jnp.exp(inter - mt)
        qC = _dot(jnp.broadcast_to(q, (8, DK_A)), C, hi=True)[0:1]
        num = s * v + winter * qC
        den = s + winter * jnp.sum(q * n_row, axis=1, keepdims=True)
        hh = num / jnp.maximum(jnp.abs(den), jnp.exp(-mt))
        h_ref[0, :, h * DV_A:(h + 1) * DV_A] = _sigmoid(og) * hh
        m_new = jnp.maximum(inter, ig)
        ws = jnp.exp(ig - m_new)
        wc = jnp.exp(inter - m_new)
        C_ref[0, h] = wc * C + (ws * _col_of_row(k)) * v
        n_ref[0, h:h + 1, :] = wc * n_row + ws * k
        m_out = jnp.where(lane == h, m_new, m_out)
    m_ref[0] = m_out


def mlstm_step(a, g, bg, C0, n0, m0):
    B = a.shape[0]
    r3 = lambda w: pl.BlockSpec((1, 1, w), lambda b: (b, 0, 0))
    cs = pl.BlockSpec((1, H_A, DK_A, DV_A), lambda b: (b, 0, 0, 0))
    ns = pl.BlockSpec((1, H_A, DK_A), lambda b: (b, 0, 0))
    return pl.pallas_call(
        _mlstm_step_body,
        grid=(B,),
        in_specs=[r3(a.shape[2]), r3(LANES), pl.BlockSpec((1, LANES), lambda b: (0, 0)), cs, ns, r3(LANES)],
        out_specs=[r3(H_A * DV_A), cs, ns, r3(LANES)],
        out_shape=[jax.ShapeDtypeStruct((B, 1, H_A * DV_A), F32),
                   jax.ShapeDtypeStruct(C0.shape, F32), jax.ShapeDtypeStruct(n0.shape, F32),
                   jax.ShapeDtypeStruct((B, 1, LANES), F32)],
        compiler_params=_params(("parallel",)),
        name="mlstm_step",
    )(a, g, bg, C0, n0, m0)


NPAIR_B = GW_B // LANES


def _swa_body(*refs, dil):
    Q = SWA_BAND
    q_refs, kc_refs, kp_refs, vc_refs, vp_refs = [refs[i * NPAIR_B:(i + 1) * NPAIR_B] for i in range(5)]
    o_refs = refs[5 * NPAIR_B:6 * NPAIR_B]
    l_refs = refs[6 * NPAIR_B:7 * NPAIR_B]
    qi = lax.broadcasted_iota(jnp.int32, (Q, Q), 0)
    kj = lax.broadcasted_iota(jnp.int32, (Q, Q), 1)
    mask_c = kj <= qi
    mask_p = jnp.logical_and(kj >= qi, pl.program_id(1) > 0)
    lane = lax.broadcasted_iota(jnp.int32, (Q, LANES), 1)
    first = lane < DH_B
    chains = [(p, hh) for p in range(NPAIR_B) for hh in range(2)]
    for r in range(dil):
        rows = (pl.ds(0, 1), pl.ds(r, Q, stride=dil) if dil > 1 else pl.ds(0, Q), slice(None))
        ld = lambda ref: ref[rows][0]
        q = [ld(ref) * (DH_B ** -0.5) for ref in q_refs]
        kc = [ld(ref).astype(BF16) for ref in kc_refs]
        kp = [ld(ref).astype(BF16) for ref in kp_refs]
        vc = [ld(ref).astype(BF16) for ref in vc_refs]
        vp = [ld(ref).astype(BF16) for ref in vp_refs]
        qm = [jnp.where(first if hh == 0 else ~first, q[p], 0.0).astype(BF16) for p, hh in chains]
        sc = [_dot_nt(qm[c], kc[p]) for c, (p, hh) in enumerate(chains)]
        sp = [_dot_nt(qm[c], kp[p]) for c, (p, hh) in enumerate(chains)]
        pcs, pps, dens, lses = [], [], [], []
        for c in range(len(chains)):
            s_c = jnp.where(mask_c, sc[c], NEG)
            s_p = jnp.where(mask_p, sp[c], NEG)
            mx = jnp.maximum(jnp.max(s_c, axis=1, keepdims=True), jnp.max(s_p, axis=1, keepdims=True))
            pc = jnp.exp(s_c - mx)
            pp = jnp.exp(s_p - mx)
            den = jnp.sum(pc, axis=1, keepdims=True) + jnp.sum(pp, axis=1, keepdims=True)
            pcs.append(pc.astype(BF16))
            pps.append(pp.astype(BF16))
            dens.append(den)
            lses.append(mx + jnp.log(den))
        os_ = [(_dot(pcs[c], vc[p]) + _dot(pps[c], vp[p])) / dens[c] for c, (p, hh) in enumerate(chains)]
        for p in range(NPAIR_B):
            o_refs[p][rows] = jnp.where(first, os_[2 * p], os_[2 * p + 1])[None]
            l_refs[p][rows] = jnp.where(first, lses[2 * p], lses[2 * p + 1])[None]


def swa_prompt(qkv, g, dil):
    B, T, W = qkv.shape
    ng = W // GW_B // 3
    unit = SWA_BAND * dil
    col = lambda part, p: (part * ng + g) * NPAIR_B + p
    cur = lambda part: [pl.BlockSpec((1, unit, LANES), functools.partial(lambda b, n, c: (b, n, c), c=col(part, p)))
                        for p in range(NPAIR_B)]
    prev = lambda part: [pl.BlockSpec((1, unit, LANES),
                                      functools.partial(lambda b, n, c: (b, jnp.maximum(n - 1, 0), c), c=col(part, p)))
                         for p in range(NPAIR_B)]
    ospec = pl.BlockSpec((1, unit, LANES), lambda b, n: (b, n, 0))
    res = pl.pallas_call(
        functools.partial(_swa_body, dil=dil),
        grid=(B, T // unit),
        in_specs=cur(0) + cur(1) + prev(1) + cur(2) + prev(2),
        out_specs=[ospec] * (2 * NPAIR_B),
        out_shape=[jax.ShapeDtypeStruct((B, T, LANES), F32)] * (2 * NPAIR_B),
        compiler_params=_params(("parallel", "parallel"), big=True),
        name="swa_prompt_d%d" % dil,
    )(*([qkv] * (5 * NPAIR_B)))
    return res[:NPAIR_B], res[NPAIR_B:]


def _swa_step_body(q_ref, kvn_ref, buf_ref, o_ref, l_ref, nb_ref, *, dil):
    wb = buf_ref.shape[-1]
    lane = lax.broadcasted_iota(jnp.int32, (1, wb), 1)
    read = (lane % dil) == 0
    last = lane == wb - 1
    for h in range(H_B):
        q = q_ref[0, h]
        kn = kvn_ref[0, 0, h]
        vn = kvn_ref[0, 1, h]
        kt = buf_ref[0, 0, h]
        vt = buf_ref[0, 1, h]
        s = jnp.where(read, jnp.sum(kt * q, axis=0, keepdims=True) * (DH_B ** -0.5), NEG)
        s_self = jnp.sum(kn * q, axis=0, keepdims=True) * (DH_B ** -0.5)
        mx = jnp.maximum(jnp.max(s, axis=1, keepdims=True), s_self)
        p = jnp.exp(s - mx)
        p_self = jnp.exp(s_self - mx)
        den = jnp.sum(p, axis=1, keepdims=True) + p_self
        o_ref[0, h] = (jnp.sum(vt * p, axis=1, keepdims=True) + p_self * vn) / den
        l_ref[0, h] = jnp.broadcast_to(mx + jnp.log(den), (DH_B, 1))
        nb_ref[0, 0, h] = jnp.where(last, kn, pltpu.roll(kt, wb - 1, axis=1))
        nb_ref[0, 1, h] = jnp.where(last, vn, pltpu.roll(vt, wb - 1, axis=1))


def swa_step(q, kvn, buf_t, dil):
    B = q.shape[0]
    qs = pl.BlockSpec((1,) + q.shape[1:], lambda b: (b, 0, 0, 0))
    ks = pl.BlockSpec((1,) + kvn.shape[1:], lambda b: (b, 0, 0, 0, 0))
    bs = pl.BlockSpec((1,) + buf_t.shape[1:], lambda b: (b, 0, 0, 0, 0))
    return pl.pallas_call(
        functools.partial(_swa_step_body, dil=dil),
        grid=(B,),
        in_specs=[qs, ks, bs],
        out_specs=[qs, qs, bs],
        out_shape=[jax.ShapeDtypeStruct(q.shape, F32)] * 2 + [jax.ShapeDtypeStruct(buf_t.shape, F32)],
        compiler_params=_params(("parallel",), big=True),
        name="swa_step_d%d" % dil,
    )(q, kvn, buf_t)


AUG = 2 * DH_C


VROWS = DH_C + 16
TQ_C = 128
HEADS_C = 4
LOG2E = 1.4426950408889634


def _moba_kprep_body(kt_ref, vt_ref, kmt_ref, ka_ref, vta_ref, *, nblk):
    n = pl.program_id(1)
    kt = kt_ref[0]

    @pl.when(n == 0)
    def _():
        kmt_ref[...] = jnp.zeros_like(kmt_ref)

    blk_lane = lax.broadcasted_iota(jnp.int32, (kt.shape[0], nblk), 1)
    kmt_ref[0] = jnp.where(blk_lane == n, jnp.mean(kt, axis=1, keepdims=True), kmt_ref[0])
    k = kt.T
    lane = lax.broadcasted_iota(jnp.int32, (MOBA_BLOCK, DH_C), 1)
    onehot = jnp.where(lane == n, 1.0, 0.0).astype(BF16)
    pieces = []
    for h in range(H_C):
        pieces += [k[:, h * DH_C:(h + 1) * DH_C].astype(BF16), onehot]
    ka_ref[0] = jnp.concatenate(pieces, axis=1)
    vt = vt_ref[0]
    ones = jnp.where(lax.broadcasted_iota(jnp.int32, (VROWS - DH_C, MOBA_BLOCK), 0) == 0, 1.0, 0.0).astype(BF16)
    pieces = []
    for h in range(H_C):
        pieces += [vt[h * DH_C:(h + 1) * DH_C].astype(BF16), ones]
    vta_ref[0] = jnp.concatenate(pieces, axis=0)


def _moba_gate_body(q_ref, kmt_ref, qt_ref, *, nblk):
    own = pl.program_id(1)
    lane = lax.broadcasted_iota(jnp.int32, (MOBA_BLOCK, nblk), 1)
    lane_f = lane.astype(F32)
    past = lane < own
    for h in range(H_C):
        sl = slice(h * DH_C, (h + 1) * DH_C)
        qh = q_ref[0, :, sl]
        g = jnp.where(past, _dot(qh, kmt_ref[0, sl, :], hi=True), NEG)
        sel = lane == own
        for _ in range(MOBA_TOPK):
            mx = jnp.max(g, axis=1, keepdims=True)
            first = jnp.min(jnp.where(g == mx, lane_f, float(nblk)), axis=1, keepdims=True)
            pick = lane_f == first
            sel = jnp.logical_or(sel, jnp.logical_and(pick, past))
            g = jnp.where(pick, -jnp.inf, g)
        bias = jnp.where(sel, 0.0, NEG)
        aug = jnp.concatenate([qh * (DH_C ** -0.5 * LOG2E), bias, jnp.zeros((MOBA_BLOCK, AUG - DH_C - nblk), F32)],
                              axis=1)
        qt_ref[0, h * AUG:(h + 1) * AUG, :] = aug.T.astype(BF16)


def _moba_attn_body(qt_ref, ka_ref, vta_ref, o_ref):
    own = pl.program_id(2)
    TK = MOBA_BLOCK
    nsub = TK // TQ_C
    chains = [(hh, qs) for hh in range(HEADS_C) for qs in range(nsub)]
    qts = [qt_ref[0, hh * AUG:(hh + 1) * AUG, qs * TQ_C:(qs + 1) * TQ_C] for hh, qs in chains]

    def block(n, hh):
        r0 = pl.multiple_of(n * TK, TK)
        return (ka_ref[0, pl.ds(r0, TK), hh * AUG:(hh + 1) * AUG],
                vta_ref[0, hh * VROWS:(hh + 1) * VROWS, pl.ds(r0, TK)])

    key = lax.broadcasted_iota(jnp.int32, (TK, TQ_C), 0)
    qry = lax.broadcasted_iota(jnp.int32, (TK, TQ_C), 1)

    def step(n, carry, diagonal):
        kvs = [block(n, hh) for hh in range(HEADS_C)]
        ss = [_dot(kvs[hh][0], qts[c]) for c, (hh, qs) in enumerate(chains)]
        ms, ps, alphas = [], [], []
        for c, (hh, qs) in enumerate(chains):
            s = jnp.where(key <= qry + qs * TQ_C, ss[c], NEG) if diagonal else ss[c]
            mx = jnp.max(s, axis=0, keepdims=True)
            m_new = mx if carry is None else jnp.maximum(carry[2 * c], mx)
            ms.append(m_new)
            ps.append(jnp.exp2(s - m_new).astype(BF16))
            alphas.append(None if carry is None else jnp.exp2(carry[2 * c] - m_new))
        out = []
        for c, (hh, qs) in enumerate(chains):
            pv = _dot(kvs[hh][1], ps[c])
            out += [ms[c], pv if carry is None else alphas[c] * carry[2 * c + 1] + pv]
        return tuple(out)

    carry = step(own, None, True)

    def body(n, carry):
        return step(n, carry, False)

    res = lax.fori_loop(0, own, body, tuple(carry))
    for qs in range(nsub):
        accs = [res[2 * c + 1] for c, (hh, q2) in enumerate(chains) if q2 == qs]
        o = jnp.concatenate([a[:DH_C] / a[DH_C:DH_C + 1] for a in accs], axis=0)
        o_ref[0, qs * TQ_C:(qs + 1) * TQ_C, :] = o.T


def moba_prompt(q, kvt):
    B, T, W = q.shape
    nblk = T // MOBA_BLOCK
    assert nblk <= AUG - DH_C
    kmt, ka, vta = pl.pallas_call(
        functools.partial(_moba_kprep_body, nblk=nblk),
        grid=(B, nblk),
        in_specs=[pl.BlockSpec((1, W, MOBA_BLOCK), lambda b, n: (b, 0, n)),
                  pl.BlockSpec((1, W, MOBA_BLOCK), lambda b, n: (b, 1, n))],
        out_specs=[pl.BlockSpec((1, W, nblk), lambda b, n: (b, 0, 0)),
                   pl.BlockSpec((1, MOBA_BLOCK, H_C * AUG), lambda b, n: (b, n, 0)),
                   pl.BlockSpec((1, H_C * VROWS, MOBA_BLOCK), lambda b, n: (b, 0, n))],
        out_shape=[jax.ShapeDtypeStruct((B, W, nblk), F32),
                   jax.ShapeDtypeStruct((B, T, H_C * AUG), BF16),
                   jax.ShapeDtypeStruct((B, H_C * VROWS, T), BF16)],
        compiler_params=_params(("parallel", "arbitrary")),
        name="moba_kprep",
    )(kvt, kvt)
    qt = pl.pallas_call(
        functools.partial(_moba_gate_body, nblk=nblk),
        grid=(B, nblk),
        in_specs=[pl.BlockSpec((1, MOBA_BLOCK, W), lambda b, i: (b, i, 0)),
                  pl.BlockSpec((1, W, nblk), lambda b, i: (b, 0, 0))],
        out_specs=pl.BlockSpec((1, H_C * AUG, MOBA_BLOCK), lambda b, i: (b, 0, i)),
        out_shape=jax.ShapeDtypeStruct((B, H_C * AUG, T), BF16),
        compiler_params=_params(("parallel", "parallel")),
        name="moba_gate",
    )(q, kmt)
    return pl.pallas_call(
        _moba_attn_body,
        grid=(B, H_C // HEADS_C, nblk),
        in_specs=[pl.BlockSpec((1, HEADS_C * AUG, MOBA_BLOCK), lambda b, p, j: (b, p, j)),
                  pl.BlockSpec((1, T, HEADS_C * AUG), lambda b, p, j: (b, 0, p)),
                  pl.BlockSpec((1, HEADS_C * VROWS, T), lambda b, p, j: (b, p, 0))],
        out_specs=pl.BlockSpec((1, MOBA_BLOCK, HEADS_C * DH_C), lambda b, p, j: (b, j, p)),
        out_shape=jax.ShapeDtypeStruct((B, T, W), F32),
        compiler_params=_params(("parallel", "parallel", "arbitrary"), big=True),
        name="moba_attn",
    )(qt, ka, vta)


PAGES_PER_STEP = 16


def _moba_step_gate_body(pt_ref, q_ref, *refs, pages_per_block, n_past):
    page_refs = refs[:PAGES_PER_STEP]
    sel_ref, g_scr = refs[PAGES_PER_STEP], refs[PAGES_PER_STEP + 1]
    s = pl.program_id(1)
    q = q_ref[0]
    W = q.shape[0]
    rows = page_refs[0].shape[-1]
    per_step = PAGES_PER_STEP // pages_per_block
    lane = lax.broadcasted_iota(jnp.int32, (W, LANES), 1)

    @pl.when(s == 0)
    def _():
        g_scr[...] = jnp.zeros_like(g_scr)

    g = g_scr[...]
    for j in range(per_step):
        tot = None
        for i in range(pages_per_block):
            t = page_refs[j * pages_per_block + i][0, 0].reshape(W, rows)
            tot = t if tot is None else tot + t
        kmean = jnp.sum(tot, axis=1, keepdims=True) / (pages_per_block * rows)
        g = jnp.where(lane == s * per_step + j, q * kmean, g)
    g_scr[...] = g

    @pl.when(s == pl.num_programs(1) - 1)
    def _():
        seg = (lax.broadcasted_iota(jnp.int32, (8, W), 1) // DH_C ==
               lax.broadcasted_iota(jnp.int32, (8, W), 0)).astype(F32)
        blk = lax.broadcasted_iota(jnp.int32, (8, LANES), 1)
        blk_f = blk.astype(F32)
        gate = jnp.where(blk < n_past, _dot(seg, g, hi=True), NEG)
        sel = jnp.zeros((8, LANES), F32)
        for r in range(MOBA_TOPK):
            mx = jnp.max(gate, axis=1, keepdims=True)
            first = jnp.min(jnp.where(gate == mx, blk_f, float(LANES)), axis=1, keepdims=True)
            sel = jnp.where(blk == r, first, sel)
            gate = jnp.where(blk_f == first, -jnp.inf, gate)
        sel_ref[0] = sel.astype(jnp.int32)


def _moba_step_attn_body(pt_ref, sel_ref, q_ref, kn_ref, vn_ref, *refs, n_pages):
    k_refs = refs[:n_pages]
    v_refs = refs[n_pages:2 * n_pages]
    o_ref = refs[2 * n_pages]
    q = q_ref[0, 0]
    kn = kn_ref[0, 0, 0]
    vn = vn_ref[0, 0, 0]
    scale = DH_C ** -0.5
    ss = [jnp.sum(k_ref[0, 0, 0] * q, axis=0, keepdims=True) * scale for k_ref in k_refs]
    s_self = jnp.sum(kn * q, axis=0, keepdims=True) * scale
    mx = s_self
    for sj in ss:
        mx = jnp.maximum(mx, jnp.max(sj, axis=1, keepdims=True))
    p_self = jnp.exp(s_self - mx)
    den = p_self
    o = p_self * vn
    for sj, v_ref in zip(ss, v_refs):
        p = jnp.exp(sj - mx)
        den = den + jnp.sum(p, axis=1, keepdims=True)
        o = o + jnp.sum(v_ref[0, 0, 0] * p, axis=1, keepdims=True)
    o_ref[0, 0] = o / den


def moba_step(q, kv_new, pool_t, page_table):
    B, W, _ = q.shape
    page = pool_t.shape[-1]
    n_pt = page_table.shape[1]
    ppb = MOBA_BLOCK // page
    n_past = n_pt // ppb
    assert MOBA_TOPK <= n_past <= LANES and PAGES_PER_STEP % ppb == 0 and n_pt % PAGES_PER_STEP == 0
    pt_flat = page_table.reshape(-1)
    kpage = (1, 1, H_C, DH_C, page)
    sel = pl.pallas_call(
        functools.partial(_moba_step_gate_body, pages_per_block=ppb, n_past=n_past),
        grid_spec=pltpu.PrefetchScalarGridSpec(
            num_scalar_prefetch=1,
            grid=(B, n_pt // PAGES_PER_STEP),
            in_specs=[pl.BlockSpec((1, W, 1), lambda b, s, pt: (b, 0, 0))] +
                     [pl.BlockSpec(kpage, functools.partial(
                         lambda b, s, pt, i: (pt[b * n_pt + s * PAGES_PER_STEP + i], 0, 0, 0, 0), i=i))
                      for i in range(PAGES_PER_STEP)],
            out_specs=pl.BlockSpec((1, 8, LANES), lambda b, s, pt: (b, 0, 0)),
            scratch_shapes=[pltpu.VMEM((W, LANES), F32)]),
        out_shape=jax.ShapeDtypeStruct((B, 8, LANES), jnp.int32),
        compiler_params=_params(("parallel", "arbitrary")),
        name="moba_step_gate",
    )(pt_flat, q, *([pool_t] * PAGES_PER_STEP))
    sel_flat = sel[:, :H_C, :MOBA_TOPK].reshape(-1)
    n_pages = MOBA_TOPK * ppb

    def page_map(b, h, pt, sl, r, i, kv):
        blk = sl[(b * H_C + h) * MOBA_TOPK + r]
        return (pt[b * n_pt + blk * ppb + i], kv, h, 0, 0)

    def page_specs(kv):
        return [pl.BlockSpec((1, 1, 1, DH_C, page), functools.partial(page_map, r=r, i=i, kv=kv))
                for r in range(MOBA_TOPK) for i in range(ppb)]

    q4 = q.reshape(B, H_C, DH_C, 1)
    head = pl.BlockSpec((1, 1, DH_C, 1), lambda b, h, pt, sl: (b, h, 0, 0))
    new = lambda kv: pl.BlockSpec((1, 1, 1, DH_C, 1), lambda b, h, pt, sl: (b, kv, h, 0, 0))
    return pl.pallas_call(
        functools.partial(_moba_step_attn_body, n_pages=n_pages),
        grid_spec=pltpu.PrefetchScalarGridSpec(
            num_scalar_prefetch=2,
            grid=(B, H_C),
            in_specs=[head, new(0), new(1)] + page_specs(0) + page_specs(1),
            out_specs=head),
        out_shape=jax.ShapeDtypeStruct(q4.shape, F32),
        compiler_params=_params(("parallel", "parallel")),
        name="moba_step_attn",
    )(pt_flat, sel_flat, q4, kv_new, kv_new, *([pool_t] * (2 * n_pages)))


def _hgrn_gates(f, wlb_ref, layer):
    w = wlb_ref[...]
    e = jnp.exp(w - jnp.max(w, axis=0, keepdims=True))
    sm = e / jnp.sum(e, axis=0, keepdims=True)
    lb = jnp.sum(sm[1:layer + 1], axis=0, keepdims=True)
    sg = _sigmoid(f)
    return jnp.log(lb + (1.0 - lb) * sg), (1.0 - lb) * (1.0 - sg)


def _hgrn_body(d_ref, wlb_ref, go_ref, o_ref, S_ref, *, layer):
    L, SB = CHUNK_D, SUB_D
    HD = H_D * DK_D

    @pl.when(pl.program_id(1) == 0)
    def _():
        S_ref[...] = jnp.zeros_like(S_ref)

    lf, kd = _hgrn_gates(d_ref[0, :, HD:2 * HD], wlb_ref, layer)
    ri = lax.broadcasted_iota(jnp.int32, (L, L), 0)
    ci = lax.broadcasted_iota(jnp.int32, (L, L), 1)
    b_all = _dot((ci <= ri).astype(F32), lf, hi=True)
    sub_row = lax.broadcasted_iota(jnp.int32, (SB, 1), 0)
    for h in range(H_D):
        sl = slice(h * DK_D, (h + 1) * DK_D)
        vsl = slice(2 * HD + h * DV_D, 2 * HD + (h + 1) * DV_D)
        q = d_ref[0, :, sl]
        v = d_ref[0, :, vsl]
        gd = d_ref[0, :, 3 * HD + h * DV_D:3 * HD + (h + 1) * DV_D]
        b = b_all[:, sl]
        k = kd[:, sl]
        S = S_ref[0, h]
        vb = v.astype(BF16)
        o_inter = _dot((q * jnp.exp(b)).astype(BF16), S.astype(BF16))
        o_blocks = []
        for r0 in range(0, L, SB):
            qI = q[r0:r0 + SB]
            bI = b[r0:r0 + SB]
            oI = o_inter[r0:r0 + SB]
            if r0 > 0:
                ref_b = b[r0 - 1:r0]
                qs = (qI * jnp.exp(bI - ref_b)).astype(BF16)
                ks = (k[:r0] * jnp.exp(ref_b - b[:r0])).astype(BF16)
                oI = oI + _dot(_dot_nt(qs, ks).astype(BF16), vb[:r0])

            for s in range(SB):
                row = r0 + s
                w = jnp.exp(jnp.minimum(bI - b[row:row + 1], 0.0))
                a = jnp.sum(qI * k[row:row + 1] * w, axis=1, keepdims=True)
                oI = oI + jnp.where(sub_row >= s, a, 0.0) * v[row:row + 1]
            o_blocks.append(oI)
        o = jnp.concatenate(o_blocks, axis=0)
        o_ref[0, :, h * DV_D:(h + 1) * DV_D] = _rms(o, go_ref[...]) * _silu(gd)
        bl = b[L - 1:L]
        kw = (k * jnp.exp(bl - b)).astype(BF16)
        S_ref[0, h] = _col_of_row(jnp.exp(bl)) * S + _dot_tn(kw, vb)


def hgrn_prompt(d, w_lb, g_out, layer):
    B, T, W = d.shape
    L = CHUNK_D
    HD = H_D * DK_D
    return pl.pallas_call(
        functools.partial(_hgrn_body, layer=layer),
        grid=(B, T // L),
        in_specs=[pl.BlockSpec((1, L, W), lambda b, c: (b, c, 0)),
                  pl.BlockSpec(w_lb.shape, lambda b, c: (0, 0)),
                  pl.BlockSpec((1, DV_D), lambda b, c: (0, 0))],
        out_specs=[pl.BlockSpec((1, L, H_D * DV_D), lambda b, c: (b, c, 0)),
                   pl.BlockSpec((1, H_D, DK_D, DV_D), lambda b, c: (b, 0, 0, 0))],
        out_shape=[jax.ShapeDtypeStruct((B, T, H_D * DV_D), F32),
                   jax.ShapeDtypeStruct((B, H_D, DK_D, DV_D), F32)],
        compiler_params=_params(("parallel", "arbitrary")),
        name="hgrn_prompt",
    )(d, w_lb, g_out.reshape(1, DV_D))


def _hgrn_step_body(d_ref, wlb_ref, go_ref, S0_ref, o_ref, S_ref, *, layer):
    HD = H_D * DK_D
    lf, kd = _hgrn_gates(d_ref[0, :, HD:2 * HD], wlb_ref, layer)
    eb = jnp.exp(lf)
    for h in range(H_D):
        sl = slice(h * DK_D, (h + 1) * DK_D)
        q = d_ref[0, :, sl]
        v = d_ref[0, :, 2 * HD + h * DV_D:2 * HD + (h + 1) * DV_D]
        gd = d_ref[0, :, 3 * HD + h * DV_D:3 * HD + (h + 1) * DV_D]
        k = kd[:, sl]
        e = eb[:, sl]
        S = S0_ref[0, h]
        a = jnp.sum(q * k, axis=1, keepdims=True)
        o = a * v + _dot(jnp.broadcast_to(q * e, (8, DK_D)), S, hi=True)[0:1]
        o_ref[0, :, h * DV_D:(h + 1) * DV_D] = _rms(o, go_ref[...]) * _silu(gd)
        S_ref[0, h] = _col_of_row(e) * S + _col_of_row(k) * v


def hgrn_step(d, w_lb, g_out, S0, layer):
    B, _, W = d.shape
    ss = pl.BlockSpec((1, H_D, DK_D, DV_D), lambda b: (b, 0, 0, 0))
    return pl.pallas_call(
        functools.partial(_hgrn_step_body, layer=layer),
        grid=(B,),
        in_specs=[pl.BlockSpec((1, 1, W), lambda b: (b, 0, 0)),
                  pl.BlockSpec(w_lb.shape, lambda b: (0, 0)),
                  pl.BlockSpec((1, DV_D), lambda b: (0, 0)), ss],
        out_specs=[pl.BlockSpec((1, 1, H_D * DV_D), lambda b: (b, 0, 0)), ss],
        out_shape=[jax.ShapeDtypeStruct((B, 1, H_D * DV_D), F32), jax.ShapeDtypeStruct(S0.shape, F32)],
        compiler_params=_params(("parallel",)),
        name="hgrn_step",
    )(d, w_lb, g_out.reshape(1, DV_D), S0)


N_A = 4 * H_A * DK_A
N_B = 3 * len(SWA_GROUPS) * GW_B
N_C = H_C * DH_C
N_D = 4 * H_D * DK_D


def _even_weights(w_in):
    gates = jnp.pad(w_in[:, N_A:N_A + 2 * H_A], ((0, 0), (0, LANES - 2 * H_A)))
    return jnp.concatenate([w_in[:, :N_A], w_in[:, N_A + 2 * H_A:], gates], axis=1)


def _trunk(x, ada, prompt, st_even, st_odd, P):
    hi = not prompt
    G, R, D = x.shape
    wdt = F32 if hi else BF16
    tm_lin = 256
    new_even = new_odd = None
    for l in range(2):
        sh1, sc1, g1, sh2, sc2, g2 = [ada[l][..., i * D:(i + 1) * D] for i in range(6)]
        if l == 0:
            w = _even_weights(P['w_in_even']).astype(wdt)
            splits = (N_A, N_B, LANES)
            if hi:
                outs, off = [], 0
                for n in splits:
                    outs += norm_mod_proj(x, P['g_pre_mix'][l], sc1, sh1, w[:, off:off + n], (n,), hi, tm_lin)
                    off += n
                a, qkv, gg = outs
            else:
                a, qkv, gg = norm_mod_proj(x, P['g_pre_mix'][l], sc1, sh1, w, splits, hi, tm_lin)
            bg = jnp.pad(P['b_gate_a'].reshape(1, 2 * H_A), ((0, 0), (0, LANES - 2 * H_A)))
            if prompt:
                ha, C, n, m = mlstm_prompt(a, gg, bg)
                swa = [swa_prompt(qkv, g, dil) for g, (_, dil) in enumerate(SWA_GROUPS)]
                T = R
                bufs = []
                for g, (win, _) in enumerate(SWA_GROUPS):
                    wb = min(win, T)
                    kb = qkv[:, T - wb:, (3 + g) * GW_B:(4 + g) * GW_B]
                    vb = qkv[:, T - wb:, (6 + g) * GW_B:(7 + g) * GW_B]
                    bufs.append(jnp.concatenate([kb, vb], axis=-1).reshape(G, wb, 2, H_B, DH_B))
                new_even = (C, n, m[:, 0, :H_A], bufs[0], bufs[1], bufs[2])
            else:
                B = R
                C0, n0, m0 = st_even[:3]
                m0p = jnp.pad(m0, ((0, 0), (0, LANES - H_A))).reshape(B, 1, LANES)
                tok = lambda t: t.reshape(B, 1, t.shape[-1])
                ha, C, n, m = mlstm_step(tok(a), tok(gg), bg, C0, n0, m0p)
                swa, bufs = [], []
                ng = len(SWA_GROUPS)
                qkv5 = qkv.reshape(B, 3, ng, H_B, DH_B, 1)
                for g, (_, dil) in enumerate(SWA_GROUPS):
                    buf_t = jnp.transpose(st_even[3 + g], (0, 2, 3, 4, 1))
                    o, lse, nb = swa_step(qkv5[:, 0, g], qkv5[:, 1:, g], buf_t, dil)
                    o, lse = o.reshape(1, B, GW_B), lse.reshape(1, B, GW_B)
                    swa.append(([o[..., p * LANES:(p + 1) * LANES] for p in range(NPAIR_B)],
                                [lse[..., p * LANES:(p + 1) * LANES] for p in range(NPAIR_B)]))
                    bufs.append(jnp.transpose(nb, (0, 4, 1, 2, 3)))
                ha = ha.reshape(1, B, -1)
                new_even = (C, n, m[:, 0, :H_A], bufs[0], bufs[1], bufs[2])
            swa_os = [swa[g][0][p] for p in range(NPAIR_B) for g in range(len(SWA_GROUPS))]
            swa_ls = [swa[g][1][p] for p in range(NPAIR_B) for g in range(len(SWA_GROUPS))]
            x = mix_out(x, g1, P['g_post_mix'][l], [ha], (swa_os, swa_ls), P['w_out_even'].astype(wdt), hi, tm_lin)
        else:
            w = P['w_in_odd'].astype(wdt)
            if prompt:
                w_qd = jnp.concatenate([w[:, :N_C], w[:, 3 * N_C:]], axis=1)
                qc, d, kvt = norm_mod_proj(x, P['g_pre_mix'][l], sc1, sh1, w_qd, (N_C, N_D), hi, tm_lin,
                                           wt=w[:, N_C:3 * N_C].T)
                hc = moba_prompt(qc, kvt)
                od, S = hgrn_prompt(d, P['w_lb'], P['g_out_d'], l)
                new_odd = (jnp.transpose(kvt.reshape(G, 2, H_C, DH_C, R), (0, 4, 1, 2, 3)), S)
            else:
                B = R
                outs, off = [], 0
                for n in (N_C, 2 * N_C, N_D):
                    outs += norm_mod_proj(x, P['g_pre_mix'][l], sc1, sh1, w[:, off:off + n], (n,), hi, tm_lin)
                    off += n
                qc, kvc, d = outs
                pool, page_table, S0 = st_odd
                hc = moba_step(qc.reshape(B, N_C, 1), kvc.reshape(B, 2, H_C, DH_C, 1),
                               jnp.transpose(pool, (0, 2, 3, 4, 1)), page_table)
                od, S = hgrn_step(d.reshape(B, 1, N_D), P['w_lb'], P['g_out_d'], S0, l)
                hc = hc.reshape(1, B, -1)
                od = od.reshape(1, B, -1)
                new_odd = (kvc.reshape(B, 1, 2, H_C, DH_C), S)
            x = mix_out(x, g1, P['g_post_mix'][l], [hc, od], None, P['w_out_odd'].astype(wdt), hi, tm_lin)
        x = ffn(x, P['g_pre_ffn'][l], sc2, sh2, g2, P['g_post_ffn'][l],
                P['w_ffn_gate'][l].astype(wdt), P['w_ffn_up'][l].astype(wdt), P['w_ffn_down'][l].astype(wdt),
                hi, tm=512, tf=256 if hi else 1408)
    return x, new_even, new_odd


def kernel(x_prompt, x_sample, state_mlstm_C, state_mlstm_n, state_mlstm_m, cache_swa_w128, cache_swa_w512, cache_swa_w2048, cache_moba_kv, state_hgrn_S, page_table, c_prompt, c_sample, w_ada, b_ada, g_pre_mix, g_post_mix, g_pre_ffn, g_post_ffn, w_in_even, b_gate_a, w_out_even, w_in_odd, w_lb, g_out_d, w_out_odd, w_ffn_gate, w_ffn_up, w_ffn_down):
    P = {'g_pre_mix': g_pre_mix, 'g_post_mix': g_post_mix, 'g_pre_ffn': g_pre_ffn, 'g_post_ffn': g_post_ffn,
         'w_in_even': w_in_even, 'b_gate_a': b_gate_a, 'w_out_even': w_out_even, 'w_in_odd': w_in_odd,
         'w_lb': w_lb, 'g_out_d': g_out_d, 'w_out_odd': w_out_odd, 'w_ffn_gate': w_ffn_gate,
         'w_ffn_up': w_ffn_up, 'w_ffn_down': w_ffn_down}
    Bp = x_prompt.shape[0]
    Bs = x_sample.shape[0]
    D = x_prompt.shape[-1]
    c_all = jnp.concatenate([c_prompt, c_sample], axis=0)
    c_all = jnp.pad(c_all, ((0, -(Bp + Bs) % 8), (0, 0)))
    ada = ada_all(c_all, w_ada, b_ada)[:, :Bp + Bs]
    ada_p = ada[:, :Bp].reshape(ada.shape[0], Bp, 1, 6 * D)
    ada_s = ada[:, Bp:].reshape(ada.shape[0], 1, Bs, 6 * D)
    y_p, ev_p, od_p = _trunk(x_prompt, ada_p, True, None, None, P)
    y_s, ev_s, od_s = _trunk(
        x_sample.reshape(1, Bs, D), ada_s, False,
        (state_mlstm_C, state_mlstm_n, state_mlstm_m, cache_swa_w128, cache_swa_w512, cache_swa_w2048),
        (cache_moba_kv, page_table, state_hgrn_S), P)
    return ((y_p, y_s.reshape(x_sample.shape)) + tuple(ev_p[:3]) + tuple(ev_s[:3]) + tuple(ev_p[3:])
            + tuple(ev_s[3:]) + (od_p[0], od_s[0], od_p[1], od_s[1]))
```

```python
import functools

import jax
import jax.numpy as jnp
from jax import lax
from jax.experimental import pallas as pl
from jax.experimental.pallas import tpu as pltpu

F32 = jnp.float32
BF16 = jnp.bfloat16
HI = lax.Precision.HIGHEST

EPS = 1e-6
NEG = -1e30

H_A, DK_A, DV_A, CHUNK_A = 4, 128, 128, 128
SWA_GROUPS = ((128, 1), (512, 4), (2048, 16))
H_B, DH_B, SWA_BAND = 4, 64, 128
GW_B = H_B * DH_B
H_C, DH_C, MOBA_BLOCK, MOBA_TOPK = 8, 64, 256, 3
H_D, DK_D, DV_D, CHUNK_D, SUB_D = 4, 128, 128, 64, 8

LANES = 128
VMEM_LIMIT = 56 << 20


def _params(sem, big=False):
    return pltpu.CompilerParams(dimension_semantics=sem,
                                vmem_limit_bytes=VMEM_LIMIT if big else None)


def _dot(a, b, hi=False):
    return jnp.dot(a, b, preferred_element_type=F32, precision=HI if hi else None)


def _dot_nt(a, b, hi=False):
    return lax.dot_general(a, b, (((1,), (1,)), ((), ())), preferred_element_type=F32,
                           precision=HI if hi else None)


def _dot_tn(a, b, hi=False):
    return lax.dot_general(a, b, (((0,), (0,)), ((), ())), preferred_element_type=F32,
                           precision=HI if hi else None)


def _rms(x, g):
    return x * lax.rsqrt(jnp.mean(x * x, axis=-1, keepdims=True) + EPS) * g


def _sigmoid(x):
    return 1.0 / (1.0 + jnp.exp(-x))


def _silu(x):
    return x * _sigmoid(x)


def _log_sigmoid(x):
    return jnp.minimum(x, 0.0) - jnp.log(1.0 + jnp.exp(-jnp.abs(x)))


def _col_of_row(row):
    n = row.shape[-1]
    eye = lax.broadcasted_iota(jnp.int32, (n, n), 0) == lax.broadcasted_iota(jnp.int32, (n, n), 1)
    return jnp.sum(jnp.where(eye, row, 0.0), axis=1, keepdims=True)


def _ada_body(c_ref, w_ref, b_ref, o_ref):
    o_ref[0] = _dot(_silu(c_ref[...]), w_ref[0], hi=True) + b_ref[0]


def ada_all(c, w_ada, b_ada, tn=768):
    depth, d, n = w_ada.shape
    r = c.shape[0]
    return pl.pallas_call(
        _ada_body,
        grid=(depth, n // tn),
        in_specs=[pl.BlockSpec((r, d), lambda l, j: (0, 0)),
                  pl.BlockSpec((1, d, tn), lambda l, j: (l, 0, j)),
                  pl.BlockSpec((1, 1, tn), lambda l, j: (l, 0, j))],
        out_specs=pl.BlockSpec((1, r, tn), lambda l, j: (l, 0, j)),
        out_shape=jax.ShapeDtypeStruct((depth, r, n), F32),
        compiler_params=_params(("parallel", "parallel")),
        name="ada",
    )(c, w_ada, b_ada.reshape(depth, 1, n))


def _proj_body(x_ref, g_ref, sc_ref, sh_ref, w_ref, *refs, splits, hi, cw, t_splits):
    h = _rms(x_ref[0], g_ref[...]) * (1.0 + sc_ref[0]) + sh_ref[0]
    hc = h if hi else h.astype(BF16)
    o_refs = refs[1:] if t_splits else refs
    off = 0
    for o_ref, n in zip(o_refs, splits):
        for c0 in range(0, n, cw):
            c1 = min(c0 + cw, n)
            o_ref[0, :, c0:c1] = _dot(hc, w_ref[:, off + c0:off + c1], hi)
        off += n
    if t_splits:
        wt_ref, off = refs[0], 0
        for ot_ref, n in zip(refs[1 + len(splits):], t_splits):
            for r0 in range(0, n, cw):
                ot_ref[0, r0:r0 + cw, :] = _dot_nt(wt_ref[off + r0:off + r0 + cw, :], hc, hi)
            off += n


def _mod_spec(mod, tm):
    if mod.shape[1] == 1:
        return pl.BlockSpec((1, 1, mod.shape[2]), lambda gi, ri: (gi, 0, 0))
    return pl.BlockSpec((1, tm, mod.shape[2]), lambda gi, ri: (gi, ri, 0))


def norm_mod_proj(x, g, sc, sh, w, splits, hi, tm, wt=None, t_splits=()):
    G, R, D = x.shape
    tm = min(tm, R)
    n_all = sum(splits)
    n_t = sum(t_splits)
    ops = [x, g.reshape(1, D), sc, sh, w]
    in_specs = [pl.BlockSpec((1, tm, D), lambda gi, ri: (gi, ri, 0)),
                pl.BlockSpec((1, D), lambda gi, ri: (0, 0)),
                _mod_spec(sc, tm), _mod_spec(sh, tm),
                pl.BlockSpec((D, n_all), lambda gi, ri: (0, 0))]
    out_specs = [pl.BlockSpec((1, tm, n), lambda gi, ri: (gi, ri, 0)) for n in splits]
    out_shape = [jax.ShapeDtypeStruct((G, R, n), F32) for n in splits]
    if n_t:
        ops.append(wt)
        in_specs.append(pl.BlockSpec((n_t, D), lambda gi, ri: (0, 0)))
        out_specs += [pl.BlockSpec((1, n, tm), lambda gi, ri: (gi, 0, ri)) for n in t_splits]
        out_shape += [jax.ShapeDtypeStruct((G, n, R), F32) for n in t_splits]
    return pl.pallas_call(
        functools.partial(_proj_body, splits=tuple(splits), hi=hi, cw=512, t_splits=tuple(t_splits)),
        grid=(G, R // tm),
        in_specs=in_specs,
        out_specs=out_specs,
        out_shape=out_shape,
        compiler_params=_params(("parallel", "parallel"), big=True),
        name="proj",
    )(*ops)


def _mixout_body(*refs, n_parts, swa, hi):
    x_ref, g1_ref, gp_ref = refs[:3]
    part_refs = refs[3:3 + n_parts]
    pos = 3 + n_parts
    ng = len(SWA_GROUPS)
    o_refs = refs[pos:pos + swa * ng]
    l_refs = refs[pos + swa * ng:pos + 2 * swa * ng]
    pos += 2 * swa * ng
    w_ref, out_ref = refs[pos], refs[pos + 1]

    def cast(a):
        return a if hi else a.astype(BF16)

    acc = None
    off = 0
    for p_ref in part_refs:
        a = p_ref[0]
        k = a.shape[-1]
        t = _dot(cast(a), w_ref[off:off + k, :], hi)
        acc = t if acc is None else acc + t
        off += k
    for c in range(swa):
        ls = [l_ref[0] for l_ref in l_refs[c * ng:(c + 1) * ng]]
        os_ = [o_ref[0] for o_ref in o_refs[c * ng:(c + 1) * ng]]
        mx = functools.reduce(jnp.maximum, ls)
        es = [jnp.exp(l - mx) for l in ls]
        hb = sum(e * o for e, o in zip(es, os_)) / sum(es)
        acc = acc + _dot(cast(hb), w_ref[off:off + hb.shape[-1], :], hi)
        off += hb.shape[-1]
    out_ref[0] = x_ref[0] + g1_ref[0] * _rms(acc, gp_ref[...])


def mix_out(x, g1, gpost, parts, swa_parts, w, hi, tm):
    G, R, D = x.shape
    tm = min(tm, R)
    row = lambda a: pl.BlockSpec((1, tm, a.shape[2]), lambda gi, ri: (gi, ri, 0))
    ops = [x, g1, gpost.reshape(1, D)] + list(parts)
    specs = [row(x), _mod_spec(g1, tm), pl.BlockSpec((1, D), lambda gi, ri: (0, 0))] + [row(p) for p in parts]
    swa = 0
    if swa_parts is not None:
        swa = len(swa_parts[0]) // len(SWA_GROUPS)
        ops += list(swa_parts[0]) + list(swa_parts[1])
        specs += [row(a) for a in ops[-2 * len(swa_parts[0]):]]
    ops.append(w)
    specs.append(pl.BlockSpec(w.shape, lambda gi, ri: (0, 0)))
    return pl.pallas_call(
        functools.partial(_mixout_body, n_parts=len(parts), swa=swa, hi=hi),
        grid=(G, R // tm),
        in_specs=specs,
        out_specs=row(x),
        out_shape=jax.ShapeDtypeStruct(x.shape, F32),
        compiler_params=_params(("parallel", "parallel"), big=True),
        name="mix_out",
    )(*ops)


def _ffn_body(x_ref, gpre_ref, sc_ref, sh_ref, g2_ref, gpost_ref, wg_ref, wu_ref, wd_ref, out_ref,
              h_scr, acc_scr, *, hi):
    k = pl.program_id(2)

    @pl.when(k == 0)
    def _():
        h = _rms(x_ref[0], gpre_ref[...]) * (1.0 + sc_ref[0]) + sh_ref[0]
        h_scr[...] = h.astype(h_scr.dtype)
        acc_scr[...] = jnp.zeros_like(acc_scr)

    h = h_scr[...]
    a = _silu(_dot(h, wg_ref[...], hi)) * _dot(h, wu_ref[...], hi)
    acc_scr[...] += _dot(a.astype(h.dtype), wd_ref[...], hi)

    @pl.when(k == pl.num_programs(2) - 1)
    def _():
        out_ref[0] = x_ref[0] + g2_ref[0] * _rms(acc_scr[...], gpost_ref[...])


def ffn(x, gpre, sc, sh, g2, gpost, wg, wu, wd, hi, tm, tf):
    G, R, D = x.shape
    FF = wg.shape[1]
    tm = min(tm, R)
    mspec = lambda m: (pl.BlockSpec((1, 1, D), lambda gi, ri, k: (gi, 0, 0)) if m.shape[1] == 1
                       else pl.BlockSpec((1, tm, D), lambda gi, ri, k: (gi, ri, 0)))
    vec = pl.BlockSpec((1, D), lambda gi, ri, k: (0, 0))
    return pl.pallas_call(
        functools.partial(_ffn_body, hi=hi),
        grid=(G, R // tm, FF // tf),
        in_specs=[pl.BlockSpec((1, tm, D), lambda gi, ri, k: (gi, ri, 0)), vec, mspec(sc), mspec(sh),
                  mspec(g2), vec,
                  pl.BlockSpec((D, tf), lambda gi, ri, k: (0, k)),
                  pl.BlockSpec((D, tf), lambda gi, ri, k: (0, k)),
                  pl.BlockSpec((tf, D), lambda gi, ri, k: (k, 0))],
        out_specs=pl.BlockSpec((1, tm, D), lambda gi, ri, k: (gi, ri, 0)),
        out_shape=jax.ShapeDtypeStruct(x.shape, F32),
        scratch_shapes=[pltpu.VMEM((tm, D), F32 if hi else BF16), pltpu.VMEM((tm, D), F32)],
        compiler_params=_params(("parallel", "parallel", "arbitrary"), big=True),
        name="ffn",
    )(x, gpre.reshape(1, D), sc, sh, g2, gpost.reshape(1, D), wg, wu, wd)


def _mlstm_body(a_ref, g_ref, bg_ref, h_ref, C_ref, n_ref, m_ref):
    L = CHUNK_A
    HD = H_A * DK_A

    @pl.when(pl.program_id(0) == 0)
    def _():
        C_ref[...] = jnp.zeros_like(C_ref)
        n_ref[...] = jnp.zeros_like(n_ref)
        m_ref[...] = jnp.zeros_like(m_ref)

    nb = a_ref.shape[0]
    lane = lax.broadcasted_iota(jnp.int32, (L, LANES), 1)
    ri = lax.broadcasted_iota(jnp.int32, (L, L), 0)
    ci = lax.broadcasted_iota(jnp.int32, (L, L), 1)
    causal = ci <= ri
    tril = causal.astype(F32)
    m_lane = lax.broadcasted_iota(jnp.int32, (1, LANES), 1)
    gates_t, bcum, bcum_t, m_all = [], [], [], []
    for b in range(nb):
        pre = g_ref[b] + bg_ref[...]
        gates = jnp.where(lane >= H_A, _log_sigmoid(pre), pre)
        bcum.append(_dot(tril, gates, hi=True))
        gates_t.append(gates.T)
        bcum_t.append(bcum[b].T)
        m_all.append(m_ref[b])
    m_out = list(m_all)
    chains = [(b, h) for b in range(nb) for h in range(H_A)]
    q = [a_ref[b, :, h * DK_A:(h + 1) * DK_A] for b, h in chains]
    k = [a_ref[b, :, HD + h * DK_A:HD + (h + 1) * DK_A] * (DK_A ** -0.5) for b, h in chains]
    vb = [a_ref[b, :, 2 * HD + h * DV_A:2 * HD + (h + 1) * DV_A].astype(BF16) for b, h in chains]
    qb = [x.astype(BF16) for x in q]
    C = [C_ref[b, h] for b, h in chains]
    n_row = [n_ref[b, h:h + 1, :] for b, h in chains]
    qk = [_dot_nt(qb[c], k[c].astype(BF16)) for c in range(len(chains))]
    qC = [_dot(qb[c], C[c].astype(BF16)) for c in range(len(chains))]
    s, winter, mt, kw, ws, wc = [], [], [], [], [], []
    for c, (b, h) in enumerate(chains):
        ig_row = gates_t[b][h:h + 1, :]
        b_col = bcum[b][:, H_A + h:H_A + h + 1]
        b_row = bcum_t[b][H_A + h:H_A + h + 1, :]
        m = m_all[b][:, h:h + 1]
        dmat = jnp.where(causal, b_col - b_row + ig_row, NEG)
        inter = b_col + m
        mt.append(jnp.maximum(inter, jnp.max(dmat, axis=1, keepdims=True)))
        winter.append(jnp.exp(inter - mt[c]))
        s.append(qk[c] * jnp.exp(dmat - mt[c]))
        bl = b_row[:, L - 1:L]
        gl = bl - b_row + ig_row
        m_new = jnp.maximum(bl + m, jnp.max(gl, axis=1, keepdims=True))
        ws.append(jnp.exp(gl - m_new))
        wc.append(jnp.exp(bl + m - m_new))
        kw.append((k[c].T * ws[c]).astype(BF16))
        m_out[b] = jnp.where(m_lane == h, m_new, m_out[b])
    sv = [_dot(s[c].astype(BF16), vb[c]) for c in range(len(chains))]
    kv = [_dot(kw[c], vb[c]) for c in range(len(chains))]
    kn = [_dot(jnp.broadcast_to(ws[c], (8, L)), k[c], hi=True)[0:1] for c in range(len(chains))]
    for c, (b, h) in enumerate(chains):
        og = a_ref[b, :, 3 * HD + h * DV_A:3 * HD + (h + 1) * DV_A]
        num = sv[c] + winter[c] * qC[c]
        den = jnp.sum(s[c], axis=1, keepdims=True) + winter[c] * jnp.sum(q[c] * n_row[c], axis=1, keepdims=True)
        h_ref[b, :, h * DV_A:(h + 1) * DV_A] = _sigmoid(og) * (num / jnp.maximum(jnp.abs(den), jnp.exp(-mt[c])))
        C_ref[b, h] = wc[c] * C[c] + kv[c]
        n_ref[b, h:h + 1, :] = wc[c] * n_row[c] + kn[c]
    for b in range(nb):
        m_ref[b] = m_out[b]


def mlstm_prompt(a, g, bg):
    B, T, _ = a.shape
    L = CHUNK_A
    return pl.pallas_call(
        _mlstm_body,
        grid=(T // L,),
        in_specs=[pl.BlockSpec((B, L, a.shape[2]), lambda c: (0, c, 0)),
                  pl.BlockSpec((B, L, LANES), lambda c: (0, c, 0)),
                  pl.BlockSpec((1, LANES), lambda c: (0, 0))],
        out_specs=[pl.BlockSpec((B, L, H_A * DV_A), lambda c: (0, c, 0)),
                   pl.BlockSpec((B, H_A, DK_A, DV_A), lambda c: (0, 0, 0, 0)),
                   pl.BlockSpec((B, H_A, DK_A), lambda c: (0, 0, 0)),
                   pl.BlockSpec((B, 1, LANES), lambda c: (0, 0, 0))],
        out_shape=[jax.ShapeDtypeStruct((B, T, H_A * DV_A), F32),
                   jax.ShapeDtypeStruct((B, H_A, DK_A, DV_A), F32),
                   jax.ShapeDtypeStruct((B, H_A, DK_A), F32),
                   jax.ShapeDtypeStruct((B, 1, LANES), F32)],
        compiler_params=_params(("arbitrary",), big=True),
        name="mlstm_prompt",
    )(a, g, bg)


def _mlstm_step_body(a_ref, g_ref, bg_ref, C0_ref, n0_ref, m0_ref, h_ref, C_ref, n_ref, m_ref):
    HD = H_A * DK_A
    pre = g_ref[0] + bg_ref[...]
    lane = lax.broadcasted_iota(jnp.int32, (1, LANES), 1)
    gates = jnp.where(lane >= H_A, _log_sigmoid(pre), pre)
    m_all = m0_ref[0]
    m_out = m_all
    for h in range(H_A):
        q = a_ref[0, :, h * DK_A:(h + 1) * DK_A]
        k = a_ref[0, :, HD + h * DK_A:HD + (h + 1) * DK_A] * (DK_A ** -0.5)
        v = a_ref[0, :, 2 * HD + h * DV_A:2 * HD + (h + 1) * DV_A]
        og = a_ref[0, :, 3 * HD + h * DV_A:3 * HD + (h + 1) * DV_A]
        ig = gates[:, h:h + 1]
        lf = gates[:, H_A + h:H_A + h + 1]
        m = m_all[:, h:h + 1]
        C = C0_ref[0, h]
        n_row = n0_ref[0, h:h + 1, :]
        inter = lf + m
        mt = jnp.maximum(inter, ig)
        s = jnp.sum(q * k, axis=1, keepdims=True) * jnp.exp(ig - mt)
        winter = jnp.exp(inter - mt)
        qC = _dot(jnp.broadcast_to(q, (8, DK_A)), C, hi=True)[0:1]
        num = s * v + winter * qC
        den = s + winter * jnp.sum(q * n_row, axis=1, keepdims=True)
        hh = num / jnp.maximum(jnp.abs(den), jnp.exp(-mt))
        h_ref[0, :, h * DV_A:(h + 1) * DV_A] = _sigmoid(og) * hh
        m_new = jnp.maximum(inter, ig)
        ws = jnp.exp(ig - m_new)
        wc = jnp.exp(inter - m_new)
        C_ref[0, h] = wc * C + (ws * _col_of_row(k)) * v
        n_ref[0, h:h + 1, :] = wc * n_row + ws * k
        m_out = jnp.where(lane == h, m_new, m_out)
    m_ref[0] = m_out


def mlstm_step(a, g, bg, C0, n0, m0):
    B = a.shape[0]
    r3 = lambda w: pl.BlockSpec((1, 1, w), lambda b: (b, 0, 0))
    cs = pl.BlockSpec((1, H_A, DK_A, DV_A), lambda b: (b, 0, 0, 0))
    ns = pl.BlockSpec((1, H_A, DK_A), lambda b: (b, 0, 0))
    return pl.pallas_call(
        _mlstm_step_body,
        grid=(B,),
        in_specs=[r3(a.shape[2]), r3(LANES), pl.BlockSpec((1, LANES), lambda b: (0, 0)), cs, ns, r3(LANES)],
        out_specs=[r3(H_A * DV_A), cs, ns, r3(LANES)],
        out_shape=[jax.ShapeDtypeStruct((B, 1, H_A * DV_A), F32),
                   jax.ShapeDtypeStruct(C0.shape, F32), jax.ShapeDtypeStruct(n0.shape, F32),
                   jax.ShapeDtypeStruct((B, 1, LANES), F32)],
        compiler_params=_params(("parallel",)),
        name="mlstm_step",
    )(a, g, bg, C0, n0, m0)


NPAIR_B = GW_B // LANES


def _swa_body(*refs, dil):
    Q = SWA_BAND
    q_refs, kc_refs, kp_refs, vc_refs, vp_refs = [refs[i * NPAIR_B:(i + 1) * NPAIR_B] for i in range(5)]
    o_refs = refs[5 * NPAIR_B:6 * NPAIR_B]
    l_refs = refs[6 * NPAIR_B:7 * NPAIR_B]
    qi = lax.broadcasted_iota(jnp.int32, (Q, Q), 0)
    kj = lax.broadcasted_iota(jnp.int32, (Q, Q), 1)
    mask_c = kj <= qi
    mask_p = jnp.logical_and(kj >= qi, pl.program_id(1) > 0)
    lane = lax.broadcasted_iota(jnp.int32, (Q, LANES), 1)
    first = lane < DH_B
    chains = [(p, hh) for p in range(NPAIR_B) for hh in range(2)]
    for r in range(dil):
        rows = (pl.ds(0, 1), pl.ds(r, Q, stride=dil) if dil > 1 else pl.ds(0, Q), slice(None))
        ld = lambda ref: ref[rows][0]
        q = [ld(ref) * (DH_B ** -0.5) for ref in q_refs]
        kc = [ld(ref).astype(BF16) for ref in kc_refs]
        kp = [ld(ref).astype(BF16) for ref in kp_refs]
        vc = [ld(ref).astype(BF16) for ref in vc_refs]
        vp = [ld(ref).astype(BF16) for ref in vp_refs]
        qm = [jnp.where(first if hh == 0 else ~first, q[p], 0.0).astype(BF16) for p, hh in chains]
        sc = [_dot_nt(qm[c], kc[p]) for c, (p, hh) in enumerate(chains)]
        sp = [_dot_nt(qm[c], kp[p]) for c, (p, hh) in enumerate(chains)]
        pcs, pps, dens, lses = [], [], [], []
        for c in range(len(chains)):
            s_c = jnp.where(mask_c, sc[c], NEG)
            s_p = jnp.where(mask_p, sp[c], NEG)
            mx = jnp.maximum(jnp.max(s_c, axis=1, keepdims=True), jnp.max(s_p, axis=1, keepdims=True))
            pc = jnp.exp(s_c - mx)
            pp = jnp.exp(s_p - mx)
            den = jnp.sum(pc, axis=1, keepdims=True) + jnp.sum(pp, axis=1, keepdims=True)
            pcs.append(pc.astype(BF16))
            pps.append(pp.astype(BF16))
            dens.append(den)
            lses.append(mx + jnp.log(den))
        os_ = [(_dot(pcs[c], vc[p]) + _dot(pps[c], vp[p])) / dens[c] for c, (p, hh) in enumerate(chains)]
        for p in range(NPAIR_B):
            o_refs[p][rows] = jnp.where(first, os_[2 * p], os_[2 * p + 1])[None]
            l_refs[p][rows] = jnp.where(first, lses[2 * p], lses[2 * p + 1])[None]


def swa_prompt(qkv, g, dil):
    B, T, W = qkv.shape
    ng = W // GW_B // 3
    unit = SWA_BAND * dil
    col = lambda part, p: (part * ng + g) * NPAIR_B + p
    cur = lambda part: [pl.BlockSpec((1, unit, LANES), functools.partial(lambda b, n, c: (b, n, c), c=col(part, p)))
                        for p in range(NPAIR_B)]
    prev = lambda part: [pl.BlockSpec((1, unit, LANES),
                                      functools.partial(lambda b, n, c: (b, jnp.maximum(n - 1, 0), c), c=col(part, p)))
                         for p in range(NPAIR_B)]
    ospec = pl.BlockSpec((1, unit, LANES), lambda b, n: (b, n, 0))
    res = pl.pallas_call(
        functools.partial(_swa_body, dil=dil),
        grid=(B, T // unit),
        in_specs=cur(0) + cur(1) + prev(1) + cur(2) + prev(2),
        out_specs=[ospec] * (2 * NPAIR_B),
        out_shape=[jax.ShapeDtypeStruct((B, T, LANES), F32)] * (2 * NPAIR_B),
        compiler_params=_params(("parallel", "parallel"), big=True),
        name="swa_prompt_d%d" % dil,
    )(*([qkv] * (5 * NPAIR_B)))
    return res[:NPAIR_B], res[NPAIR_B:]


def _swa_step_body(q_ref, kvn_ref, buf_ref, o_ref, l_ref, nb_ref, *, dil):
    wb = buf_ref.shape[-1]
    lane = lax.broadcasted_iota(jnp.int32, (1, wb), 1)
    read = (lane % dil) == 0
    last = lane == wb - 1
    for h in range(H_B):
        q = q_ref[0, h]
        kn = kvn_ref[0, 0, h]
        vn = kvn_ref[0, 1, h]
        kt = buf_ref[0, 0, h]
        vt = buf_ref[0, 1, h]
        s = jnp.where(read, jnp.sum(kt * q, axis=0, keepdims=True) * (DH_B ** -0.5), NEG)
        s_self = jnp.sum(kn * q, axis=0, keepdims=True) * (DH_B ** -0.5)
        mx = jnp.maximum(jnp.max(s, axis=1, keepdims=True), s_self)
        p = jnp.exp(s - mx)
        p_self = jnp.exp(s_self - mx)
        den = jnp.sum(p, axis=1, keepdims=True) + p_self
        o_ref[0, h] = (jnp.sum(vt * p, axis=1, keepdims=True) + p_self * vn) / den
        l_ref[0, h] = jnp.broadcast_to(mx + jnp.log(den), (DH_B, 1))
        nb_ref[0, 0, h] = jnp.where(last, kn, pltpu.roll(kt, wb - 1, axis=1))
        nb_ref[0, 1, h] = jnp.where(last, vn, pltpu.roll(vt, wb - 1, axis=1))


def swa_step(q, kvn, buf_t, dil):
    B = q.shape[0]
    qs = pl.BlockSpec((1,) + q.shape[1:], lambda b: (b, 0, 0, 0))
    ks = pl.BlockSpec((1,) + kvn.shape[1:], lambda b: (b, 0, 0, 0, 0))
    bs = pl.BlockSpec((1,) + buf_t.shape[1:], lambda b: (b, 0, 0, 0, 0))
    return pl.pallas_call(
        functools.partial(_swa_step_body, dil=dil),
        grid=(B,),
        in_specs=[qs, ks, bs],
        out_specs=[qs, qs, bs],
        out_shape=[jax.ShapeDtypeStruct(q.shape, F32)] * 2 + [jax.ShapeDtypeStruct(buf_t.shape, F32)],
        compiler_params=_params(("parallel",), big=True),
        name="swa_step_d%d" % dil,
    )(q, kvn, buf_t)


AUG = 2 * DH_C


VROWS = DH_C + 16
TQ_C = 128
HEADS_C = 4
LOG2E = 1.4426950408889634


def _moba_kprep_body(kt_ref, vt_ref, km_ref, ka_ref, vta_ref, *, nblk):
    n = pl.program_id(1)
    k = kt_ref[0].T
    km_ref[0, pl.ds(n, 1), :] = jnp.mean(k, axis=0, keepdims=True)
    lane = lax.broadcasted_iota(jnp.int32, (MOBA_BLOCK, DH_C), 1)
    onehot = jnp.where(lane == n, 1.0, 0.0).astype(BF16)
    pieces = []
    for h in range(H_C):
        pieces += [k[:, h * DH_C:(h + 1) * DH_C].astype(BF16), onehot]
    ka_ref[0] = jnp.concatenate(pieces, axis=1)
    vt = vt_ref[0]
    ones = jnp.where(lax.broadcasted_iota(jnp.int32, (VROWS - DH_C, MOBA_BLOCK), 0) == 0, 1.0, 0.0).astype(BF16)
    pieces = []
    for h in range(H_C):
        pieces += [vt[h * DH_C:(h + 1) * DH_C].astype(BF16), ones]
    vta_ref[0] = jnp.concatenate(pieces, axis=0)


def _moba_gate_body(qt_ref, km_ref, qa_ref, *, nblk):
    own = pl.program_id(1)
    blk = lax.broadcasted_iota(jnp.int32, (nblk, MOBA_BLOCK), 0)
    blk_f = blk.astype(F32)
    past = blk < own
    gs = [_dot(km_ref[0, :, h * DH_C:(h + 1) * DH_C], qt_ref[0, h * DH_C:(h + 1) * DH_C, :], hi=True)
          for h in range(H_C)]
    for h in range(H_C):
        g = jnp.where(past, gs[h], NEG)
        sel = blk == own
        for _ in range(MOBA_TOPK):
            mx = jnp.max(g, axis=0, keepdims=True)
            first = jnp.min(jnp.where(g == mx, blk_f, float(nblk)), axis=0, keepdims=True)
            pick = blk_f == first
            sel = jnp.logical_or(sel, jnp.logical_and(pick, past))
            g = jnp.where(pick, -jnp.inf, g)
        bias = jnp.where(sel, 0.0, NEG)
        aug = jnp.concatenate([qt_ref[0, h * DH_C:(h + 1) * DH_C, :] * (DH_C ** -0.5 * LOG2E), bias,
                               jnp.zeros((AUG - DH_C - nblk, MOBA_BLOCK), F32)], axis=0)
        qa_ref[0, h * AUG:(h + 1) * AUG, :] = aug.astype(BF16)


def _moba_attn_body(qt_ref, ka_ref, vta_ref, o_ref):
    own = pl.program_id(2)
    TK = MOBA_BLOCK
    nsub = TK // TQ_C
    chains = [(hh, qs) for hh in range(HEADS_C) for qs in range(nsub)]
    qts = [qt_ref[0, hh * AUG:(hh + 1) * AUG, qs * TQ_C:(qs + 1) * TQ_C] for hh, qs in chains]

    def block(n, hh):
        r0 = pl.multiple_of(n * TK, TK)
        return (ka_ref[0, pl.ds(r0, TK), hh * AUG:(hh + 1) * AUG],
                vta_ref[0, hh * VROWS:(hh + 1) * VROWS, pl.ds(r0, TK)])

    key = lax.broadcasted_iota(jnp.int32, (TK, TQ_C), 0)
    qry = lax.broadcasted_iota(jnp.int32, (TK, TQ_C), 1)

    def scores(n):
        ks = [block(n, hh)[0] for hh in range(HEADS_C)]
        return [_dot(ks[hh], qts[c]) for c, (hh, qs) in enumerate(chains)]

    def values(n, ps):
        vs = [block(n, hh)[1] for hh in range(HEADS_C)]
        return [_dot(vs[hh], ps[c]) for c, (hh, qs) in enumerate(chains)]

    def softmax(ss, ms_prev, diagonal):
        ms, ps, alphas = [], [], []
        for c, (hh, qs) in enumerate(chains):
            s = jnp.where(key <= qry + qs * TQ_C, ss[c], NEG) if diagonal else ss[c]
            mx = jnp.max(s, axis=0, keepdims=True)
            m_new = mx if ms_prev is None else jnp.maximum(ms_prev[c], mx)
            ms.append(m_new)
            ps.append(jnp.exp2(s - m_new).astype(BF16))
            alphas.append(jnp.ones_like(mx) if ms_prev is None else jnp.exp2(ms_prev[c] - m_new))
        return tuple(ms), tuple(ps), tuple(alphas)

    def accumulate(accs, alphas, pv):
        return tuple(alphas[c] * accs[c] + pv[c] for c in range(len(chains)))

    ms, ps, alphas = softmax(scores(own), None, True)
    accs = tuple(jnp.zeros((VROWS, TQ_C), F32) for _ in chains)

    def body(n, state):
        ms, accs, ps, alphas = state
        ss = scores(n)
        pv = values(jnp.where(n == 0, own, n - 1), ps)
        accs = accumulate(accs, alphas, pv)
        ms, ps, alphas = softmax(ss, ms, False)
        return ms, accs, ps, alphas

    ms, accs, ps, alphas = lax.fori_loop(0, own, body, (ms, accs, ps, alphas))
    accs = accumulate(accs, alphas, values(jnp.maximum(own - 1, 0), ps))
    for qs in range(nsub):
        sub = [accs[c] for c, (hh, q2) in enumerate(chains) if q2 == qs]
        o = jnp.concatenate([a[:DH_C] / a[DH_C:DH_C + 1] for a in sub], axis=0)
        o_ref[0, qs * TQ_C:(qs + 1) * TQ_C, :] = o.T


def moba_prompt(q_t, kvt):
    B, W, T = q_t.shape
    nblk = T // MOBA_BLOCK
    assert nblk <= AUG - DH_C and nblk % 8 == 0
    km, ka, vta = pl.pallas_call(
        functools.partial(_moba_kprep_body, nblk=nblk),
        grid=(B, nblk),
        in_specs=[pl.BlockSpec((1, W, MOBA_BLOCK), lambda b, n: (b, 0, n)),
                  pl.BlockSpec((1, W, MOBA_BLOCK), lambda b, n: (b, 1, n))],
        out_specs=[pl.BlockSpec((1, nblk, W), lambda b, n: (b, 0, 0)),
                   pl.BlockSpec((1, MOBA_BLOCK, H_C * AUG), lambda b, n: (b, n, 0)),
                   pl.BlockSpec((1, H_C * VROWS, MOBA_BLOCK), lambda b, n: (b, 0, n))],
        out_shape=[jax.ShapeDtypeStruct((B, nblk, W), F32),
                   jax.ShapeDtypeStruct((B, T, H_C * AUG), BF16),
                   jax.ShapeDtypeStruct((B, H_C * VROWS, T), BF16)],
        compiler_params=_params(("parallel", "arbitrary")),
        name="moba_kprep",
    )(kvt, kvt)
    qt = pl.pallas_call(
        functools.partial(_moba_gate_body, nblk=nblk),
        grid=(B, nblk),
        in_specs=[pl.BlockSpec((1, W, MOBA_BLOCK), lambda b, i: (b, 0, i)),
                  pl.BlockSpec((1, nblk, W), lambda b, i: (b, 0, 0))],
        out_specs=pl.BlockSpec((1, H_C * AUG, MOBA_BLOCK), lambda b, i: (b, 0, i)),
        out_shape=jax.ShapeDtypeStruct((B, H_C * AUG, T), BF16),
        compiler_params=_params(("parallel", "parallel")),
        name="moba_gate",
    )(q_t, km)
    return pl.pallas_call(
        _moba_attn_body,
        grid=(B, H_C // HEADS_C, nblk),
        in_specs=[pl.BlockSpec((1, HEADS_C * AUG, MOBA_BLOCK), lambda b, p, j: (b, p, j)),
                  pl.BlockSpec((1, T, HEADS_C * AUG), lambda b, p, j: (b, 0, p)),
                  pl.BlockSpec((1, HEADS_C * VROWS, T), lambda b, p, j: (b, p, 0))],
        out_specs=pl.BlockSpec((1, MOBA_BLOCK, HEADS_C * DH_C), lambda b, p, j: (b, j, p)),
        out_shape=jax.ShapeDtypeStruct((B, T, W), F32),
        compiler_params=_params(("parallel", "parallel", "arbitrary"), big=True),
        name="moba_attn",
    )(qt, ka, vta)


PAGES_PER_STEP = 32
HEADS_STEP_C = 4


def _moba_step_gate_body(pt_ref, q_ref, *refs, pages_per_block, n_past):
    page_refs = refs[:PAGES_PER_STEP]
    sel_ref, g_scr = refs[PAGES_PER_STEP], refs[PAGES_PER_STEP + 1]
    s = pl.program_id(1)
    q = q_ref[0]
    W = q.shape[0]
    rows = page_refs[0].shape[-1]
    per_step = PAGES_PER_STEP // pages_per_block
    lane = lax.broadcasted_iota(jnp.int32, (W, LANES), 1)

    @pl.when(s == 0)
    def _():
        g_scr[...] = jnp.zeros_like(g_scr)

    g = g_scr[...]
    for j in range(per_step):
        tot = None
        for i in range(pages_per_block):
            t = page_refs[j * pages_per_block + i][0, 0].reshape(W, rows)
            tot = t if tot is None else tot + t
        kmean = jnp.sum(tot, axis=1, keepdims=True) / (pages_per_block * rows)
        g = jnp.where(lane == s * per_step + j, q * kmean, g)
    g_scr[...] = g

    @pl.when(s == pl.num_programs(1) - 1)
    def _():
        seg = (lax.broadcasted_iota(jnp.int32, (8, W), 1) // DH_C ==
               lax.broadcasted_iota(jnp.int32, (8, W), 0)).astype(F32)
        blk = lax.broadcasted_iota(jnp.int32, (8, LANES), 1)
        blk_f = blk.astype(F32)
        gate = jnp.where(blk < n_past, _dot(seg, g, hi=True), NEG)
        sel = jnp.zeros((8, LANES), F32)
        for r in range(MOBA_TOPK):
            mx = jnp.max(gate, axis=1, keepdims=True)
            first = jnp.min(jnp.where(gate == mx, blk_f, float(LANES)), axis=1, keepdims=True)
            sel = jnp.where(blk == r, first, sel)
            gate = jnp.where(blk_f == first, -jnp.inf, gate)
        sel_ref[0] = sel.astype(jnp.int32)


def _moba_step_attn_body(pt_ref, sel_ref, q_ref, kn_ref, vn_ref, *refs, n_pages):
    nh = HEADS_STEP_C
    o_ref = refs[2 * nh * n_pages]
    scale = DH_C ** -0.5
    for hh in range(nh):
        k_refs = refs[hh * n_pages:(hh + 1) * n_pages]
        v_refs = refs[(nh + hh) * n_pages:(nh + hh + 1) * n_pages]
        q = q_ref[0, hh]
        kn = kn_ref[0, 0, hh]
        vn = vn_ref[0, 0, hh]
        ss = [jnp.sum(k_ref[0, 0, 0] * q, axis=0, keepdims=True) * scale for k_ref in k_refs]
        s_self = jnp.sum(kn * q, axis=0, keepdims=True) * scale
        mx = s_self
        for sj in ss:
            mx = jnp.maximum(mx, jnp.max(sj, axis=1, keepdims=True))
        p_self = jnp.exp(s_self - mx)
        den = p_self
        o = p_self * vn
        for sj, v_ref in zip(ss, v_refs):
            p = jnp.exp(sj - mx)
            den = den + jnp.sum(p, axis=1, keepdims=True)
            o = o + jnp.sum(v_ref[0, 0, 0] * p, axis=1, keepdims=True)
        o_ref[0, hh] = o / den


def moba_step(q, kv_new, pool_t, page_table):
    B, W, _ = q.shape
    page = pool_t.shape[-1]
    n_pt = page_table.shape[1]
    ppb = MOBA_BLOCK // page
    n_past = n_pt // ppb
    assert MOBA_TOPK <= n_past <= LANES and PAGES_PER_STEP % ppb == 0 and n_pt % PAGES_PER_STEP == 0
    pt_flat = page_table.reshape(-1)
    kpage = (1, 1, H_C, DH_C, page)
    sel = pl.pallas_call(
        functools.partial(_moba_step_gate_body, pages_per_block=ppb, n_past=n_past),
        grid_spec=pltpu.PrefetchScalarGridSpec(
            num_scalar_prefetch=1,
            grid=(B, n_pt // PAGES_PER_STEP),
            in_specs=[pl.BlockSpec((1, W, 1), lambda b, s, pt: (b, 0, 0))] +
                     [pl.BlockSpec(kpage, functools.partial(
                         lambda b, s, pt, i: (pt[b * n_pt + s * PAGES_PER_STEP + i], 0, 0, 0, 0), i=i))
                      for i in range(PAGES_PER_STEP)],
            out_specs=pl.BlockSpec((1, 8, LANES), lambda b, s, pt: (b, 0, 0)),
            scratch_shapes=[pltpu.VMEM((W, LANES), F32)]),
        out_shape=jax.ShapeDtypeStruct((B, 8, LANES), jnp.int32),
        compiler_params=_params(("parallel", "arbitrary"), big=True),
        name="moba_step_gate",
    )(pt_flat, q, *([pool_t] * PAGES_PER_STEP))
    sel_flat = sel[:, :H_C, :MOBA_TOPK].reshape(-1)
    n_pages = MOBA_TOPK * ppb

    nh = HEADS_STEP_C

    def page_map(b, hq, pt, sl, hh, r, i, kv):
        h = hq * nh + hh
        blk = sl[(b * H_C + h) * MOBA_TOPK + r]
        return (pt[b * n_pt + blk * ppb + i], kv, h, 0, 0)

    def page_specs(kv):
        return [pl.BlockSpec((1, 1, 1, DH_C, page), functools.partial(page_map, hh=hh, r=r, i=i, kv=kv))
                for hh in range(nh) for r in range(MOBA_TOPK) for i in range(ppb)]

    q4 = q.reshape(B, H_C, DH_C, 1)
    head = pl.BlockSpec((1, nh, DH_C, 1), lambda b, hq, pt, sl: (b, hq, 0, 0))
    new = lambda kv: pl.BlockSpec((1, 1, nh, DH_C, 1), lambda b, hq, pt, sl: (b, kv, hq, 0, 0))
    return pl.pallas_call(
        functools.partial(_moba_step_attn_body, n_pages=n_pages),
        grid_spec=pltpu.PrefetchScalarGridSpec(
            num_scalar_prefetch=2,
            grid=(B, H_C // nh),
            in_specs=[head, new(0), new(1)] + page_specs(0) + page_specs(1),
            out_specs=head),
        out_shape=jax.ShapeDtypeStruct(q4.shape, F32),
        compiler_params=_params(("parallel", "parallel")),
        name="moba_step_attn",
    )(pt_flat, sel_flat, q4, kv_new, kv_new, *([pool_t] * (2 * nh * n_pages)))


def _hgrn_gates(f, wlb_ref, layer):
    w = wlb_ref[...]
    e = jnp.exp(w - jnp.max(w, axis=0, keepdims=True))
    sm = e / jnp.sum(e, axis=0, keepdims=True)
    lb = jnp.sum(sm[1:layer + 1], axis=0, keepdims=True)
    sg = _sigmoid(f)
    return jnp.log(lb + (1.0 - lb) * sg), (1.0 - lb) * (1.0 - sg)


def _hgrn_body(d_ref, wlb_ref, go_ref, o_ref, S_ref, *, layer):
    L, SB = CHUNK_D, SUB_D
    HD = H_D * DK_D

    @pl.when(pl.program_id(0) == 0)
    def _():
        S_ref[...] = jnp.zeros_like(S_ref)

    nb = d_ref.shape[0]
    ri = lax.broadcasted_iota(jnp.int32, (L, L), 0)
    ci = lax.broadcasted_iota(jnp.int32, (L, L), 1)
    tril = (ci <= ri).astype(F32)
    sub_row = lax.broadcasted_iota(jnp.int32, (SB, 1), 0)
    gates = [_hgrn_gates(d_ref[b, :, HD:2 * HD], wlb_ref, layer) for b in range(nb)]
    b_all = [_dot(tril, lf, hi=True) for lf, _ in gates]
    chains = [(b, h) for b in range(nb) for h in range(H_D)]
    nch = range(len(chains))
    blocks = range(SB, L, SB)
    q = [d_ref[b, :, h * DK_D:(h + 1) * DK_D] for b, h in chains]
    v = [d_ref[b, :, 2 * HD + h * DV_D:2 * HD + (h + 1) * DV_D] for b, h in chains]
    bc = [b_all[b][:, h * DK_D:(h + 1) * DK_D] for b, h in chains]
    k = [gates[b][1][:, h * DK_D:(h + 1) * DK_D] for b, h in chains]
    S = [S_ref[b, h] for b, h in chains]
    vb = [x.astype(BF16) for x in v]
    o_inter = [_dot((q[c] * jnp.exp(bc[c])).astype(BF16), S[c].astype(BF16)) for c in nch]
    rnd = lambda x: x.astype(BF16).astype(F32)
    vr = [rnd(x) for x in v]
    a_off = []
    for c in nch:
        for r0 in blocks:
            ref_b = bc[c][r0 - 1:r0]
            qs = rnd(q[c][r0:r0 + SB] * jnp.exp(bc[c][r0:r0 + SB] - ref_b))
            ks = rnd(k[c][:r0] * jnp.exp(ref_b - bc[c][:r0]))
            a_off.append(rnd(_dot_nt(qs, ks)))
    kw = [(k[c] * jnp.exp(bc[c][L - 1:L] - bc[c])).astype(BF16) for c in nch]
    kv = [_dot_tn(kw[c], vb[c]) for c in nch]
    o_off = [_dot(a_off[c * len(blocks) + i], vr[c][:r0]) for c in nch for i, r0 in enumerate(blocks)]
    for c, (b, h) in enumerate(chains):
        b2 = bc[c] * LOG2E
        o_blocks = []
        for i, r0 in enumerate(range(0, L, SB)):
            qI = q[c][r0:r0 + SB]
            bI = b2[r0:r0 + SB]
            oI = o_inter[c][r0:r0 + SB]
            if r0 > 0:
                oI = oI + o_off[c * len(blocks) + i - 1]
            for s in range(SB):
                row = r0 + s
                a = jnp.sum(qI * k[c][row:row + 1] * jnp.exp2(bI - b2[row:row + 1]), axis=1, keepdims=True)
                oI = oI + jnp.where(sub_row >= s, a, 0.0) * v[c][row:row + 1]
            o_blocks.append(oI)
        o = jnp.concatenate(o_blocks, axis=0)
        gd = d_ref[b, :, 3 * HD + h * DV_D:3 * HD + (h + 1) * DV_D]
        o_ref[b, :, h * DV_D:(h + 1) * DV_D] = _rms(o, go_ref[...]) * _silu(gd)
        S_ref[b, h] = _col_of_row(jnp.exp(bc[c][L - 1:L])) * S[c] + kv[c]


def hgrn_prompt(d, w_lb, g_out, layer):
    B, T, W = d.shape
    L = CHUNK_D
    HD = H_D * DK_D
    return pl.pallas_call(
        functools.partial(_hgrn_body, layer=layer),
        grid=(T // L,),
        in_specs=[pl.BlockSpec((B, L, W), lambda c: (0, c, 0)),
                  pl.BlockSpec(w_lb.shape, lambda c: (0, 0)),
                  pl.BlockSpec((1, DV_D), lambda c: (0, 0))],
        out_specs=[pl.BlockSpec((B, L, H_D * DV_D), lambda c: (0, c, 0)),
                   pl.BlockSpec((B, H_D, DK_D, DV_D), lambda c: (0, 0, 0, 0))],
        out_shape=[jax.ShapeDtypeStruct((B, T, H_D * DV_D), F32),
                   jax.ShapeDtypeStruct((B, H_D, DK_D, DV_D), F32)],
        compiler_params=_params(("arbitrary",), big=True),
        name="hgrn_prompt",
    )(d, w_lb, g_out.reshape(1, DV_D))


def _hgrn_step_body(d_ref, wlb_ref, go_ref, S0_ref, o_ref, S_ref, *, layer):
    HD = H_D * DK_D
    lf, kd = _hgrn_gates(d_ref[0, :, HD:2 * HD], wlb_ref, layer)
    eb = jnp.exp(lf)
    for h in range(H_D):
        sl = slice(h * DK_D, (h + 1) * DK_D)
        q = d_ref[0, :, sl]
        v = d_ref[0, :, 2 * HD + h * DV_D:2 * HD + (h + 1) * DV_D]
        gd = d_ref[0, :, 3 * HD + h * DV_D:3 * HD + (h + 1) * DV_D]
        k = kd[:, sl]
        e = eb[:, sl]
        S = S0_ref[0, h]
        a = jnp.sum(q * k, axis=1, keepdims=True)
        o = a * v + _dot(jnp.broadcast_to(q * e, (8, DK_D)), S, hi=True)[0:1]
        o_ref[0, :, h * DV_D:(h + 1) * DV_D] = _rms(o, go_ref[...]) * _silu(gd)
        S_ref[0, h] = _col_of_row(e) * S + _col_of_row(k) * v


def hgrn_step(d, w_lb, g_out, S0, layer):
    B, _, W = d.shape
    ss = pl.BlockSpec((1, H_D, DK_D, DV_D), lambda b: (b, 0, 0, 0))
    return pl.pallas_call(
        functools.partial(_hgrn_step_body, layer=layer),
        grid=(B,),
        in_specs=[pl.BlockSpec((1, 1, W), lambda b: (b, 0, 0)),
                  pl.BlockSpec(w_lb.shape, lambda b: (0, 0)),
                  pl.BlockSpec((1, DV_D), lambda b: (0, 0)), ss],
        out_specs=[pl.BlockSpec((1, 1, H_D * DV_D), lambda b: (b, 0, 0)), ss],
        out_shape=[jax.ShapeDtypeStruct((B, 1, H_D * DV_D), F32), jax.ShapeDtypeStruct(S0.shape, F32)],
        compiler_params=_params(("parallel",)),
        name="hgrn_step",
    )(d, w_lb, g_out.reshape(1, DV_D), S0)


N_A = 4 * H_A * DK_A
N_B = 3 * len(SWA_GROUPS) * GW_B
N_C = H_C * DH_C
N_D = 4 * H_D * DK_D


def _even_weights(w_in):
    gates = jnp.pad(w_in[:, N_A:N_A + 2 * H_A], ((0, 0), (0, LANES - 2 * H_A)))
    return jnp.concatenate([w_in[:, :N_A], w_in[:, N_A + 2 * H_A:], gates], axis=1)


def _trunk(x, ada, prompt, st_even, st_odd, P):
    hi = not prompt
    G, R, D = x.shape
    wdt = F32 if hi else BF16
    tm_lin = 256
    tm_out = 512
    new_even = new_odd = None
    for l in range(2):
        sh1, sc1, g1, sh2, sc2, g2 = [ada[l][..., i * D:(i + 1) * D] for i in range(6)]
        if l == 0:
            w = _even_weights(P['w_in_even']).astype(wdt)
            splits = (N_A, N_B, LANES)
            if hi:
                outs, off = [], 0
                for n in splits:
                    outs += norm_mod_proj(x, P['g_pre_mix'][l], sc1, sh1, w[:, off:off + n], (n,), hi, tm_lin)
                    off += n
                a, qkv, gg = outs
            else:
                a, qkv, gg = norm_mod_proj(x, P['g_pre_mix'][l], sc1, sh1, w, splits, hi, tm_lin)
            bg = jnp.pad(P['b_gate_a'].reshape(1, 2 * H_A), ((0, 0), (0, LANES - 2 * H_A)))
            if prompt:
                ha, C, n, m = mlstm_prompt(a, gg, bg)
                swa = [swa_prompt(qkv, g, dil) for g, (_, dil) in enumerate(SWA_GROUPS)]
                T = R
                bufs = []
                for g, (win, _) in enumerate(SWA_GROUPS):
                    wb = min(win, T)
                    kb = qkv[:, T - wb:, (3 + g) * GW_B:(4 + g) * GW_B]
                    vb = qkv[:, T - wb:, (6 + g) * GW_B:(7 + g) * GW_B]
                    bufs.append(jnp.concatenate([kb, vb], axis=-1).reshape(G, wb, 2, H_B, DH_B))
                new_even = (C, n, m[:, 0, :H_A], bufs[0], bufs[1], bufs[2])
            else:
                B = R
                C0, n0, m0 = st_even[:3]
                m0p = jnp.pad(m0, ((0, 0), (0, LANES - H_A))).reshape(B, 1, LANES)
                tok = lambda t: t.reshape(B, 1, t.shape[-1])
                ha, C, n, m = mlstm_step(tok(a), tok(gg), bg, C0, n0, m0p)
                swa, bufs = [], []
                ng = len(SWA_GROUPS)
                qkv5 = qkv.reshape(B, 3, ng, H_B, DH_B, 1)
                for g, (_, dil) in enumerate(SWA_GROUPS):
                    buf_t = jnp.transpose(st_even[3 + g], (0, 2, 3, 4, 1))
                    o, lse, nb = swa_step(qkv5[:, 0, g], qkv5[:, 1:, g], buf_t, dil)
                    o, lse = o.reshape(1, B, GW_B), lse.reshape(1, B, GW_B)
                    swa.append(([o[..., p * LANES:(p + 1) * LANES] for p in range(NPAIR_B)],
                                [lse[..., p * LANES:(p + 1) * LANES] for p in range(NPAIR_B)]))
                    bufs.append(jnp.transpose(nb, (0, 4, 1, 2, 3)))
                ha = ha.reshape(1, B, -1)
                new_even = (C, n, m[:, 0, :H_A], bufs[0], bufs[1], bufs[2])
            swa_os = [swa[g][0][p] for p in range(NPAIR_B) for g in range(len(SWA_GROUPS))]
            swa_ls = [swa[g][1][p] for p in range(NPAIR_B) for g in range(len(SWA_GROUPS))]
            x = mix_out(x, g1, P['g_post_mix'][l], [ha], (swa_os, swa_ls), P['w_out_even'].astype(wdt), hi, tm_out)
        else:
            w = P['w_in_odd'].astype(wdt)
            if prompt:
                d, q_t, kvt = norm_mod_proj(x, P['g_pre_mix'][l], sc1, sh1, w[:, 3 * N_C:], (N_D,), hi, tm_lin,
                                            wt=w[:, :3 * N_C].T, t_splits=(N_C, 2 * N_C))
                hc = moba_prompt(q_t, kvt)
                od, S = hgrn_prompt(d, P['w_lb'], P['g_out_d'], l)
                new_odd = (jnp.transpose(kvt.reshape(G, 2, H_C, DH_C, R), (0, 4, 1, 2, 3)), S)
            else:
                B = R
                outs, off = [], 0
                for n in (N_C, 2 * N_C, N_D):
                    outs += norm_mod_proj(x, P['g_pre_mix'][l], sc1, sh1, w[:, off:off + n], (n,), hi, tm_lin)
                    off += n
                qc, kvc, d = outs
                pool, page_table, S0 = st_odd
                hc = moba_step(qc.reshape(B, N_C, 1), kvc.reshape(B, 2, H_C, DH_C, 1),
                               jnp.transpose(pool, (0, 2, 3, 4, 1)), page_table)
                od, S = hgrn_step(d.reshape(B, 1, N_D), P['w_lb'], P['g_out_d'], S0, l)
                hc = hc.reshape(1, B, -1)
                od = od.reshape(1, B, -1)
                new_odd = (kvc.reshape(B, 1, 2, H_C, DH_C), S)
            x = mix_out(x, g1, P['g_post_mix'][l], [hc, od], None, P['w_out_odd'].astype(wdt), hi, tm_out)
        x = ffn(x, P['g_pre_ffn'][l], sc2, sh2, g2, P['g_post_ffn'][l],
                P['w_ffn_gate'][l].astype(wdt), P['w_ffn_up'][l].astype(wdt), P['w_ffn_down'][l].astype(wdt),
                hi, tm=512, tf=256 if hi else 1408)
    return x, new_even, new_odd


def kernel(x_prompt, x_sample, state_mlstm_C, state_mlstm_n, state_mlstm_m, cache_swa_w128, cache_swa_w512, cache_swa_w2048, cache_moba_kv, state_hgrn_S, page_table, c_prompt, c_sample, w_ada, b_ada, g_pre_mix, g_post_mix, g_pre_ffn, g_post_ffn, w_in_even, b_gate_a, w_out_even, w_in_odd, w_lb, g_out_d, w_out_odd, w_ffn_gate, w_ffn_up, w_ffn_down):
    P = {'g_pre_mix': g_pre_mix, 'g_post_mix': g_post_mix, 'g_pre_ffn': g_pre_ffn, 'g_post_ffn': g_post_ffn,
         'w_in_even': w_in_even, 'b_gate_a': b_gate_a, 'w_out_even': w_out_even, 'w_in_odd': w_in_odd,
         'w_lb': w_lb, 'g_out_d': g_out_d, 'w_out_odd': w_out_odd, 'w_ffn_gate': w_ffn_gate,
         'w_ffn_up': w_ffn_up, 'w_ffn_down': w_ffn_down}
    Bp = x_prompt.shape[0]
    Bs = x_sample.shape[0]
    D = x_prompt.shape[-1]
    c_all = jnp.concatenate([c_prompt, c_sample], axis=0)
    c_all = jnp.pad(c_all, ((0, -(Bp + Bs) % 8), (0, 0)))
    ada = ada_all(c_all, w_ada, b_ada)[:, :Bp + Bs]
    ada_p = ada[:, :Bp].reshape(ada.shape[0], Bp, 1, 6 * D)
    ada_s = ada[:, Bp:].reshape(ada.shape[0], 1, Bs, 6 * D)
    y_p, ev_p, od_p = _trunk(x_prompt, ada_p, True, None, None, P)
    y_s, ev_s, od_s = _trunk(
        x_sample.reshape(1, Bs, D), ada_s, False,
        (state_mlstm_C, state_mlstm_n, state_mlstm_m, cache_swa_w128, cache_swa_w512, cache_swa_w2048),
        (cache_moba_kv, page_table, state_hgrn_S), P)
    return ((y_p, y_s.reshape(x_sample.shape)) + tuple(ev_p[:3]) + tuple(ev_s[:3]) + tuple(ev_p[3:])
            + tuple(ev_s[3:]) + (od_p[0], od_s[0], od_p[1], od_s[1]))
```

```python
import functools

import jax
import jax.numpy as jnp
from jax import lax
from jax.experimental import pallas as pl
from jax.experimental.pallas import tpu as pltpu

F32 = jnp.float32
BF16 = jnp.bfloat16
HI = lax.Precision.HIGHEST

EPS = 1e-6
NEG = -1e30

H_A, DK_A, DV_A, CHUNK_A = 4, 128, 128, 128
SWA_GROUPS = ((128, 1), (512, 4), (2048, 16))
H_B, DH_B, SWA_BAND = 4, 64, 128
GW_B = H_B * DH_B
H_C, DH_C, MOBA_BLOCK, MOBA_TOPK = 8, 64, 256, 3
H_D, DK_D, DV_D, CHUNK_D, SUB_D = 4, 128, 128, 64, 8

LANES = 128
VMEM_LIMIT = 56 << 20


def _params(sem, big=False):
    return pltpu.CompilerParams(dimension_semantics=sem,
                                vmem_limit_bytes=VMEM_LIMIT if big else None)


def _dot(a, b, hi=False):
    return jnp.dot(a, b, preferred_element_type=F32, precision=HI if hi else None)


def _dot_nt(a, b, hi=False):
    return lax.dot_general(a, b, (((1,), (1,)), ((), ())), preferred_element_type=F32,
                           precision=HI if hi else None)


def _dot_tn(a, b, hi=False):
    return lax.dot_general(a, b, (((0,), (0,)), ((), ())), preferred_element_type=F32,
                           precision=HI if hi else None)


def _rms(x, g):
    return x * lax.rsqrt(jnp.mean(x * x, axis=-1, keepdims=True) + EPS) * g


def _sigmoid(x):
    return 1.0 / (1.0 + jnp.exp(-x))


def _silu(x):
    return x * _sigmoid(x)


def _log_sigmoid(x):
    return jnp.minimum(x, 0.0) - jnp.log(1.0 + jnp.exp(-jnp.abs(x)))


def _col_of_row(row):
    n = row.shape[-1]
    eye = lax.broadcasted_iota(jnp.int32, (n, n), 0) == lax.broadcasted_iota(jnp.int32, (n, n), 1)
    return jnp.sum(jnp.where(eye, row, 0.0), axis=1, keepdims=True)


def _ada_body(c_ref, w_ref, b_ref, o_ref):
    o_ref[0] = _dot(_silu(c_ref[...]), w_ref[0], hi=True) + b_ref[0]


def ada_all(c, w_ada, b_ada, tn=768):
    depth, d, n = w_ada.shape
    r = c.shape[0]
    return pl.pallas_call(
        _ada_body,
        grid=(depth, n // tn),
        in_specs=[pl.BlockSpec((r, d), lambda l, j: (0, 0)),
                  pl.BlockSpec((1, d, tn), lambda l, j: (l, 0, j)),
                  pl.BlockSpec((1, 1, tn), lambda l, j: (l, 0, j))],
        out_specs=pl.BlockSpec((1, r, tn), lambda l, j: (l, 0, j)),
        out_shape=jax.ShapeDtypeStruct((depth, r, n), F32),
        compiler_params=_params(("parallel", "parallel")),
        name="ada",
    )(c, w_ada, b_ada.reshape(depth, 1, n))


def _proj_body(x_ref, g_ref, sc_ref, sh_ref, w_ref, *refs, splits, hi, cw, t_splits):
    h = _rms(x_ref[0], g_ref[...]) * (1.0 + sc_ref[0]) + sh_ref[0]
    hc = h if hi else h.astype(BF16)
    o_refs = refs[1:] if t_splits else refs
    off = 0
    for o_ref, n in zip(o_refs, splits):
        for c0 in range(0, n, cw):
            c1 = min(c0 + cw, n)
            o_ref[0, :, c0:c1] = _dot(hc, w_ref[:, off + c0:off + c1], hi)
        off += n
    if t_splits:
        wt_ref, off = refs[0], 0
        for ot_ref, n in zip(refs[1 + len(splits):], t_splits):
            for r0 in range(0, n, cw):
                ot_ref[0, r0:r0 + cw, :] = _dot_nt(wt_ref[off + r0:off + r0 + cw, :], hc, hi)
            off += n


def _mod_spec(mod, tm):
    if mod.shape[1] == 1:
        return pl.BlockSpec((1, 1, mod.shape[2]), lambda gi, ri: (gi, 0, 0))
    return pl.BlockSpec((1, tm, mod.shape[2]), lambda gi, ri: (gi, ri, 0))


def norm_mod_proj(x, g, sc, sh, w, splits, hi, tm, wt=None, t_splits=()):
    G, R, D = x.shape
    tm = min(tm, R)
    n_all = sum(splits)
    n_t = sum(t_splits)
    ops = [x, g.reshape(1, D), sc, sh, w]
    in_specs = [pl.BlockSpec((1, tm, D), lambda gi, ri: (gi, ri, 0)),
                pl.BlockSpec((1, D), lambda gi, ri: (0, 0)),
                _mod_spec(sc, tm), _mod_spec(sh, tm),
                pl.BlockSpec((D, n_all), lambda gi, ri: (0, 0))]
    out_specs = [pl.BlockSpec((1, tm, n), lambda gi, ri: (gi, ri, 0)) for n in splits]
    out_shape = [jax.ShapeDtypeStruct((G, R, n), F32) for n in splits]
    if n_t:
        ops.append(wt)
        in_specs.append(pl.BlockSpec((n_t, D), lambda gi, ri: (0, 0)))
        out_specs += [pl.BlockSpec((1, n, tm), lambda gi, ri: (gi, 0, ri)) for n in t_splits]
        out_shape += [jax.ShapeDtypeStruct((G, n, R), F32) for n in t_splits]
    return pl.pallas_call(
        functools.partial(_proj_body, splits=tuple(splits), hi=hi, cw=512, t_splits=tuple(t_splits)),
        grid=(G, R // tm),
        in_specs=in_specs,
        out_specs=out_specs,
        out_shape=out_shape,
        compiler_params=_params(("parallel", "parallel"), big=True),
        name="proj",
    )(*ops)


def _mixout_body(*refs, n_parts, swa, hi):
    x_ref, g1_ref, gp_ref = refs[:3]
    part_refs = refs[3:3 + n_parts]
    pos = 3 + n_parts
    ng = len(SWA_GROUPS)
    o_refs = refs[pos:pos + swa * ng]
    l_refs = refs[pos + swa * ng:pos + 2 * swa * ng]
    pos += 2 * swa * ng
    w_ref, out_ref = refs[pos], refs[pos + 1]

    def cast(a):
        return a if hi else a.astype(BF16)

    acc = None
    off = 0
    for p_ref in part_refs:
        a = p_ref[0]
        k = a.shape[-1]
        t = _dot(cast(a), w_ref[off:off + k, :], hi)
        acc = t if acc is None else acc + t
        off += k
    for c in range(swa):
        ls = [l_ref[0] for l_ref in l_refs[c * ng:(c + 1) * ng]]
        os_ = [o_ref[0] for o_ref in o_refs[c * ng:(c + 1) * ng]]
        mx = functools.reduce(jnp.maximum, ls)
        es = [jnp.exp(l - mx) for l in ls]
        hb = sum(e * o for e, o in zip(es, os_)) / sum(es)
        acc = acc + _dot(cast(hb), w_ref[off:off + hb.shape[-1], :], hi)
        off += hb.shape[-1]
    out_ref[0] = x_ref[0] + g1_ref[0] * _rms(acc, gp_ref[...])


def mix_out(x, g1, gpost, parts, swa_parts, w, hi, tm):
    G, R, D = x.shape
    tm = min(tm, R)
    row = lambda a: pl.BlockSpec((1, tm, a.shape[2]), lambda gi, ri: (gi, ri, 0))
    ops = [x, g1, gpost.reshape(1, D)] + list(parts)
    specs = [row(x), _mod_spec(g1, tm), pl.BlockSpec((1, D), lambda gi, ri: (0, 0))] + [row(p) for p in parts]
    swa = 0
    if swa_parts is not None:
        swa = len(swa_parts[0]) // len(SWA_GROUPS)
        ops += list(swa_parts[0]) + list(swa_parts[1])
        specs += [row(a) for a in ops[-2 * len(swa_parts[0]):]]
    ops.append(w)
    specs.append(pl.BlockSpec(w.shape, lambda gi, ri: (0, 0)))
    return pl.pallas_call(
        functools.partial(_mixout_body, n_parts=len(parts), swa=swa, hi=hi),
        grid=(G, R // tm),
        in_specs=specs,
        out_specs=row(x),
        out_shape=jax.ShapeDtypeStruct(x.shape, F32),
        compiler_params=_params(("parallel", "parallel"), big=True),
        name="mix_out",
    )(*ops)


def _ffn_body(x_ref, gpre_ref, sc_ref, sh_ref, g2_ref, gpost_ref, wg_ref, wu_ref, wd_ref, out_ref,
              h_scr, acc_scr, *, hi):
    k = pl.program_id(2)

    @pl.when(k == 0)
    def _():
        h = _rms(x_ref[0], gpre_ref[...]) * (1.0 + sc_ref[0]) + sh_ref[0]
        h_scr[...] = h.astype(h_scr.dtype)
        acc_scr[...] = jnp.zeros_like(acc_scr)

    h = h_scr[...]
    a = _silu(_dot(h, wg_ref[...], hi)) * _dot(h, wu_ref[...], hi)
    acc_scr[...] += _dot(a.astype(h.dtype), wd_ref[...], hi)

    @pl.when(k == pl.num_programs(2) - 1)
    def _():
        out_ref[0] = x_ref[0] + g2_ref[0] * _rms(acc_scr[...], gpost_ref[...])


def ffn(x, gpre, sc, sh, g2, gpost, wg, wu, wd, hi, tm, tf):
    G, R, D = x.shape
    FF = wg.shape[1]
    tm = min(tm, R)
    mspec = lambda m: (pl.BlockSpec((1, 1, D), lambda gi, ri, k: (gi, 0, 0)) if m.shape[1] == 1
                       else pl.BlockSpec((1, tm, D), lambda gi, ri, k: (gi, ri, 0)))
    vec = pl.BlockSpec((1, D), lambda gi, ri, k: (0, 0))
    return pl.pallas_call(
        functools.partial(_ffn_body, hi=hi),
        grid=(G, R // tm, FF // tf),
        in_specs=[pl.BlockSpec((1, tm, D), lambda gi, ri, k: (gi, ri, 0)), vec, mspec(sc), mspec(sh),
                  mspec(g2), vec,
                  pl.BlockSpec((D, tf), lambda gi, ri, k: (0, k)),
                  pl.BlockSpec((D, tf), lambda gi, ri, k: (0, k)),
                  pl.BlockSpec((tf, D), lambda gi, ri, k: (k, 0))],
        out_specs=pl.BlockSpec((1, tm, D), lambda gi, ri, k: (gi, ri, 0)),
        out_shape=jax.ShapeDtypeStruct(x.shape, F32),
        scratch_shapes=[pltpu.VMEM((tm, D), F32 if hi else BF16), pltpu.VMEM((tm, D), F32)],
        compiler_params=_params(("parallel", "parallel", "arbitrary"), big=True),
        name="ffn",
    )(x, gpre.reshape(1, D), sc, sh, g2, gpost.reshape(1, D), wg, wu, wd)


def _mlstm_body(a_ref, g_ref, bg_ref, h_ref, C_ref, n_ref, m_ref):
    L = CHUNK_A
    HD = H_A * DK_A

    @pl.when(pl.program_id(0) == 0)
    def _():
        C_ref[...] = jnp.zeros_like(C_ref)
        n_ref[...] = jnp.zeros_like(n_ref)
        m_ref[...] = jnp.zeros_like(m_ref)

    nb = a_ref.shape[0]
    lane = lax.broadcasted_iota(jnp.int32, (L, LANES), 1)
    ri = lax.broadcasted_iota(jnp.int32, (L, L), 0)
    ci = lax.broadcasted_iota(jnp.int32, (L, L), 1)
    causal = ci <= ri
    tril = causal.astype(F32)
    m_lane = lax.broadcasted_iota(jnp.int32, (1, LANES), 1)
    gates_t, bcum, bcum_t, m_all = [], [], [], []
    for b in range(nb):
        pre = g_ref[b] + bg_ref[...]
        gates = jnp.where(lane >= H_A, _log_sigmoid(pre), pre)
        bcum.append(_dot(tril, gates, hi=True))
        gates_t.append(gates.T)
        bcum_t.append(bcum[b].T)
        m_all.append(m_ref[b])
    m_out = list(m_all)
    chains = [(b, h) for b in range(nb) for h in range(H_A)]
    q = [a_ref[b, :, h * DK_A:(h + 1) * DK_A] for b, h in chains]
    k = [a_ref[b, :, HD + h * DK_A:HD + (h + 1) * DK_A] * (DK_A ** -0.5) for b, h in chains]
    vb = [a_ref[b, :, 2 * HD + h * DV_A:2 * HD + (h + 1) * DV_A].astype(BF16) for b, h in chains]
    qb = [x.astype(BF16) for x in q]
    C = [C_ref[b, h] for b, h in chains]
    n_row = [n_ref[b, h:h + 1, :] for b, h in chains]
    qk = [_dot_nt(qb[c], k[c].astype(BF16)) for c in range(len(chains))]
    qC = [_dot(qb[c], C[c].astype(BF16)) for c in range(len(chains))]
    s, winter, mt, kw, ws, wc = [], [], [], [], [], []
    for c, (b, h) in enumerate(chains):
        ig_row = gates_t[b][h:h + 1, :]
        b_col = bcum[b][:, H_A + h:H_A + h + 1]
        b_row = bcum_t[b][H_A + h:H_A + h + 1, :]
        m = m_all[b][:, h:h + 1]
        dmat = jnp.where(causal, b_col - b_row + ig_row, NEG)
        inter = b_col + m
        mt.append(jnp.maximum(inter, jnp.max(dmat, axis=1, keepdims=True)))
        winter.append(jnp.exp(inter - mt[c]))
        s.append(qk[c] * jnp.exp(dmat - mt[c]))
        bl = b_row[:, L - 1:L]
        gl = bl - b_row + ig_row
        m_new = jnp.maximum(bl + m, jnp.max(gl, axis=1, keepdims=True))
        ws.append(jnp.exp(gl - m_new))
        wc.append(jnp.exp(bl + m - m_new))
        kw.append((k[c].T * ws[c]).astype(BF16))
        m_out[b] = jnp.where(m_lane == h, m_new, m_out[b])
    sv = [_dot(s[c].astype(BF16), vb[c]) for c in range(len(chains))]
    kv = [_dot(kw[c], vb[c]) for c in range(len(chains))]
    kn = [_dot(jnp.broadcast_to(ws[c], (8, L)), k[c], hi=True)[0:1] for c in range(len(chains))]
    for c, (b, h) in enumerate(chains):
        og = a_ref[b, :, 3 * HD + h * DV_A:3 * HD + (h + 1) * DV_A]
        num = sv[c] + winter[c] * qC[c]
        den = jnp.sum(s[c], axis=1, keepdims=True) + winter[c] * jnp.sum(q[c] * n_row[c], axis=1, keepdims=True)
        hh = _sigmoid(og) * (num / jnp.maximum(jnp.abs(den), jnp.exp(-mt[c])))
        h_ref[b, :, h * DV_A:(h + 1) * DV_A] = hh.astype(h_ref.dtype)
        C_ref[b, h] = wc[c] * C[c] + kv[c]
        n_ref[b, h:h + 1, :] = wc[c] * n_row[c] + kn[c]
    for b in range(nb):
        m_ref[b] = m_out[b]


def mlstm_prompt(a, g, bg):
    B, T, _ = a.shape
    L = CHUNK_A
    return pl.pallas_call(
        _mlstm_body,
        grid=(T // L,),
        in_specs=[pl.BlockSpec((B, L, a.shape[2]), lambda c: (0, c, 0)),
                  pl.BlockSpec((B, L, LANES), lambda c: (0, c, 0)),
                  pl.BlockSpec((1, LANES), lambda c: (0, 0))],
        out_specs=[pl.BlockSpec((B, L, H_A * DV_A), lambda c: (0, c, 0)),
                   pl.BlockSpec((B, H_A, DK_A, DV_A), lambda c: (0, 0, 0, 0)),
                   pl.BlockSpec((B, H_A, DK_A), lambda c: (0, 0, 0)),
                   pl.BlockSpec((B, 1, LANES), lambda c: (0, 0, 0))],
        out_shape=[jax.ShapeDtypeStruct((B, T, H_A * DV_A), BF16),
                   jax.ShapeDtypeStruct((B, H_A, DK_A, DV_A), F32),
                   jax.ShapeDtypeStruct((B, H_A, DK_A), F32),
                   jax.ShapeDtypeStruct((B, 1, LANES), F32)],
        compiler_params=_params(("arbitrary",), big=True),
        name="mlstm_prompt",
    )(a, g, bg)


def _mlstm_step_body(a_ref, g_ref, bg_ref, C0_ref, n0_ref, m0_ref, h_ref, C_ref, n_ref, m_ref):
    HD = H_A * DK_A
    pre = g_ref[0] + bg_ref[...]
    lane = lax.broadcasted_iota(jnp.int32, (1, LANES), 1)
    gates = jnp.where(lane >= H_A, _log_sigmoid(pre), pre)
    m_all = m0_ref[0]
    m_out = m_all
    for h in range(H_A):
        q = a_ref[0, :, h * DK_A:(h + 1) * DK_A]
        k = a_ref[0, :, HD + h * DK_A:HD + (h + 1) * DK_A] * (DK_A ** -0.5)
        v = a_ref[0, :, 2 * HD + h * DV_A:2 * HD + (h + 1) * DV_A]
        og = a_ref[0, :, 3 * HD + h * DV_A:3 * HD + (h + 1) * DV_A]
        ig = gates[:, h:h + 1]
        lf = gates[:, H_A + h:H_A + h + 1]
        m = m_all[:, h:h + 1]
        C = C0_ref[0, h]
        n_row = n0_ref[0, h:h + 1, :]
        inter = lf + m
        mt = jnp.maximum(inter, ig)
        s = jnp.sum(q * k, axis=1, keepdims=True) * jnp.exp(ig - mt)
        winter = jnp.exp(inter - mt)
        qC = _dot(jnp.broadcast_to(q, (8, DK_A)), C, hi=True)[0:1]
        num = s * v + winter * qC
        den = s + winter * jnp.sum(q * n_row, axis=1, keepdims=True)
        hh = num / jnp.maximum(jnp.abs(den), jnp.exp(-mt))
        h_ref[0, :, h * DV_A:(h + 1) * DV_A] = _sigmoid(og) * hh
        m_new = jnp.maximum(inter, ig)
        ws = jnp.exp(ig - m_new)
        wc = jnp.exp(inter - m_new)
        C_ref[0, h] = wc * C + (ws * _col_of_row(k)) * v
        n_ref[0, h:h + 1, :] = wc * n_row + ws * k
        m_out = jnp.where(lane == h, m_new, m_out)
    m_ref[0] = m_out


def mlstm_step(a, g, bg, C0, n0, m0):
    B = a.shape[0]
    r3 = lambda w: pl.BlockSpec((1, 1, w), lambda b: (b, 0, 0))
    cs = pl.BlockSpec((1, H_A, DK_A, DV_A), lambda b: (b, 0, 0, 0))
    ns = pl.BlockSpec((1, H_A, DK_A), lambda b: (b, 0, 0))
    return pl.pallas_call(
        _mlstm_step_body,
        grid=(B,),
        in_specs=[r3(a.shape[2]), r3(LANES), pl.BlockSpec((1, LANES), lambda b: (0, 0)), cs, ns, r3(LANES)],
        out_specs=[r3(H_A * DV_A), cs, ns, r3(LANES)],
        out_shape=[jax.ShapeDtypeStruct((B, 1, H_A * DV_A), F32),
                   jax.ShapeDtypeStruct(C0.shape, F32), jax.ShapeDtypeStruct(n0.shape, F32),
                   jax.ShapeDtypeStruct((B, 1, LANES), F32)],
        compiler_params=_params(("parallel",)),
        name="mlstm_step",
    )(a, g, bg, C0, n0, m0)


NPAIR_B = GW_B // LANES


def _swa_body(*refs, dil):
    Q = SWA_BAND
    q_refs, kc_refs, kp_refs, vc_refs, vp_refs = [refs[i * NPAIR_B:(i + 1) * NPAIR_B] for i in range(5)]
    o_refs = refs[5 * NPAIR_B:6 * NPAIR_B]
    l_refs = refs[6 * NPAIR_B:7 * NPAIR_B]
    qi = lax.broadcasted_iota(jnp.int32, (Q, Q), 0)
    kj = lax.broadcasted_iota(jnp.int32, (Q, Q), 1)
    mask_c = kj <= qi
    mask_p = jnp.logical_and(kj >= qi, pl.program_id(1) > 0)
    lane = lax.broadcasted_iota(jnp.int32, (Q, LANES), 1)
    first = lane < DH_B
    chains = [(p, hh) for p in range(NPAIR_B) for hh in range(2)]
    for r in range(dil):
        rows = (pl.ds(0, 1), pl.ds(r, Q, stride=dil) if dil > 1 else pl.ds(0, Q), slice(None))
        ld = lambda ref: ref[rows][0]
        q = [ld(ref) * (DH_B ** -0.5) for ref in q_refs]
        kc = [ld(ref).astype(BF16) for ref in kc_refs]
        kp = [ld(ref).astype(BF16) for ref in kp_refs]
        vc = [ld(ref).astype(BF16) for ref in vc_refs]
        vp = [ld(ref).astype(BF16) for ref in vp_refs]
        qm = [jnp.where(first if hh == 0 else ~first, q[p], 0.0).astype(BF16) for p, hh in chains]
        sc = [_dot_nt(qm[c], kc[p]) for c, (p, hh) in enumerate(chains)]
        sp = [_dot_nt(qm[c], kp[p]) for c, (p, hh) in enumerate(chains)]
        pcs, pps, dens, lses = [], [], [], []
        for c in range(len(chains)):
            s_c = jnp.where(mask_c, sc[c], NEG)
            s_p = jnp.where(mask_p, sp[c], NEG)
            mx = jnp.maximum(jnp.max(s_c, axis=1, keepdims=True), jnp.max(s_p, axis=1, keepdims=True))
            pc = jnp.exp(s_c - mx)
            pp = jnp.exp(s_p - mx)
            den = jnp.sum(pc, axis=1, keepdims=True) + jnp.sum(pp, axis=1, keepdims=True)
            pcs.append(pc.astype(BF16))
            pps.append(pp.astype(BF16))
            dens.append(den)
            lses.append(mx + jnp.log(den))
        os_ = [(_dot(pcs[c], vc[p]) + _dot(pps[c], vp[p])) / dens[c] for c, (p, hh) in enumerate(chains)]
        for p in range(NPAIR_B):
            o_refs[p][rows] = jnp.where(first, os_[2 * p], os_[2 * p + 1])[None]
            l_refs[p][rows] = jnp.where(first, lses[2 * p], lses[2 * p + 1])[None]


def swa_prompt(qkv, g, dil):
    B, T, W = qkv.shape
    ng = W // GW_B // 3
    unit = SWA_BAND * dil
    col = lambda part, p: (part * ng + g) * NPAIR_B + p
    cur = lambda part: [pl.BlockSpec((1, unit, LANES), functools.partial(lambda b, n, c: (b, n, c), c=col(part, p)))
                        for p in range(NPAIR_B)]
    prev = lambda part: [pl.BlockSpec((1, unit, LANES),
                                      functools.partial(lambda b, n, c: (b, jnp.maximum(n - 1, 0), c), c=col(part, p)))
                         for p in range(NPAIR_B)]
    ospec = pl.BlockSpec((1, unit, LANES), lambda b, n: (b, n, 0))
    res = pl.pallas_call(
        functools.partial(_swa_body, dil=dil),
        grid=(B, T // unit),
        in_specs=cur(0) + cur(1) + prev(1) + cur(2) + prev(2),
        out_specs=[ospec] * (2 * NPAIR_B),
        out_shape=[jax.ShapeDtypeStruct((B, T, LANES), F32)] * (2 * NPAIR_B),
        compiler_params=_params(("parallel", "parallel"), big=True),
        name="swa_prompt_d%d" % dil,
    )(*([qkv] * (5 * NPAIR_B)))
    return res[:NPAIR_B], res[NPAIR_B:]


def _swa_step_body(q_ref, kvn_ref, buf_ref, o_ref, l_ref, nb_ref, *, dil):
    wb = buf_ref.shape[-1]
    lane = lax.broadcasted_iota(jnp.int32, (1, wb), 1)
    read = (lane % dil) == 0
    last = lane == wb - 1
    for h in range(H_B):
        q = q_ref[0, h]
        kn = kvn_ref[0, 0, h]
        vn = kvn_ref[0, 1, h]
        kt = buf_ref[0, 0, h]
        vt = buf_ref[0, 1, h]
        s = jnp.where(read, jnp.sum(kt * q, axis=0, keepdims=True) * (DH_B ** -0.5), NEG)
        s_self = jnp.sum(kn * q, axis=0, keepdims=True) * (DH_B ** -0.5)
        mx = jnp.maximum(jnp.max(s, axis=1, keepdims=True), s_self)
        p = jnp.exp(s - mx)
        p_self = jnp.exp(s_self - mx)
        den = jnp.sum(p, axis=1, keepdims=True) + p_self
        o_ref[0, h] = (jnp.sum(vt * p, axis=1, keepdims=True) + p_self * vn) / den
        l_ref[0, h] = jnp.broadcast_to(mx + jnp.log(den), (DH_B, 1))
        nb_ref[0, 0, h] = jnp.where(last, kn, pltpu.roll(kt, wb - 1, axis=1))
        nb_ref[0, 1, h] = jnp.where(last, vn, pltpu.roll(vt, wb - 1, axis=1))


def swa_step(q, kvn, buf_t, dil):
    B = q.shape[0]
    qs = pl.BlockSpec((1,) + q.shape[1:], lambda b: (b, 0, 0, 0))
    ks = pl.BlockSpec((1,) + kvn.shape[1:], lambda b: (b, 0, 0, 0, 0))
    bs = pl.BlockSpec((1,) + buf_t.shape[1:], lambda b: (b, 0, 0, 0, 0))
    return pl.pallas_call(
        functools.partial(_swa_step_body, dil=dil),
        grid=(B,),
        in_specs=[qs, ks, bs],
        out_specs=[qs, qs, bs],
        out_shape=[jax.ShapeDtypeStruct(q.shape, F32)] * 2 + [jax.ShapeDtypeStruct(buf_t.shape, F32)],
        compiler_params=_params(("parallel",), big=True),
        name="swa_step_d%d" % dil,
    )(q, kvn, buf_t)


AUG = 2 * DH_C


VROWS = DH_C + 16
TQ_C = 128
HEADS_C = 4
LOG2E = 1.4426950408889634


def _moba_kprep_body(kt_ref, vt_ref, km_ref, ka_ref, vta_ref, *, nblk):
    n = pl.program_id(1)
    k = kt_ref[0].T
    km_ref[0, pl.ds(n, 1), :] = jnp.mean(k, axis=0, keepdims=True)
    lane = lax.broadcasted_iota(jnp.int32, (MOBA_BLOCK, DH_C), 1)
    onehot = jnp.where(lane == n, 1.0, 0.0).astype(BF16)
    pieces = []
    for h in range(H_C):
        pieces += [k[:, h * DH_C:(h + 1) * DH_C].astype(BF16), onehot]
    ka_ref[0] = jnp.concatenate(pieces, axis=1)
    vt = vt_ref[0]
    ones = jnp.where(lax.broadcasted_iota(jnp.int32, (VROWS - DH_C, MOBA_BLOCK), 0) == 0, 1.0, 0.0).astype(BF16)
    pieces = []
    for h in range(H_C):
        pieces += [vt[h * DH_C:(h + 1) * DH_C].astype(BF16), ones]
    vta_ref[0] = jnp.concatenate(pieces, axis=0)


def _moba_gate_body(qt_ref, km_ref, qa_ref, *, nblk):
    own = pl.program_id(1)
    blk = lax.broadcasted_iota(jnp.int32, (nblk, MOBA_BLOCK), 0)
    blk_f = blk.astype(F32)
    past = blk < own
    gs = [_dot(km_ref[0, :, h * DH_C:(h + 1) * DH_C], qt_ref[0, h * DH_C:(h + 1) * DH_C, :], hi=True)
          for h in range(H_C)]
    for h in range(H_C):
        g = jnp.where(past, gs[h], NEG)
        sel = blk == own
        for _ in range(MOBA_TOPK):
            mx = jnp.max(g, axis=0, keepdims=True)
            first = jnp.min(jnp.where(g == mx, blk_f, float(nblk)), axis=0, keepdims=True)
            pick = blk_f == first
            sel = jnp.logical_or(sel, jnp.logical_and(pick, past))
            g = jnp.where(pick, -jnp.inf, g)
        bias = jnp.where(sel, 0.0, NEG)
        aug = jnp.concatenate([qt_ref[0, h * DH_C:(h + 1) * DH_C, :] * (DH_C ** -0.5 * LOG2E), bias,
                               jnp.zeros((AUG - DH_C - nblk, MOBA_BLOCK), F32)], axis=0)
        qa_ref[0, h * AUG:(h + 1) * AUG, :] = aug.astype(BF16)


def _moba_attn_body(qt_ref, ka_ref, vta_ref, o_ref, sa_scr, sb_scr):
    own = pl.program_id(2)
    TK = MOBA_BLOCK
    nsub = TK // TQ_C
    chains = [(hh, qs) for hh in range(HEADS_C) for qs in range(nsub)]
    qts = [qt_ref[0, hh * AUG:(hh + 1) * AUG, qs * TQ_C:(qs + 1) * TQ_C] for hh, qs in chains]

    def block(n, hh):
        r0 = pl.multiple_of(n * TK, TK)
        return (ka_ref[0, pl.ds(r0, TK), hh * AUG:(hh + 1) * AUG],
                vta_ref[0, hh * VROWS:(hh + 1) * VROWS, pl.ds(r0, TK)])

    key = lax.broadcasted_iota(jnp.int32, (TK, TQ_C), 0)
    qry = lax.broadcasted_iota(jnp.int32, (TK, TQ_C), 1)

    def issue_scores(n, s_scr):
        ks = [block(n, hh)[0] for hh in range(HEADS_C)]
        for c, (hh, qs) in enumerate(chains):
            s_scr[c] = _dot(ks[hh], qts[c])

    def absorb(n, s_scr, ms_prev, accs, diagonal):
        vs = [block(n, hh)[1] for hh in range(HEADS_C)]
        ms, ps, alphas = [], [], []
        for c, (hh, qs) in enumerate(chains):
            s = s_scr[c]
            if diagonal:
                s = jnp.where(key <= qry + qs * TQ_C, s, NEG)
            mx = jnp.max(s, axis=0, keepdims=True)
            m_new = mx if ms_prev is None else jnp.maximum(ms_prev[c], mx)
            ms.append(m_new)
            ps.append(jnp.exp2(s - m_new).astype(BF16))
            alphas.append(None if ms_prev is None else jnp.exp2(ms_prev[c] - m_new))
        pv = [_dot(vs[hh], ps[c]) for c, (hh, qs) in enumerate(chains)]
        return tuple(ms), tuple(pv[c] if accs is None else alphas[c] * accs[c] + pv[c] for c in range(len(chains)))

    last = jnp.maximum(own - 1, 0)
    issue_scores(own, sa_scr)
    issue_scores(0, sb_scr)
    state = absorb(own, sa_scr, None, None, True)

    def body(i, state):
        issue_scores(2 * i + 1, sa_scr)
        state = absorb(2 * i, sb_scr, state[0], state[1], False)
        issue_scores(jnp.minimum(2 * i + 2, last), sb_scr)
        return absorb(2 * i + 1, sa_scr, state[0], state[1], False)

    state = lax.fori_loop(0, own // 2, body, state)
    ms, accs = lax.cond(own % 2 == 1, lambda st: absorb(last, sb_scr, st[0], st[1], False), lambda st: st, state)
    for qs in range(nsub):
        sub = [accs[c] for c, (hh, q2) in enumerate(chains) if q2 == qs]
        o = jnp.concatenate([a[:DH_C] / a[DH_C:DH_C + 1] for a in sub], axis=0)
        o_ref[0, qs * TQ_C:(qs + 1) * TQ_C, :] = o.T.astype(o_ref.dtype)


def moba_prompt(q_t, kvt):
    B, W, T = q_t.shape
    nblk = T // MOBA_BLOCK
    assert nblk <= AUG - DH_C and nblk % 8 == 0
    km, ka, vta = pl.pallas_call(
        functools.partial(_moba_kprep_body, nblk=nblk),
        grid=(B, nblk),
        in_specs=[pl.BlockSpec((1, W, MOBA_BLOCK), lambda b, n: (b, 0, n)),
                  pl.BlockSpec((1, W, MOBA_BLOCK), lambda b, n: (b, 1, n))],
        out_specs=[pl.BlockSpec((1, nblk, W), lambda b, n: (b, 0, 0)),
                   pl.BlockSpec((1, MOBA_BLOCK, H_C * AUG), lambda b, n: (b, n, 0)),
                   pl.BlockSpec((1, H_C * VROWS, MOBA_BLOCK), lambda b, n: (b, 0, n))],
        out_shape=[jax.ShapeDtypeStruct((B, nblk, W), F32),
                   jax.ShapeDtypeStruct((B, T, H_C * AUG), BF16),
                   jax.ShapeDtypeStruct((B, H_C * VROWS, T), BF16)],
        compiler_params=_params(("parallel", "arbitrary")),
        name="moba_kprep",
    )(kvt, kvt)
    qt = pl.pallas_call(
        functools.partial(_moba_gate_body, nblk=nblk),
        grid=(B, nblk),
        in_specs=[pl.BlockSpec((1, W, MOBA_BLOCK), lambda b, i: (b, 0, i)),
                  pl.BlockSpec((1, nblk, W), lambda b, i: (b, 0, 0))],
        out_specs=pl.BlockSpec((1, H_C * AUG, MOBA_BLOCK), lambda b, i: (b, 0, i)),
        out_shape=jax.ShapeDtypeStruct((B, H_C * AUG, T), BF16),
        compiler_params=_params(("parallel", "parallel")),
        name="moba_gate",
    )(q_t, km)
    return pl.pallas_call(
        _moba_attn_body,
        grid=(B, H_C // HEADS_C, nblk),
        in_specs=[pl.BlockSpec((1, HEADS_C * AUG, MOBA_BLOCK), lambda b, p, j: (b, p, j)),
                  pl.BlockSpec((1, T, HEADS_C * AUG), lambda b, p, j: (b, 0, p)),
                  pl.BlockSpec((1, HEADS_C * VROWS, T), lambda b, p, j: (b, p, 0))],
        out_specs=pl.BlockSpec((1, MOBA_BLOCK, HEADS_C * DH_C), lambda b, p, j: (b, j, p)),
        out_shape=jax.ShapeDtypeStruct((B, T, W), BF16),
        scratch_shapes=[pltpu.VMEM((HEADS_C * (MOBA_BLOCK // TQ_C), MOBA_BLOCK, TQ_C), F32)] * 2,
        compiler_params=_params(("parallel", "parallel", "arbitrary"), big=True),
        name="moba_attn",
    )(qt, ka, vta)


PAGES_PER_STEP = 32
HEADS_STEP_C = 4


def _moba_step_gate_body(pt_ref, q_ref, *refs, pages_per_block, n_past):
    page_refs = refs[:PAGES_PER_STEP]
    sel_ref, g_scr = refs[PAGES_PER_STEP], refs[PAGES_PER_STEP + 1]
    s = pl.program_id(1)
    q = q_ref[0]
    W = q.shape[0]
    rows = page_refs[0].shape[-1]
    per_step = PAGES_PER_STEP // pages_per_block
    lane = lax.broadcasted_iota(jnp.int32, (W, LANES), 1)

    @pl.when(s == 0)
    def _():
        g_scr[...] = jnp.zeros_like(g_scr)

    g = g_scr[...]
    for j in range(per_step):
        tot = None
        for i in range(pages_per_block):
            t = page_refs[j * pages_per_block + i][0, 0].reshape(W, rows)
            tot = t if tot is None else tot + t
        kmean = jnp.sum(tot, axis=1, keepdims=True) / (pages_per_block * rows)
        g = jnp.where(lane == s * per_step + j, q * kmean, g)
    g_scr[...] = g

    @pl.when(s == pl.num_programs(1) - 1)
    def _():
        seg = (lax.broadcasted_iota(jnp.int32, (8, W), 1) // DH_C ==
               lax.broadcasted_iota(jnp.int32, (8, W), 0)).astype(F32)
        blk = lax.broadcasted_iota(jnp.int32, (8, LANES), 1)
        blk_f = blk.astype(F32)
        gate = jnp.where(blk < n_past, _dot(seg, g, hi=True), NEG)
        sel = jnp.zeros((8, LANES), F32)
        for r in range(MOBA_TOPK):
            mx = jnp.max(gate, axis=1, keepdims=True)
            first = jnp.min(jnp.where(gate == mx, blk_f, float(LANES)), axis=1, keepdims=True)
            sel = jnp.where(blk == r, first, sel)
            gate = jnp.where(blk_f == first, -jnp.inf, gate)
        sel_ref[0] = sel.astype(jnp.int32)


def _moba_step_attn_body(pt_ref, sel_ref, q_ref, kn_ref, vn_ref, *refs, n_pages):
    nh = HEADS_STEP_C
    o_ref = refs[2 * nh * n_pages]
    scale = DH_C ** -0.5
    for hh in range(nh):
        k_refs = refs[hh * n_pages:(hh + 1) * n_pages]
        v_refs = refs[(nh + hh) * n_pages:(nh + hh + 1) * n_pages]
        q = q_ref[0, hh]
        kn = kn_ref[0, 0, hh]
        vn = vn_ref[0, 0, hh]
        ss = [jnp.sum(k_ref[0, 0, 0] * q, axis=0, keepdims=True) * scale for k_ref in k_refs]
        s_self = jnp.sum(kn * q, axis=0, keepdims=True) * scale
        mx = s_self
        for sj in ss:
            mx = jnp.maximum(mx, jnp.max(sj, axis=1, keepdims=True))
        p_self = jnp.exp(s_self - mx)
        den = p_self
        o = p_self * vn
        for sj, v_ref in zip(ss, v_refs):
            p = jnp.exp(sj - mx)
            den = den + jnp.sum(p, axis=1, keepdims=True)
            o = o + jnp.sum(v_ref[0, 0, 0] * p, axis=1, keepdims=True)
        o_ref[0, hh] = o / den


def moba_step(q, kv_new, pool_t, page_table):
    B, W, _ = q.shape
    page = pool_t.shape[-1]
    n_pt = page_table.shape[1]
    ppb = MOBA_BLOCK // page
    n_past = n_pt // ppb
    assert MOBA_TOPK <= n_past <= LANES and PAGES_PER_STEP % ppb == 0 and n_pt % PAGES_PER_STEP == 0
    pt_flat = page_table.reshape(-1)
    kpage = (1, 1, H_C, DH_C, page)
    sel = pl.pallas_call(
        functools.partial(_moba_step_gate_body, pages_per_block=ppb, n_past=n_past),
        grid_spec=pltpu.PrefetchScalarGridSpec(
            num_scalar_prefetch=1,
            grid=(B, n_pt // PAGES_PER_STEP),
            in_specs=[pl.BlockSpec((1, W, 1), lambda b, s, pt: (b, 0, 0))] +
                     [pl.BlockSpec(kpage, functools.partial(
                         lambda b, s, pt, i: (pt[b * n_pt + s * PAGES_PER_STEP + i], 0, 0, 0, 0), i=i))
                      for i in range(PAGES_PER_STEP)],
            out_specs=pl.BlockSpec((1, 8, LANES), lambda b, s, pt: (b, 0, 0)),
            scratch_shapes=[pltpu.VMEM((W, LANES), F32)]),
        out_shape=jax.ShapeDtypeStruct((B, 8, LANES), jnp.int32),
        compiler_params=_params(("parallel", "arbitrary"), big=True),
        name="moba_step_gate",
    )(pt_flat, q, *([pool_t] * PAGES_PER_STEP))
    sel_flat = sel[:, :H_C, :MOBA_TOPK].reshape(-1)
    n_pages = MOBA_TOPK * ppb

    nh = HEADS_STEP_C

    def page_map(b, hq, pt, sl, hh, r, i, kv):
        h = hq * nh + hh
        blk = sl[(b * H_C + h) * MOBA_TOPK + r]
        return (pt[b * n_pt + blk * ppb + i], kv, h, 0, 0)

    def page_specs(kv):
        return [pl.BlockSpec((1, 1, 1, DH_C, page), functools.partial(page_map, hh=hh, r=r, i=i, kv=kv))
                for hh in range(nh) for r in range(MOBA_TOPK) for i in range(ppb)]

    q4 = q.reshape(B, H_C, DH_C, 1)
    head = pl.BlockSpec((1, nh, DH_C, 1), lambda b, hq, pt, sl: (b, hq, 0, 0))
    new = lambda kv: pl.BlockSpec((1, 1, nh, DH_C, 1), lambda b, hq, pt, sl: (b, kv, hq, 0, 0))
    return pl.pallas_call(
        functools.partial(_moba_step_attn_body, n_pages=n_pages),
        grid_spec=pltpu.PrefetchScalarGridSpec(
            num_scalar_prefetch=2,
            grid=(B, H_C // nh),
            in_specs=[head, new(0), new(1)] + page_specs(0) + page_specs(1),
            out_specs=head),
        out_shape=jax.ShapeDtypeStruct(q4.shape, F32),
        compiler_params=_params(("parallel", "parallel")),
        name="moba_step_attn",
    )(pt_flat, sel_flat, q4, kv_new, kv_new, *([pool_t] * (2 * nh * n_pages)))


def _hgrn_gates(f, wlb_ref, layer):
    w = wlb_ref[...]
    e = jnp.exp(w - jnp.max(w, axis=0, keepdims=True))
    sm = e / jnp.sum(e, axis=0, keepdims=True)
    lb = jnp.sum(sm[1:layer + 1], axis=0, keepdims=True)
    sg = _sigmoid(f)
    return jnp.log(lb + (1.0 - lb) * sg), (1.0 - lb) * (1.0 - sg)


def _hgrn_body(d_ref, wlb_ref, go_ref, o_ref, S_ref, *, layer):
    L, SB = CHUNK_D, SUB_D
    HD = H_D * DK_D

    @pl.when(pl.program_id(0) == 0)
    def _():
        S_ref[...] = jnp.zeros_like(S_ref)

    nb = d_ref.shape[0]
    ri = lax.broadcasted_iota(jnp.int32, (L, L), 0)
    ci = lax.broadcasted_iota(jnp.int32, (L, L), 1)
    tril = (ci <= ri).astype(F32)
    sub_row = lax.broadcasted_iota(jnp.int32, (SB, 1), 0)
    gates = [_hgrn_gates(d_ref[b, :, HD:2 * HD], wlb_ref, layer) for b in range(nb)]
    b_all = [_dot(tril, lf, hi=True) for lf, _ in gates]
    chains = [(b, h) for b in range(nb) for h in range(H_D)]
    nch = range(len(chains))
    blocks = range(SB, L, SB)
    q = [d_ref[b, :, h * DK_D:(h + 1) * DK_D] for b, h in chains]
    v = [d_ref[b, :, 2 * HD + h * DV_D:2 * HD + (h + 1) * DV_D] for b, h in chains]
    bc = [b_all[b][:, h * DK_D:(h + 1) * DK_D] for b, h in chains]
    k = [gates[b][1][:, h * DK_D:(h + 1) * DK_D] for b, h in chains]
    S = [S_ref[b, h] for b, h in chains]
    vb = [x.astype(BF16) for x in v]
    o_inter = [_dot((q[c] * jnp.exp(bc[c])).astype(BF16), S[c].astype(BF16)) for c in nch]
    rnd = lambda x: x.astype(BF16).astype(F32)
    vr = [rnd(x) for x in v]
    a_off = []
    for c in nch:
        for r0 in blocks:
            ref_b = bc[c][r0 - 1:r0]
            qs = rnd(q[c][r0:r0 + SB] * jnp.exp(bc[c][r0:r0 + SB] - ref_b))
            ks = rnd(k[c][:r0] * jnp.exp(ref_b - bc[c][:r0]))
            a_off.append(rnd(_dot_nt(qs, ks)))
    kw = [(k[c] * jnp.exp(bc[c][L - 1:L] - bc[c])).astype(BF16) for c in nch]
    kv = [_dot_tn(kw[c], vb[c]) for c in nch]
    o_off = [_dot(a_off[c * len(blocks) + i], vr[c][:r0]) for c in nch for i, r0 in enumerate(blocks)]
    for c, (b, h) in enumerate(chains):
        b2 = bc[c] * LOG2E
        o_blocks = []
        for i, r0 in enumerate(range(0, L, SB)):
            qI = q[c][r0:r0 + SB]
            bI = b2[r0:r0 + SB]
            oI = o_inter[c][r0:r0 + SB]
            if r0 > 0:
                oI = oI + o_off[c * len(blocks) + i - 1]
            for s in range(SB):
                row = r0 + s
                a = jnp.sum(qI * k[c][row:row + 1] * jnp.exp2(bI - b2[row:row + 1]), axis=1, keepdims=True)
                oI = oI + jnp.where(sub_row >= s, a, 0.0) * v[c][row:row + 1]
            o_blocks.append(oI)
        o = jnp.concatenate(o_blocks, axis=0)
        gd = d_ref[b, :, 3 * HD + h * DV_D:3 * HD + (h + 1) * DV_D]
        o_ref[b, :, h * DV_D:(h + 1) * DV_D] = (_rms(o, go_ref[...]) * _silu(gd)).astype(o_ref.dtype)
        S_ref[b, h] = _col_of_row(jnp.exp(bc[c][L - 1:L])) * S[c] + kv[c]


def hgrn_prompt(d, w_lb, g_out, layer):
    B, T, W = d.shape
    L = CHUNK_D
    HD = H_D * DK_D
    return pl.pallas_call(
        functools.partial(_hgrn_body, layer=layer),
        grid=(T // L,),
        in_specs=[pl.BlockSpec((B, L, W), lambda c: (0, c, 0)),
                  pl.BlockSpec(w_lb.shape, lambda c: (0, 0)),
                  pl.BlockSpec((1, DV_D), lambda c: (0, 0))],
        out_specs=[pl.BlockSpec((B, L, H_D * DV_D), lambda c: (0, c, 0)),
                   pl.BlockSpec((B, H_D, DK_D, DV_D), lambda c: (0, 0, 0, 0))],
        out_shape=[jax.ShapeDtypeStruct((B, T, H_D * DV_D), BF16),
                   jax.ShapeDtypeStruct((B, H_D, DK_D, DV_D), F32)],
        compiler_params=_params(("arbitrary",), big=True),
        name="hgrn_prompt",
    )(d, w_lb, g_out.reshape(1, DV_D))


def _hgrn_step_body(d_ref, wlb_ref, go_ref, S0_ref, o_ref, S_ref, *, layer):
    HD = H_D * DK_D
    lf, kd = _hgrn_gates(d_ref[0, :, HD:2 * HD], wlb_ref, layer)
    eb = jnp.exp(lf)
    for h in range(H_D):
        sl = slice(h * DK_D, (h + 1) * DK_D)
        q = d_ref[0, :, sl]
        v = d_ref[0, :, 2 * HD + h * DV_D:2 * HD + (h + 1) * DV_D]
        gd = d_ref[0, :, 3 * HD + h * DV_D:3 * HD + (h + 1) * DV_D]
        k = kd[:, sl]
        e = eb[:, sl]
        S = S0_ref[0, h]
        a = jnp.sum(q * k, axis=1, keepdims=True)
        o = a * v + _dot(jnp.broadcast_to(q * e, (8, DK_D)), S, hi=True)[0:1]
        o_ref[0, :, h * DV_D:(h + 1) * DV_D] = _rms(o, go_ref[...]) * _silu(gd)
        S_ref[0, h] = _col_of_row(e) * S + _col_of_row(k) * v


def hgrn_step(d, w_lb, g_out, S0, layer):
    B, _, W = d.shape
    ss = pl.BlockSpec((1, H_D, DK_D, DV_D), lambda b: (b, 0, 0, 0))
    return pl.pallas_call(
        functools.partial(_hgrn_step_body, layer=layer),
        grid=(B,),
        in_specs=[pl.BlockSpec((1, 1, W), lambda b: (b, 0, 0)),
                  pl.BlockSpec(w_lb.shape, lambda b: (0, 0)),
                  pl.BlockSpec((1, DV_D), lambda b: (0, 0)), ss],
        out_specs=[pl.BlockSpec((1, 1, H_D * DV_D), lambda b: (b, 0, 0)), ss],
        out_shape=[jax.ShapeDtypeStruct((B, 1, H_D * DV_D), F32), jax.ShapeDtypeStruct(S0.shape, F32)],
        compiler_params=_params(("parallel",)),
        name="hgrn_step",
    )(d, w_lb, g_out.reshape(1, DV_D), S0)


N_A = 4 * H_A * DK_A
N_B = 3 * len(SWA_GROUPS) * GW_B
N_C = H_C * DH_C
N_D = 4 * H_D * DK_D


def _even_weights(w_in):
    gates = jnp.pad(w_in[:, N_A:N_A + 2 * H_A], ((0, 0), (0, LANES - 2 * H_A)))
    return jnp.concatenate([w_in[:, :N_A], w_in[:, N_A + 2 * H_A:], gates], axis=1)


def _trunk(x, ada, prompt, st_even, st_odd, P):
    hi = not prompt
    G, R, D = x.shape
    wdt = F32 if hi else BF16
    tm_lin = 256
    tm_out = 512
    new_even = new_odd = None
    for l in range(2):
        sh1, sc1, g1, sh2, sc2, g2 = [ada[l][..., i * D:(i + 1) * D] for i in range(6)]
        if l == 0:
            w = _even_weights(P['w_in_even']).astype(wdt)
            splits = (N_A, N_B, LANES)
            if hi:
                outs, off = [], 0
                for n in splits:
                    outs += norm_mod_proj(x, P['g_pre_mix'][l], sc1, sh1, w[:, off:off + n], (n,), hi, tm_lin)
                    off += n
                a, qkv, gg = outs
            else:
                a, qkv, gg = norm_mod_proj(x, P['g_pre_mix'][l], sc1, sh1, w, splits, hi, tm_lin)
            bg = jnp.pad(P['b_gate_a'].reshape(1, 2 * H_A), ((0, 0), (0, LANES - 2 * H_A)))
            if prompt:
                ha, C, n, m = mlstm_prompt(a, gg, bg)
                swa = [swa_prompt(qkv, g, dil) for g, (_, dil) in enumerate(SWA_GROUPS)]
                T = R
                bufs = []
                for g, (win, _) in enumerate(SWA_GROUPS):
                    wb = min(win, T)
                    kb = qkv[:, T - wb:, (3 + g) * GW_B:(4 + g) * GW_B]
                    vb = qkv[:, T - wb:, (6 + g) * GW_B:(7 + g) * GW_B]
                    bufs.append(jnp.concatenate([kb, vb], axis=-1).reshape(G, wb, 2, H_B, DH_B))
                new_even = (C, n, m[:, 0, :H_A], bufs[0], bufs[1], bufs[2])
            else:
                B = R
                C0, n0, m0 = st_even[:3]
                m0p = jnp.pad(m0, ((0, 0), (0, LANES - H_A))).reshape(B, 1, LANES)
                tok = lambda t: t.reshape(B, 1, t.shape[-1])
                ha, C, n, m = mlstm_step(tok(a), tok(gg), bg, C0, n0, m0p)
                swa, bufs = [], []
                ng = len(SWA_GROUPS)
                qkv5 = qkv.reshape(B, 3, ng, H_B, DH_B, 1)
                for g, (_, dil) in enumerate(SWA_GROUPS):
                    buf_t = jnp.transpose(st_even[3 + g], (0, 2, 3, 4, 1))
                    o, lse, nb = swa_step(qkv5[:, 0, g], qkv5[:, 1:, g], buf_t, dil)
                    o, lse = o.reshape(1, B, GW_B), lse.reshape(1, B, GW_B)
                    swa.append(([o[..., p * LANES:(p + 1) * LANES] for p in range(NPAIR_B)],
                                [lse[..., p * LANES:(p + 1) * LANES] for p in range(NPAIR_B)]))
                    bufs.append(jnp.transpose(nb, (0, 4, 1, 2, 3)))
                ha = ha.reshape(1, B, -1)
                new_even = (C, n, m[:, 0, :H_A], bufs[0], bufs[1], bufs[2])
            swa_os = [swa[g][0][p] for p in range(NPAIR_B) for g in range(len(SWA_GROUPS))]
            swa_ls = [swa[g][1][p] for p in range(NPAIR_B) for g in range(len(SWA_GROUPS))]
            x = mix_out(x, g1, P['g_post_mix'][l], [ha], (swa_os, swa_ls), P['w_out_even'].astype(wdt), hi, tm_out)
        else:
            w = P['w_in_odd'].astype(wdt)
            if prompt:
                d, q_t, kvt = norm_mod_proj(x, P['g_pre_mix'][l], sc1, sh1, w[:, 3 * N_C:], (N_D,), hi, tm_lin,
                                            wt=w[:, :3 * N_C].T, t_splits=(N_C, 2 * N_C))
                hc = moba_prompt(q_t, kvt)
                od, S = hgrn_prompt(d, P['w_lb'], P['g_out_d'], l)
                new_odd = (jnp.transpose(kvt.reshape(G, 2, H_C, DH_C, R), (0, 4, 1, 2, 3)), S)
            else:
                B = R
                outs, off = [], 0
                for n in (N_C, 2 * N_C, N_D):
                    outs += norm_mod_proj(x, P['g_pre_mix'][l], sc1, sh1, w[:, off:off + n], (n,), hi, tm_lin)
                    off += n
                qc, kvc, d = outs
                pool, page_table, S0 = st_odd
                hc = moba_step(qc.reshape(B, N_C, 1), kvc.reshape(B, 2, H_C, DH_C, 1),
                               jnp.transpose(pool, (0, 2, 3, 4, 1)), page_table)
                od, S = hgrn_step(d.reshape(B, 1, N_D), P['w_lb'], P['g_out_d'], S0, l)
                hc = hc.reshape(1, B, -1)
                od = od.reshape(1, B, -1)
                new_odd = (kvc.reshape(B, 1, 2, H_C, DH_C), S)
            x = mix_out(x, g1, P['g_post_mix'][l], [hc, od], None, P['w_out_odd'].astype(wdt), hi, tm_out)
        x = ffn(x, P['g_pre_ffn'][l], sc2, sh2, g2, P['g_post_ffn'][l],
                P['w_ffn_gate'][l].astype(wdt), P['w_ffn_up'][l].astype(wdt), P['w_ffn_down'][l].astype(wdt),
                hi, tm=512, tf=256 if hi else 1408)
    return x, new_even, new_odd


def kernel(x_prompt, x_sample, state_mlstm_C, state_mlstm_n, state_mlstm_m, cache_swa_w128, cache_swa_w512, cache_swa_w2048, cache_moba_kv, state_hgrn_S, page_table, c_prompt, c_sample, w_ada, b_ada, g_pre_mix, g_post_mix, g_pre_ffn, g_post_ffn, w_in_even, b_gate_a, w_out_even, w_in_odd, w_lb, g_out_d, w_out_odd, w_ffn_gate, w_ffn_up, w_ffn_down):
    P = {'g_pre_mix': g_pre_mix, 'g_post_mix': g_post_mix, 'g_pre_ffn': g_pre_ffn, 'g_post_ffn': g_post_ffn,
         'w_in_even': w_in_even, 'b_gate_a': b_gate_a, 'w_out_even': w_out_even, 'w_in_odd': w_in_odd,
         'w_lb': w_lb, 'g_out_d': g_out_d, 'w_out_odd': w_out_odd, 'w_ffn_gate': w_ffn_gate,
         'w_ffn_up': w_ffn_up, 'w_ffn_down': w_ffn_down}
    Bp = x_prompt.shape[0]
    Bs = x_sample.shape[0]
    D = x_prompt.shape[-1]
    c_all = jnp.concatenate([c_prompt, c_sample], axis=0)
    c_all = jnp.pad(c_all, ((0, -(Bp + Bs) % 8), (0, 0)))
    ada = ada_all(c_all, w_ada, b_ada)[:, :Bp + Bs]
    ada_p = ada[:, :Bp].reshape(ada.shape[0], Bp, 1, 6 * D)
    ada_s = ada[:, Bp:].reshape(ada.shape[0], 1, Bs, 6 * D)
    y_p, ev_p, od_p = _trunk(x_prompt, ada_p, True, None, None, P)
    y_s, ev_s, od_s = _trunk(
        x_sample.reshape(1, Bs, D), ada_s, False,
        (state_mlstm_C, state_mlstm_n, state_mlstm_m, cache_swa_w128, cache_swa_w512, cache_swa_w2048),
        (cache_moba_kv, page_table, state_hgrn_S), P)
    return ((y_p, y_s.reshape(x_sample.shape)) + tuple(ev_p[:3]) + tuple(ev_s[:3]) + tuple(ev_p[3:])
            + tuple(ev_s[3:]) + (od_p[0], od_s[0], od_p[1], od_s[1]))
```

```python
import functools

import jax
import jax.numpy as jnp
from jax import lax
from jax.experimental import pallas as pl
from jax.experimental.pallas import tpu as pltpu

F32 = jnp.float32
BF16 = jnp.bfloat16
HI = lax.Precision.HIGHEST

EPS = 1e-6
NEG = -1e30

H_A, DK_A, DV_A, CHUNK_A = 4, 128, 128, 128
SWA_GROUPS = ((128, 1), (512, 4), (2048, 16))
H_B, DH_B, SWA_BAND = 4, 64, 128
GW_B = H_B * DH_B
H_C, DH_C, MOBA_BLOCK, MOBA_TOPK = 8, 64, 256, 3
H_D, DK_D, DV_D, CHUNK_D, SUB_D = 4, 128, 128, 64, 8

LANES = 128
VMEM_LIMIT = 56 << 20


def _params(sem, big=False):
    return pltpu.CompilerParams(dimension_semantics=sem,
                                vmem_limit_bytes=VMEM_LIMIT if big else None)


def _dot(a, b, hi=False):
    return jnp.dot(a, b, preferred_element_type=F32, precision=HI if hi else None)


def _dot_nt(a, b, hi=False):
    return lax.dot_general(a, b, (((1,), (1,)), ((), ())), preferred_element_type=F32,
                           precision=HI if hi else None)


def _dot_tn(a, b, hi=False):
    return lax.dot_general(a, b, (((0,), (0,)), ((), ())), preferred_element_type=F32,
                           precision=HI if hi else None)


def _rms(x, g):
    return x * lax.rsqrt(jnp.mean(x * x, axis=-1, keepdims=True) + EPS) * g


def _sigmoid(x):
    return 1.0 / (1.0 + jnp.exp(-x))


def _silu(x):
    return x * _sigmoid(x)


def _log_sigmoid(x):
    return jnp.minimum(x, 0.0) - jnp.log(1.0 + jnp.exp(-jnp.abs(x)))


def _col_of_row(row):
    n = row.shape[-1]
    eye = lax.broadcasted_iota(jnp.int32, (n, n), 0) == lax.broadcasted_iota(jnp.int32, (n, n), 1)
    return jnp.sum(jnp.where(eye, row, 0.0), axis=1, keepdims=True)


def _ada_body(c_ref, w_ref, b_ref, o_ref):
    o_ref[0] = _dot(_silu(c_ref[...]), w_ref[0], hi=True) + b_ref[0]


def ada_all(c, w_ada, b_ada, tn=768):
    depth, d, n = w_ada.shape
    r = c.shape[0]
    return pl.pallas_call(
        _ada_body,
        grid=(depth, n // tn),
        in_specs=[pl.BlockSpec((r, d), lambda l, j: (0, 0)),
                  pl.BlockSpec((1, d, tn), lambda l, j: (l, 0, j)),
                  pl.BlockSpec((1, 1, tn), lambda l, j: (l, 0, j))],
        out_specs=pl.BlockSpec((1, r, tn), lambda l, j: (l, 0, j)),
        out_shape=jax.ShapeDtypeStruct((depth, r, n), F32),
        compiler_params=_params(("parallel", "parallel")),
        name="ada",
    )(c, w_ada, b_ada.reshape(depth, 1, n))


def _proj_body(x_ref, g_ref, sc_ref, sh_ref, w_ref, *refs, splits, hi, cw, t_splits):
    h = _rms(x_ref[0], g_ref[...]) * (1.0 + sc_ref[0]) + sh_ref[0]
    hc = h if hi else h.astype(BF16)
    o_refs = refs[1:] if t_splits else refs
    off = 0
    for o_ref, n in zip(o_refs, splits):
        for c0 in range(0, n, cw):
            c1 = min(c0 + cw, n)
            o_ref[0, :, c0:c1] = _dot(hc, w_ref[:, off + c0:off + c1], hi)
        off += n
    if t_splits:
        wt_ref, off = refs[0], 0
        for ot_ref, n in zip(refs[1 + len(splits):], t_splits):
            for r0 in range(0, n, cw):
                ot_ref[0, r0:r0 + cw, :] = _dot_nt(wt_ref[off + r0:off + r0 + cw, :], hc, hi)
            off += n


def _mod_spec(mod, tm):
    if mod.shape[1] == 1:
        return pl.BlockSpec((1, 1, mod.shape[2]), lambda gi, ri: (gi, 0, 0))
    return pl.BlockSpec((1, tm, mod.shape[2]), lambda gi, ri: (gi, ri, 0))


def norm_mod_proj(x, g, sc, sh, w, splits, hi, tm, wt=None, t_splits=()):
    G, R, D = x.shape
    tm = min(tm, R)
    n_all = sum(splits)
    n_t = sum(t_splits)
    ops = [x, g.reshape(1, D), sc, sh, w]
    in_specs = [pl.BlockSpec((1, tm, D), lambda gi, ri: (gi, ri, 0)),
                pl.BlockSpec((1, D), lambda gi, ri: (0, 0)),
                _mod_spec(sc, tm), _mod_spec(sh, tm),
                pl.BlockSpec((D, n_all), lambda gi, ri: (0, 0))]
    out_specs = [pl.BlockSpec((1, tm, n), lambda gi, ri: (gi, ri, 0)) for n in splits]
    out_shape = [jax.ShapeDtypeStruct((G, R, n), F32) for n in splits]
    if n_t:
        ops.append(wt)
        in_specs.append(pl.BlockSpec((n_t, D), lambda gi, ri: (0, 0)))
        out_specs += [pl.BlockSpec((1, n, tm), lambda gi, ri: (gi, 0, ri)) for n in t_splits]
        out_shape += [jax.ShapeDtypeStruct((G, n, R), F32) for n in t_splits]
    return pl.pallas_call(
        functools.partial(_proj_body, splits=tuple(splits), hi=hi, cw=512, t_splits=tuple(t_splits)),
        grid=(G, R // tm),
        in_specs=in_specs,
        out_specs=out_specs,
        out_shape=out_shape,
        compiler_params=_params(("parallel", "parallel"), big=True),
        name="proj",
    )(*ops)


def _mixout_body(*refs, n_parts, swa, hi):
    x_ref, g1_ref, gp_ref = refs[:3]
    part_refs = refs[3:3 + n_parts]
    pos = 3 + n_parts
    ng = len(SWA_GROUPS)
    o_refs = refs[pos:pos + swa * ng]
    l_refs = refs[pos + swa * ng:pos + 2 * swa * ng]
    pos += 2 * swa * ng
    w_ref, out_ref = refs[pos], refs[pos + 1]

    def cast(a):
        return a if hi else a.astype(BF16)

    acc = None
    off = 0
    for p_ref in part_refs:
        a = p_ref[0]
        k = a.shape[-1]
        t = _dot(cast(a), w_ref[off:off + k, :], hi)
        acc = t if acc is None else acc + t
        off += k
    for c in range(swa):
        ls = [l_ref[0] for l_ref in l_refs[c * ng:(c + 1) * ng]]
        os_ = [o_ref[0] for o_ref in o_refs[c * ng:(c + 1) * ng]]
        mx = functools.reduce(jnp.maximum, ls)
        es = [jnp.exp(l - mx) for l in ls]
        hb = sum(e * o for e, o in zip(es, os_)) / sum(es)
        acc = acc + _dot(cast(hb), w_ref[off:off + hb.shape[-1], :], hi)
        off += hb.shape[-1]
    out_ref[0] = x_ref[0] + g1_ref[0] * _rms(acc, gp_ref[...])


def mix_out(x, g1, gpost, parts, swa_parts, w, hi, tm):
    G, R, D = x.shape
    tm = min(tm, R)
    row = lambda a: pl.BlockSpec((1, tm, a.shape[2]), lambda gi, ri: (gi, ri, 0))
    ops = [x, g1, gpost.reshape(1, D)] + list(parts)
    specs = [row(x), _mod_spec(g1, tm), pl.BlockSpec((1, D), lambda gi, ri: (0, 0))] + [row(p) for p in parts]
    swa = 0
    if swa_parts is not None:
        swa = len(swa_parts[0]) // len(SWA_GROUPS)
        ops += list(swa_parts[0]) + list(swa_parts[1])
        specs += [row(a) for a in ops[-2 * len(swa_parts[0]):]]
    ops.append(w)
    specs.append(pl.BlockSpec(w.shape, lambda gi, ri: (0, 0)))
    return pl.pallas_call(
        functools.partial(_mixout_body, n_parts=len(parts), swa=swa, hi=hi),
        grid=(G, R // tm),
        in_specs=specs,
        out_specs=row(x),
        out_shape=jax.ShapeDtypeStruct(x.shape, F32),
        compiler_params=_params(("parallel", "parallel"), big=True),
        name="mix_out",
    )(*ops)


FFN_CHUNK = 512


def _ffn_body(x_ref, gpre_ref, sc_ref, sh_ref, g2_ref, gpost_ref, wg_ref, wu_ref, wd_ref, out_ref,
              h_scr, acc_scr, *, hi):
    k = pl.program_id(2)

    @pl.when(k == 0)
    def _():
        h = _rms(x_ref[0], gpre_ref[...]) * (1.0 + sc_ref[0]) + sh_ref[0]
        h_scr[...] = h.astype(h_scr.dtype)
        acc_scr[...] = jnp.zeros_like(acc_scr)

    h = h_scr[...]
    tf = wg_ref.shape[1]
    cuts = list(range(0, tf, FFN_CHUNK)) + [tf]
    chunks = list(zip(cuts[:-1], cuts[1:]))
    gu = [(_dot(h, wg_ref[:, c0:c1], hi), _dot(h, wu_ref[:, c0:c1], hi)) for c0, c1 in chunks[:1]]
    acc = None
    for i, (c0, c1) in enumerate(chunks):
        if i + 1 < len(chunks):
            n0, n1 = chunks[i + 1]
            gu.append((_dot(h, wg_ref[:, n0:n1], hi), _dot(h, wu_ref[:, n0:n1], hi)))
        a = _silu(gu[i][0]) * gu[i][1]
        d = _dot(a.astype(h.dtype), wd_ref[c0:c1, :], hi)
        acc = d if acc is None else acc + d
    acc_scr[...] += acc

    @pl.when(k == pl.num_programs(2) - 1)
    def _():
        out_ref[0] = x_ref[0] + g2_ref[0] * _rms(acc_scr[...], gpost_ref[...])


def ffn(x, gpre, sc, sh, g2, gpost, wg, wu, wd, hi, tm, tf):
    G, R, D = x.shape
    FF = wg.shape[1]
    tm = min(tm, R)
    mspec = lambda m: (pl.BlockSpec((1, 1, D), lambda gi, ri, k: (gi, 0, 0)) if m.shape[1] == 1
                       else pl.BlockSpec((1, tm, D), lambda gi, ri, k: (gi, ri, 0)))
    vec = pl.BlockSpec((1, D), lambda gi, ri, k: (0, 0))
    wmode = dict(pipeline_mode=pl.Buffered(1)) if tf == FF else {}
    return pl.pallas_call(
        functools.partial(_ffn_body, hi=hi),
        grid=(G, R // tm, FF // tf),
        in_specs=[pl.BlockSpec((1, tm, D), lambda gi, ri, k: (gi, ri, 0)), vec, mspec(sc), mspec(sh),
                  mspec(g2), vec,
                  pl.BlockSpec((D, tf), lambda gi, ri, k: (0, k), **wmode),
                  pl.BlockSpec((D, tf), lambda gi, ri, k: (0, k), **wmode),
                  pl.BlockSpec((tf, D), lambda gi, ri, k: (k, 0), **wmode)],
        out_specs=pl.BlockSpec((1, tm, D), lambda gi, ri, k: (gi, ri, 0)),
        out_shape=jax.ShapeDtypeStruct(x.shape, F32),
        scratch_shapes=[pltpu.VMEM((tm, D), F32 if hi else BF16), pltpu.VMEM((tm, D), F32)],
        compiler_params=_params(("parallel", "parallel", "arbitrary"), big=True),
        name="ffn",
    )(x, gpre.reshape(1, D), sc, sh, g2, gpost.reshape(1, D), wg, wu, wd)


def _mlstm_body(a_ref, g_ref, bg_ref, h_ref, C_ref, n_ref, m_ref):
    L = CHUNK_A
    HD = H_A * DK_A

    @pl.when(pl.program_id(0) == 0)
    def _():
        C_ref[...] = jnp.zeros_like(C_ref)
        n_ref[...] = jnp.zeros_like(n_ref)
        m_ref[...] = jnp.zeros_like(m_ref)

    nb = a_ref.shape[0]
    lane = lax.broadcasted_iota(jnp.int32, (L, LANES), 1)
    ri = lax.broadcasted_iota(jnp.int32, (L, L), 0)
    ci = lax.broadcasted_iota(jnp.int32, (L, L), 1)
    causal = ci <= ri
    tril = causal.astype(F32)
    m_lane = lax.broadcasted_iota(jnp.int32, (1, LANES), 1)
    gates_t, bcum, bcum_t, m_all = [], [], [], []
    for b in range(nb):
        pre = g_ref[b] + bg_ref[...]
        gates = jnp.where(lane >= H_A, _log_sigmoid(pre), pre)
        bcum.append(_dot(tril, gates, hi=True))
        gates_t.append(gates.T)
        bcum_t.append(bcum[b].T)
        m_all.append(m_ref[b])
    m_out = list(m_all)
    chains = [(b, h) for b in range(nb) for h in range(H_A)]
    q = [a_ref[b, :, h * DK_A:(h + 1) * DK_A] for b, h in chains]
    k = [a_ref[b, :, HD + h * DK_A:HD + (h + 1) * DK_A] * (DK_A ** -0.5) for b, h in chains]
    vb = [a_ref[b, :, 2 * HD + h * DV_A:2 * HD + (h + 1) * DV_A].astype(BF16) for b, h in chains]
    qb = [x.astype(BF16) for x in q]
    C = [C_ref[b, h] for b, h in chains]
    n_row = [n_ref[b, h:h + 1, :] for b, h in chains]
    qk = [_dot_nt(qb[c], k[c].astype(BF16)) for c in range(len(chains))]
    qC = [_dot(qb[c], C[c].astype(BF16)) for c in range(len(chains))]
    s, winter, mt, kw, ws, wc = [], [], [], [], [], []
    for c, (b, h) in enumerate(chains):
        ig_row = gates_t[b][h:h + 1, :]
        b_col = bcum[b][:, H_A + h:H_A + h + 1]
        b_row = bcum_t[b][H_A + h:H_A + h + 1, :]
        m = m_all[b][:, h:h + 1]
        dmat = jnp.where(causal, b_col - b_row + ig_row, NEG)
        inter = b_col + m
        mt.append(jnp.maximum(inter, jnp.max(dmat, axis=1, keepdims=True)))
        winter.append(jnp.exp(inter - mt[c]))
        s.append(qk[c] * jnp.exp(dmat - mt[c]))
        bl = b_row[:, L - 1:L]
        gl = bl - b_row + ig_row
        m_new = jnp.maximum(bl + m, jnp.max(gl, axis=1, keepdims=True))
        ws.append(jnp.exp(gl - m_new))
        wc.append(jnp.exp(bl + m - m_new))
        kw.append((k[c].T * ws[c]).astype(BF16))
        m_out[b] = jnp.where(m_lane == h, m_new, m_out[b])
    sv = [_dot(s[c].astype(BF16), vb[c]) for c in range(len(chains))]
    kv = [_dot(kw[c], vb[c]) for c in range(len(chains))]
    kn = [_dot(jnp.broadcast_to(ws[c], (8, L)), k[c], hi=True)[0:1] for c in range(len(chains))]
    for c, (b, h) in enumerate(chains):
        og = a_ref[b, :, 3 * HD + h * DV_A:3 * HD + (h + 1) * DV_A]
        num = sv[c] + winter[c] * qC[c]
        den = jnp.sum(s[c], axis=1, keepdims=True) + winter[c] * jnp.sum(q[c] * n_row[c], axis=1, keepdims=True)
        hh = _sigmoid(og) * (num / jnp.maximum(jnp.abs(den), jnp.exp(-mt[c])))
        h_ref[b, :, h * DV_A:(h + 1) * DV_A] = hh.astype(h_ref.dtype)
        C_ref[b, h] = wc[c] * C[c] + kv[c]
        n_ref[b, h:h + 1, :] = wc[c] * n_row[c] + kn[c]
    for b in range(nb):
        m_ref[b] = m_out[b]


def mlstm_prompt(a, g, bg):
    B, T, _ = a.shape
    L = CHUNK_A
    return pl.pallas_call(
        _mlstm_body,
        grid=(T // L,),
        in_specs=[pl.BlockSpec((B, L, a.shape[2]), lambda c: (0, c, 0)),
                  pl.BlockSpec((B, L, LANES), lambda c: (0, c, 0)),
                  pl.BlockSpec((1, LANES), lambda c: (0, 0))],
        out_specs=[pl.BlockSpec((B, L, H_A * DV_A), lambda c: (0, c, 0)),
                   pl.BlockSpec((B, H_A, DK_A, DV_A), lambda c: (0, 0, 0, 0)),
                   pl.BlockSpec((B, H_A, DK_A), lambda c: (0, 0, 0)),
                   pl.BlockSpec((B, 1, LANES), lambda c: (0, 0, 0))],
        out_shape=[jax.ShapeDtypeStruct((B, T, H_A * DV_A), BF16),
                   jax.ShapeDtypeStruct((B, H_A, DK_A, DV_A), F32),
                   jax.ShapeDtypeStruct((B, H_A, DK_A), F32),
                   jax.ShapeDtypeStruct((B, 1, LANES), F32)],
        compiler_params=_params(("arbitrary",), big=True),
        name="mlstm_prompt",
    )(a, g, bg)


def _mlstm_step_body(a_ref, g_ref, bg_ref, C0_ref, n0_ref, m0_ref, h_ref, C_ref, n_ref, m_ref):
    HD = H_A * DK_A
    pre = g_ref[0] + bg_ref[...]
    lane = lax.broadcasted_iota(jnp.int32, (1, LANES), 1)
    gates = jnp.where(lane >= H_A, _log_sigmoid(pre), pre)
    m_all = m0_ref[0]
    m_out = m_all
    for h in range(H_A):
        q = a_ref[0, :, h * DK_A:(h + 1) * DK_A]
        k = a_ref[0, :, HD + h * DK_A:HD + (h + 1) * DK_A] * (DK_A ** -0.5)
        v = a_ref[0, :, 2 * HD + h * DV_A:2 * HD + (h + 1) * DV_A]
        og = a_ref[0, :, 3 * HD + h * DV_A:3 * HD + (h + 1) * DV_A]
        ig = gates[:, h:h + 1]
        lf = gates[:, H_A + h:H_A + h + 1]
        m = m_all[:, h:h + 1]
        C = C0_ref[0, h]
        n_row = n0_ref[0, h:h + 1, :]
        inter = lf + m
        mt = jnp.maximum(inter, ig)
        s = jnp.sum(q * k, axis=1, keepdims=True) * jnp.exp(ig - mt)
        winter = jnp.exp(inter - mt)
        qC = _dot(jnp.broadcast_to(q, (8, DK_A)), C, hi=True)[0:1]
        num = s * v + winter * qC
        den = s + winter * jnp.sum(q * n_row, axis=1, keepdims=True)
        hh = num / jnp.maximum(jnp.abs(den), jnp.exp(-mt))
        h_ref[0, :, h * DV_A:(h + 1) * DV_A] = _sigmoid(og) * hh
        m_new = jnp.maximum(inter, ig)
        ws = jnp.exp(ig - m_new)
        wc = jnp.exp(inter - m_new)
        C_ref[0, h] = wc * C + (ws * _col_of_row(k)) * v
        n_ref[0, h:h + 1, :] = wc * n_row + ws * k
        m_out = jnp.where(lane == h, m_new, m_out)
    m_ref[0] = m_out


def mlstm_step(a, g, bg, C0, n0, m0):
    B = a.shape[0]
    r3 = lambda w: pl.BlockSpec((1, 1, w), lambda b: (b, 0, 0))
    cs = pl.BlockSpec((1, H_A, DK_A, DV_A), lambda b: (b, 0, 0, 0))
    ns = pl.BlockSpec((1, H_A, DK_A), lambda b: (b, 0, 0))
    return pl.pallas_call(
        _mlstm_step_body,
        grid=(B,),
        in_specs=[r3(a.shape[2]), r3(LANES), pl.BlockSpec((1, LANES), lambda b: (0, 0)), cs, ns, r3(LANES)],
        out_specs=[r3(H_A * DV_A), cs, ns, r3(LANES)],
        out_shape=[jax.ShapeDtypeStruct((B, 1, H_A * DV_A), F32),
                   jax.ShapeDtypeStruct(C0.shape, F32), jax.ShapeDtypeStruct(n0.shape, F32),
                   jax.ShapeDtypeStruct((B, 1, LANES), F32)],
        compiler_params=_params(("parallel",)),
        name="mlstm_step",
    )(a, g, bg, C0, n0, m0)


NPAIR_B = GW_B // LANES
SWA_ROWS = 512


def _swa_body(*refs, dil):
    Q = SWA_BAND
    q_refs, kc_refs, kp_refs, vc_refs, vp_refs = [refs[i * NPAIR_B:(i + 1) * NPAIR_B] for i in range(5)]
    o_refs = refs[5 * NPAIR_B:6 * NPAIR_B]
    l_refs = refs[6 * NPAIR_B:7 * NPAIR_B]
    qi = lax.broadcasted_iota(jnp.int32, (Q, Q), 0)
    kj = lax.broadcasted_iota(jnp.int32, (Q, Q), 1)
    mask_c = kj <= qi
    mask_in = kj >= qi
    mask_first = jnp.logical_and(mask_in, pl.program_id(1) > 0)
    lane = lax.broadcasted_iota(jnp.int32, (Q, LANES), 1)
    first = lane < DH_B
    chains = [(p, hh) for p in range(NPAIR_B) for hh in range(2)]
    nbs = q_refs[0].shape[1] // (Q * dil)
    for j, r in [(j, r) for j in range(nbs) for r in range(dil)]:
        at = lambda start: (pl.ds(0, 1), pl.ds(start, Q, stride=dil) if dil > 1 else pl.ds(start, Q), slice(None))
        rows = at(j * Q * dil + r)
        ld = lambda ref, rw=rows: ref[rw][0]
        q = [ld(ref) * (DH_B ** -0.5) for ref in q_refs]
        kc = [ld(ref).astype(BF16) for ref in kc_refs]
        vc = [ld(ref).astype(BF16) for ref in vc_refs]
        if j == 0:
            prows, mask_p = at((nbs - 1) * Q * dil + r), mask_first
            kp = [ld(ref, prows).astype(BF16) for ref in kp_refs]
            vp = [ld(ref, prows).astype(BF16) for ref in vp_refs]
        else:
            prows, mask_p = at((j - 1) * Q * dil + r), mask_in
            kp = [ld(ref, prows).astype(BF16) for ref in kc_refs]
            vp = [ld(ref, prows).astype(BF16) for ref in vc_refs]
        qm = [jnp.where(first if hh == 0 else ~first, q[p], 0.0).astype(BF16) for p, hh in chains]
        sc = [_dot_nt(qm[c], kc[p]) for c, (p, hh) in enumerate(chains)]
        sp = [_dot_nt(qm[c], kp[p]) for c, (p, hh) in enumerate(chains)]
        pcs, pps, dens, lses = [], [], [], []
        for c in range(len(chains)):
            s_c = jnp.where(mask_c, sc[c], NEG)
            s_p = jnp.where(mask_p, sp[c], NEG)
            mx = jnp.maximum(jnp.max(s_c, axis=1, keepdims=True), jnp.max(s_p, axis=1, keepdims=True))
            pc = jnp.exp(s_c - mx)
            pp = jnp.exp(s_p - mx)
            den = jnp.sum(pc, axis=1, keepdims=True) + jnp.sum(pp, axis=1, keepdims=True)
            pcs.append(pc.astype(BF16))
            pps.append(pp.astype(BF16))
            dens.append(den)
            lses.append(mx + jnp.log(den))
        os_ = [(_dot(pcs[c], vc[p]) + _dot(pps[c], vp[p])) / dens[c] for c, (p, hh) in enumerate(chains)]
        for p in range(NPAIR_B):
            o_refs[p][rows] = jnp.where(first, os_[2 * p], os_[2 * p + 1])[None]
            l_refs[p][rows] = jnp.where(first, lses[2 * p], lses[2 * p + 1])[None]


def swa_prompt(qkv, g, dil):
    B, T, W = qkv.shape
    ng = W // GW_B // 3
    unit = max(SWA_BAND * dil, min(SWA_ROWS, T))
    col = lambda part, p: (part * ng + g) * NPAIR_B + p
    cur = lambda part: [pl.BlockSpec((1, unit, LANES), functools.partial(lambda b, n, c: (b, n, c), c=col(part, p)))
                        for p in range(NPAIR_B)]
    prev = lambda part: [pl.BlockSpec((1, unit, LANES),
                                      functools.partial(lambda b, n, c: (b, jnp.maximum(n - 1, 0), c), c=col(part, p)))
                         for p in range(NPAIR_B)]
    ospec = pl.BlockSpec((1, unit, LANES), lambda b, n: (b, n, 0))
    res = pl.pallas_call(
        functools.partial(_swa_body, dil=dil),
        grid=(B, T // unit),
        in_specs=cur(0) + cur(1) + prev(1) + cur(2) + prev(2),
        out_specs=[ospec] * (2 * NPAIR_B),
        out_shape=[jax.ShapeDtypeStruct((B, T, LANES), F32)] * (2 * NPAIR_B),
        compiler_params=_params(("parallel", "parallel"), big=True),
        name="swa_prompt_d%d" % dil,
    )(*([qkv] * (5 * NPAIR_B)))
    return res[:NPAIR_B], res[NPAIR_B:]


def _swa_step_body(q_ref, kvn_ref, buf_ref, o_ref, l_ref, nb_ref, *, dil):
    wb = buf_ref.shape[-1]
    lane = lax.broadcasted_iota(jnp.int32, (1, wb), 1)
    read = (lane % dil) == 0
    last = lane == wb - 1
    for h in range(H_B):
        q = q_ref[0, h]
        kn = kvn_ref[0, 0, h]
        vn = kvn_ref[0, 1, h]
        kt = buf_ref[0, 0, h]
        vt = buf_ref[0, 1, h]
        s = jnp.where(read, jnp.sum(kt * q, axis=0, keepdims=True) * (DH_B ** -0.5), NEG)
        s_self = jnp.sum(kn * q, axis=0, keepdims=True) * (DH_B ** -0.5)
        mx = jnp.maximum(jnp.max(s, axis=1, keepdims=True), s_self)
        p = jnp.exp(s - mx)
        p_self = jnp.exp(s_self - mx)
        den = jnp.sum(p, axis=1, keepdims=True) + p_self
        o_ref[0, h] = (jnp.sum(vt * p, axis=1, keepdims=True) + p_self * vn) / den
        l_ref[0, h] = jnp.broadcast_to(mx + jnp.log(den), (DH_B, 1))
        nb_ref[0, 0, h] = jnp.where(last, kn, pltpu.roll(kt, wb - 1, axis=1))
        nb_ref[0, 1, h] = jnp.where(last, vn, pltpu.roll(vt, wb - 1, axis=1))


def swa_step(q, kvn, buf_t, dil):
    B = q.shape[0]
    qs = pl.BlockSpec((1,) + q.shape[1:], lambda b: (b, 0, 0, 0))
    ks = pl.BlockSpec((1,) + kvn.shape[1:], lambda b: (b, 0, 0, 0, 0))
    bs = pl.BlockSpec((1,) + buf_t.shape[1:], lambda b: (b, 0, 0, 0, 0))
    return pl.pallas_call(
        functools.partial(_swa_step_body, dil=dil),
        grid=(B,),
        in_specs=[qs, ks, bs],
        out_specs=[qs, qs, bs],
        out_shape=[jax.ShapeDtypeStruct(q.shape, F32)] * 2 + [jax.ShapeDtypeStruct(buf_t.shape, F32)],
        compiler_params=_params(("parallel",), big=True),
        name="swa_step_d%d" % dil,
    )(q, kvn, buf_t)


AUG = 2 * DH_C


VROWS = DH_C + 16
TQ_C = 128
HEADS_C = 4
LOG2E = 1.4426950408889634


def _moba_kprep_body(kt_ref, vt_ref, km_ref, ka_ref, vta_ref, *, nblk):
    n = pl.program_id(1)
    k = kt_ref[0].T
    km_ref[0, pl.ds(n, 1), :] = jnp.mean(k, axis=0, keepdims=True)
    lane = lax.broadcasted_iota(jnp.int32, (MOBA_BLOCK, DH_C), 1)
    onehot = jnp.where(lane == n, 1.0, 0.0).astype(BF16)
    pieces = []
    for h in range(H_C):
        pieces += [k[:, h * DH_C:(h + 1) * DH_C].astype(BF16), onehot]
    ka_ref[0] = jnp.concatenate(pieces, axis=1)
    vt = vt_ref[0]
    ones = jnp.where(lax.broadcasted_iota(jnp.int32, (VROWS - DH_C, MOBA_BLOCK), 0) == 0, 1.0, 0.0).astype(BF16)
    pieces = []
    for h in range(H_C):
        pieces += [vt[h * DH_C:(h + 1) * DH_C].astype(BF16), ones]
    vta_ref[0] = jnp.concatenate(pieces, axis=0)


def _moba_gate_body(qt_ref, km_ref, qa_ref, *, nblk):
    own = pl.program_id(1)
    blk = lax.broadcasted_iota(jnp.int32, (nblk, MOBA_BLOCK), 0)
    blk_f = blk.astype(F32)
    past = blk < own
    gs = [_dot(km_ref[0, :, h * DH_C:(h + 1) * DH_C], qt_ref[0, h * DH_C:(h + 1) * DH_C, :], hi=True)
          for h in range(H_C)]
    for h in range(H_C):
        g = jnp.where(past, gs[h], NEG)
        sel = blk == own
        for _ in range(MOBA_TOPK):
            mx = jnp.max(g, axis=0, keepdims=True)
            first = jnp.min(jnp.where(g == mx, blk_f, float(nblk)), axis=0, keepdims=True)
            pick = blk_f == first
            sel = jnp.logical_or(sel, jnp.logical_and(pick, past))
            g = jnp.where(pick, -jnp.inf, g)
        bias = jnp.where(sel, 0.0, NEG)
        aug = jnp.concatenate([qt_ref[0, h * DH_C:(h + 1) * DH_C, :] * (DH_C ** -0.5 * LOG2E), bias,
                               jnp.zeros((AUG - DH_C - nblk, MOBA_BLOCK), F32)], axis=0)
        qa_ref[0, h * AUG:(h + 1) * AUG, :] = aug.astype(BF16)


def _moba_attn_body(qt_ref, ka_ref, vta_ref, o_ref, sa_scr, sb_scr):
    own = pl.program_id(2)
    TK = MOBA_BLOCK
    nsub = TK // TQ_C
    chains = [(hh, qs) for hh in range(HEADS_C) for qs in range(nsub)]
    qts = [qt_ref[0, hh * AUG:(hh + 1) * AUG, qs * TQ_C:(qs + 1) * TQ_C] for hh, qs in chains]

    def block(n, hh):
        r0 = pl.multiple_of(n * TK, TK)
        return (ka_ref[0, pl.ds(r0, TK), hh * AUG:(hh + 1) * AUG],
                vta_ref[0, hh * VROWS:(hh + 1) * VROWS, pl.ds(r0, TK)])

    key = lax.broadcasted_iota(jnp.int32, (TK, TQ_C), 0)
    qry = lax.broadcasted_iota(jnp.int32, (TK, TQ_C), 1)

    def issue_scores(n, s_scr):
        ks = [block(n, hh)[0] for hh in range(HEADS_C)]
        for c, (hh, qs) in enumerate(chains):
            s_scr[c] = _dot(ks[hh], qts[c])

    def absorb(n, s_scr, ms_prev, accs, diagonal):
        vs = [block(n, hh)[1] for hh in range(HEADS_C)]
        ms, ps, alphas = [], [], []
        for c, (hh, qs) in enumerate(chains):
            s = s_scr[c]
            if diagonal:
                s = jnp.where(key <= qry + qs * TQ_C, s, NEG)
            mx = jnp.max(s, axis=0, keepdims=True)
            m_new = mx if ms_prev is None else jnp.maximum(ms_prev[c], mx)
            ms.append(m_new)
            ps.append(jnp.exp2(s - m_new).astype(BF16))
            alphas.append(None if ms_prev is None else jnp.exp2(ms_prev[c] - m_new))
        pv = [_dot(vs[hh], ps[c]) for c, (hh, qs) in enumerate(chains)]
        return tuple(ms), tuple(pv[c] if accs is None else alphas[c] * accs[c] + pv[c] for c in range(len(chains)))

    last = jnp.maximum(own - 1, 0)
    issue_scores(own, sa_scr)
    issue_scores(0, sb_scr)
    state = absorb(own, sa_scr, None, None, True)

    def body(i, state):
        issue_scores(2 * i + 1, sa_scr)
        state = absorb(2 * i, sb_scr, state[0], state[1], False)
        issue_scores(jnp.minimum(2 * i + 2, last), sb_scr)
        return absorb(2 * i + 1, sa_scr, state[0], state[1], False)

    state = lax.fori_loop(0, own // 2, body, state)
    ms, accs = lax.cond(own % 2 == 1, lambda st: absorb(last, sb_scr, st[0], st[1], False), lambda st: st, state)
    for qs in range(nsub):
        sub = [accs[c] for c, (hh, q2) in enumerate(chains) if q2 == qs]
        o = jnp.concatenate([a[:DH_C] / a[DH_C:DH_C + 1] for a in sub], axis=0)
        o_ref[0, qs * TQ_C:(qs + 1) * TQ_C, :] = o.T.astype(o_ref.dtype)


def moba_prompt(q_t, kvt):
    B, W, T = q_t.shape
    nblk = T // MOBA_BLOCK
    assert nblk <= AUG - DH_C and nblk % 8 == 0
    km, ka, vta = pl.pallas_call(
        functools.partial(_moba_kprep_body, nblk=nblk),
        grid=(B, nblk),
        in_specs=[pl.BlockSpec((1, W, MOBA_BLOCK), lambda b, n: (b, 0, n)),
                  pl.BlockSpec((1, W, MOBA_BLOCK), lambda b, n: (b, 1, n))],
        out_specs=[pl.BlockSpec((1, nblk, W), lambda b, n: (b, 0, 0)),
                   pl.BlockSpec((1, MOBA_BLOCK, H_C * AUG), lambda b, n: (b, n, 0)),
                   pl.BlockSpec((1, H_C * VROWS, MOBA_BLOCK), lambda b, n: (b, 0, n))],
        out_shape=[jax.ShapeDtypeStruct((B, nblk, W), F32),
                   jax.ShapeDtypeStruct((B, T, H_C * AUG), BF16),
                   jax.ShapeDtypeStruct((B, H_C * VROWS, T), BF16)],
        compiler_params=_params(("parallel", "arbitrary")),
        name="moba_kprep",
    )(kvt, kvt)
    qt = pl.pallas_call(
        functools.partial(_moba_gate_body, nblk=nblk),
        grid=(B, nblk),
        in_specs=[pl.BlockSpec((1, W, MOBA_BLOCK), lambda b, i: (b, 0, i)),
                  pl.BlockSpec((1, nblk, W), lambda b, i: (b, 0, 0))],
        out_specs=pl.BlockSpec((1, H_C * AUG, MOBA_BLOCK), lambda b, i: (b, 0, i)),
        out_shape=jax.ShapeDtypeStruct((B, H_C * AUG, T), BF16),
        compiler_params=_params(("parallel", "parallel")),
        name="moba_gate",
    )(q_t, km)
    return pl.pallas_call(
        _moba_attn_body,
        grid=(B, H_C // HEADS_C, nblk),
        in_specs=[pl.BlockSpec((1, HEADS_C * AUG, MOBA_BLOCK), lambda b, p, j: (b, p, j)),
                  pl.BlockSpec((1, T, HEADS_C * AUG), lambda b, p, j: (b, 0, p)),
                  pl.BlockSpec((1, HEADS_C * VROWS, T), lambda b, p, j: (b, p, 0))],
        out_specs=pl.BlockSpec((1, MOBA_BLOCK, HEADS_C * DH_C), lambda b, p, j: (b, j, p)),
        out_shape=jax.ShapeDtypeStruct((B, T, W), BF16),
        scratch_shapes=[pltpu.VMEM((HEADS_C * (MOBA_BLOCK // TQ_C), MOBA_BLOCK, TQ_C), F32)] * 2,
        compiler_params=_params(("parallel", "parallel", "arbitrary"), big=True),
        name="moba_attn",
    )(qt, ka, vta)


PAGES_PER_STEP = 32
HEADS_STEP_C = 4


def _moba_step_gate_body(pt_ref, q_ref, *refs, pages_per_block, n_past):
    page_refs = refs[:PAGES_PER_STEP]
    sel_ref, g_scr = refs[PAGES_PER_STEP], refs[PAGES_PER_STEP + 1]
    s = pl.program_id(1)
    q = q_ref[0]
    W = q.shape[0]
    rows = page_refs[0].shape[-1]
    per_step = PAGES_PER_STEP // pages_per_block
    lane = lax.broadcasted_iota(jnp.int32, (W, LANES), 1)

    @pl.when(s == 0)
    def _():
        g_scr[...] = jnp.zeros_like(g_scr)

    g = g_scr[...]
    for j in range(per_step):
        tot = None
        for i in range(pages_per_block):
            t = page_refs[j * pages_per_block + i][0, 0].reshape(W, rows)
            tot = t if tot is None else tot + t
        kmean = jnp.sum(tot, axis=1, keepdims=True) / (pages_per_block * rows)
        g = jnp.where(lane == s * per_step + j, q * kmean, g)
    g_scr[...] = g

    @pl.when(s == pl.num_programs(1) - 1)
    def _():
        seg = (lax.broadcasted_iota(jnp.int32, (8, W), 1) // DH_C ==
               lax.broadcasted_iota(jnp.int32, (8, W), 0)).astype(F32)
        blk = lax.broadcasted_iota(jnp.int32, (8, LANES), 1)
        blk_f = blk.astype(F32)
        gate = jnp.where(blk < n_past, _dot(seg, g, hi=True), NEG)
        sel = jnp.zeros((8, LANES), F32)
        for r in range(MOBA_TOPK):
            mx = jnp.max(gate, axis=1, keepdims=True)
            first = jnp.min(jnp.where(gate == mx, blk_f, float(LANES)), axis=1, keepdims=True)
            sel = jnp.where(blk == r, first, sel)
            gate = jnp.where(blk_f == first, -jnp.inf, gate)
        sel_ref[0] = sel.astype(jnp.int32)


def _moba_step_attn_body(pt_ref, sel_ref, q_ref, kn_ref, vn_ref, *refs, n_pages):
    nh = HEADS_STEP_C
    o_ref = refs[2 * nh * n_pages]
    scale = DH_C ** -0.5
    for hh in range(nh):
        k_refs = refs[hh * n_pages:(hh + 1) * n_pages]
        v_refs = refs[(nh + hh) * n_pages:(nh + hh + 1) * n_pages]
        q = q_ref[0, hh]
        kn = kn_ref[0, 0, hh]
        vn = vn_ref[0, 0, hh]
        ss = [jnp.sum(k_ref[0, 0, 0] * q, axis=0, keepdims=True) * scale for k_ref in k_refs]
        s_self = jnp.sum(kn * q, axis=0, keepdims=True) * scale
        mx = s_self
        for sj in ss:
            mx = jnp.maximum(mx, jnp.max(sj, axis=1, keepdims=True))
        p_self = jnp.exp(s_self - mx)
        den = p_self
        o = p_self * vn
        for sj, v_ref in zip(ss, v_refs):
            p = jnp.exp(sj - mx)
            den = den + jnp.sum(p, axis=1, keepdims=True)
            o = o + jnp.sum(v_ref[0, 0, 0] * p, axis=1, keepdims=True)
        o_ref[0, hh] = o / den


def moba_step(q, kv_new, pool_t, page_table):
    B, W, _ = q.shape
    page = pool_t.shape[-1]
    n_pt = page_table.shape[1]
    ppb = MOBA_BLOCK // page
    n_past = n_pt // ppb
    assert MOBA_TOPK <= n_past <= LANES and PAGES_PER_STEP % ppb == 0 and n_pt % PAGES_PER_STEP == 0
    pt_flat = page_table.reshape(-1)
    kpage = (1, 1, H_C, DH_C, page)
    sel = pl.pallas_call(
        functools.partial(_moba_step_gate_body, pages_per_block=ppb, n_past=n_past),
        grid_spec=pltpu.PrefetchScalarGridSpec(
            num_scalar_prefetch=1,
            grid=(B, n_pt // PAGES_PER_STEP),
            in_specs=[pl.BlockSpec((1, W, 1), lambda b, s, pt: (b, 0, 0))] +
                     [pl.BlockSpec(kpage, functools.partial(
                         lambda b, s, pt, i: (pt[b * n_pt + s * PAGES_PER_STEP + i], 0, 0, 0, 0), i=i))
                      for i in range(PAGES_PER_STEP)],
            out_specs=pl.BlockSpec((1, 8, LANES), lambda b, s, pt: (b, 0, 0)),
            scratch_shapes=[pltpu.VMEM((W, LANES), F32)]),
        out_shape=jax.ShapeDtypeStruct((B, 8, LANES), jnp.int32),
        compiler_params=_params(("parallel", "arbitrary"), big=True),
        name="moba_step_gate",
    )(pt_flat, q, *([pool_t] * PAGES_PER_STEP))
    sel_flat = sel[:, :H_C, :MOBA_TOPK].reshape(-1)
    n_pages = MOBA_TOPK * ppb

    nh = HEADS_STEP_C

    def page_map(b, hq, pt, sl, hh, r, i, kv):
        h = hq * nh + hh
        blk = sl[(b * H_C + h) * MOBA_TOPK + r]
        return (pt[b * n_pt + blk * ppb + i], kv, h, 0, 0)

    def page_specs(kv):
        return [pl.BlockSpec((1, 1, 1, DH_C, page), functools.partial(page_map, hh=hh, r=r, i=i, kv=kv))
                for hh in range(nh) for r in range(MOBA_TOPK) for i in range(ppb)]

    q4 = q.reshape(B, H_C, DH_C, 1)
    head = pl.BlockSpec((1, nh, DH_C, 1), lambda b, hq, pt, sl: (b, hq, 0, 0))
    new = lambda kv: pl.BlockSpec((1, 1, nh, DH_C, 1), lambda b, hq, pt, sl: (b, kv, hq, 0, 0))
    return pl.pallas_call(
        functools.partial(_moba_step_attn_body, n_pages=n_pages),
        grid_spec=pltpu.PrefetchScalarGridSpec(
            num_scalar_prefetch=2,
            grid=(B, H_C // nh),
            in_specs=[head, new(0), new(1)] + page_specs(0) + page_specs(1),
            out_specs=head),
        out_shape=jax.ShapeDtypeStruct(q4.shape, F32),
        compiler_params=_params(("parallel", "parallel")),
        name="moba_step_attn",
    )(pt_flat, sel_flat, q4, kv_new, kv_new, *([pool_t] * (2 * nh * n_pages)))


def _hgrn_gates(f, wlb_ref, layer):
    w = wlb_ref[...]
    e = jnp.exp(w - jnp.max(w, axis=0, keepdims=True))
    sm = e / jnp.sum(e, axis=0, keepdims=True)
    lb = jnp.sum(sm[1:layer + 1], axis=0, keepdims=True)
    sg = _sigmoid(f)
    return jnp.log(lb + (1.0 - lb) * sg), (1.0 - lb) * (1.0 - sg)


def _hgrn_body(d_ref, wlb_ref, go_ref, o_ref, S_ref, *, layer):
    L, SB = CHUNK_D, SUB_D
    HD = H_D * DK_D

    @pl.when(pl.program_id(0) == 0)
    def _():
        S_ref[...] = jnp.zeros_like(S_ref)

    nb = d_ref.shape[0]
    ri = lax.broadcasted_iota(jnp.int32, (L, L), 0)
    ci = lax.broadcasted_iota(jnp.int32, (L, L), 1)
    tril = (ci <= ri).astype(F32)
    sub_row = lax.broadcasted_iota(jnp.int32, (SB, 1), 0)
    gates = [_hgrn_gates(d_ref[b, :, HD:2 * HD], wlb_ref, layer) for b in range(nb)]
    b_all = [_dot(tril, lf, hi=True) for lf, _ in gates]
    chains = [(b, h) for b in range(nb) for h in range(H_D)]
    nch = range(len(chains))
    blocks = range(SB, L, SB)
    q = [d_ref[b, :, h * DK_D:(h + 1) * DK_D] for b, h in chains]
    v = [d_ref[b, :, 2 * HD + h * DV_D:2 * HD + (h + 1) * DV_D] for b, h in chains]
    bc = [b_all[b][:, h * DK_D:(h + 1) * DK_D] for b, h in chains]
    k = [gates[b][1][:, h * DK_D:(h + 1) * DK_D] for b, h in chains]
    S = [S_ref[b, h] for b, h in chains]
    vb = [x.astype(BF16) for x in v]
    o_inter = [_dot((q[c] * jnp.exp(bc[c])).astype(BF16), S[c].astype(BF16)) for c in nch]
    rnd = lambda x: x.astype(BF16).astype(F32)
    vr = [rnd(x) for x in v]
    a_off = []
    for c in nch:
        for r0 in blocks:
            ref_b = bc[c][r0 - 1:r0]
            qs = rnd(q[c][r0:r0 + SB] * jnp.exp(bc[c][r0:r0 + SB] - ref_b))
            ks = rnd(k[c][:r0] * jnp.exp(ref_b - bc[c][:r0]))
            a_off.append(rnd(_dot_nt(qs, ks)))
    kw = [(k[c] * jnp.exp(bc[c][L - 1:L] - bc[c])).astype(BF16) for c in nch]
    kv = [_dot_tn(kw[c], vb[c]) for c in nch]
    o_off = [_dot(a_off[c * len(blocks) + i], vr[c][:r0]) for c in nch for i, r0 in enumerate(blocks)]
    for c, (b, h) in enumerate(chains):
        b2 = bc[c] * LOG2E
        o_blocks = []
        for i, r0 in enumerate(range(0, L, SB)):
            qI = q[c][r0:r0 + SB]
            bI = b2[r0:r0 + SB]
            oI = o_inter[c][r0:r0 + SB]
            if r0 > 0:
                oI = oI + o_off[c * len(blocks) + i - 1]
            for s in range(SB):
                row = r0 + s
                a = jnp.sum(qI * k[c][row:row + 1] * jnp.exp2(bI - b2[row:row + 1]), axis=1, keepdims=True)
                oI = oI + jnp.where(sub_row >= s, a, 0.0) * v[c][row:row + 1]
            o_blocks.append(oI)
        o = jnp.concatenate(o_blocks, axis=0)
        gd = d_ref[b, :, 3 * HD + h * DV_D:3 * HD + (h + 1) * DV_D]
        o_ref[b, :, h * DV_D:(h + 1) * DV_D] = (_rms(o, go_ref[...]) * _silu(gd)).astype(o_ref.dtype)
        S_ref[b, h] = _col_of_row(jnp.exp(bc[c][L - 1:L])) * S[c] + kv[c]


def hgrn_prompt(d, w_lb, g_out, layer):
    B, T, W = d.shape
    L = CHUNK_D
    HD = H_D * DK_D
    return pl.pallas_call(
        functools.partial(_hgrn_body, layer=layer),
        grid=(T // L,),
        in_specs=[pl.BlockSpec((B, L, W), lambda c: (0, c, 0)),
                  pl.BlockSpec(w_lb.shape, lambda c: (0, 0)),
                  pl.BlockSpec((1, DV_D), lambda c: (0, 0))],
        out_specs=[pl.BlockSpec((B, L, H_D * DV_D), lambda c: (0, c, 0)),
                   pl.BlockSpec((B, H_D, DK_D, DV_D), lambda c: (0, 0, 0, 0))],
        out_shape=[jax.ShapeDtypeStruct((B, T, H_D * DV_D), BF16),
                   jax.ShapeDtypeStruct((B, H_D, DK_D, DV_D), F32)],
        compiler_params=_params(("arbitrary",), big=True),
        name="hgrn_prompt",
    )(d, w_lb, g_out.reshape(1, DV_D))


def _hgrn_step_body(d_ref, wlb_ref, go_ref, S0_ref, o_ref, S_ref, *, layer):
    HD = H_D * DK_D
    lf, kd = _hgrn_gates(d_ref[0, :, HD:2 * HD], wlb_ref, layer)
    eb = jnp.exp(lf)
    for h in range(H_D):
        sl = slice(h * DK_D, (h + 1) * DK_D)
        q = d_ref[0, :, sl]
        v = d_ref[0, :, 2 * HD + h * DV_D:2 * HD + (h + 1) * DV_D]
        gd = d_ref[0, :, 3 * HD + h * DV_D:3 * HD + (h + 1) * DV_D]
        k = kd[:, sl]
        e = eb[:, sl]
        S = S0_ref[0, h]
        a = jnp.sum(q * k, axis=1, keepdims=True)
        o = a * v + _dot(jnp.broadcast_to(q * e, (8, DK_D)), S, hi=True)[0:1]
        o_ref[0, :, h * DV_D:(h + 1) * DV_D] = _rms(o, go_ref[...]) * _silu(gd)
        S_ref[0, h] = _col_of_row(e) * S + _col_of_row(k) * v


def hgrn_step(d, w_lb, g_out, S0, layer):
    B, _, W = d.shape
    ss = pl.BlockSpec((1, H_D, DK_D, DV_D), lambda b: (b, 0, 0, 0))
    return pl.pallas_call(
        functools.partial(_hgrn_step_body, layer=layer),
        grid=(B,),
        in_specs=[pl.BlockSpec((1, 1, W), lambda b: (b, 0, 0)),
                  pl.BlockSpec(w_lb.shape, lambda b: (0, 0)),
                  pl.BlockSpec((1, DV_D), lambda b: (0, 0)), ss],
        out_specs=[pl.BlockSpec((1, 1, H_D * DV_D), lambda b: (b, 0, 0)), ss],
        out_shape=[jax.ShapeDtypeStruct((B, 1, H_D * DV_D), F32), jax.ShapeDtypeStruct(S0.shape, F32)],
        compiler_params=_params(("parallel",)),
        name="hgrn_step",
    )(d, w_lb, g_out.reshape(1, DV_D), S0)


N_A = 4 * H_A * DK_A
N_B = 3 * len(SWA_GROUPS) * GW_B
N_C = H_C * DH_C
N_D = 4 * H_D * DK_D


def _even_weights(w_in):
    gates = jnp.pad(w_in[:, N_A:N_A + 2 * H_A], ((0, 0), (0, LANES - 2 * H_A)))
    return jnp.concatenate([w_in[:, :N_A], w_in[:, N_A + 2 * H_A:], gates], axis=1)


def _trunk(x, ada, prompt, st_even, st_odd, P):
    hi = not prompt
    G, R, D = x.shape
    wdt = F32 if hi else BF16
    tm_lin = 256
    tm_out = 512
    new_even = new_odd = None
    for l in range(2):
        sh1, sc1, g1, sh2, sc2, g2 = [ada[l][..., i * D:(i + 1) * D] for i in range(6)]
        if l == 0:
            w = _even_weights(P['w_in_even']).astype(wdt)
            splits = (N_A, N_B, LANES)
            if hi:
                outs, off = [], 0
                for n in splits:
                    outs += norm_mod_proj(x, P['g_pre_mix'][l], sc1, sh1, w[:, off:off + n], (n,), hi, tm_lin)
                    off += n
                a, qkv, gg = outs
            else:
                a, qkv, gg = norm_mod_proj(x, P['g_pre_mix'][l], sc1, sh1, w, splits, hi, tm_lin)
            bg = jnp.pad(P['b_gate_a'].reshape(1, 2 * H_A), ((0, 0), (0, LANES - 2 * H_A)))
            if prompt:
                ha, C, n, m = mlstm_prompt(a, gg, bg)
                swa = [swa_prompt(qkv, g, dil) for g, (_, dil) in enumerate(SWA_GROUPS)]
                T = R
                bufs = []
                for g, (win, _) in enumerate(SWA_GROUPS):
                    wb = min(win, T)
                    kb = qkv[:, T - wb:, (3 + g) * GW_B:(4 + g) * GW_B]
                    vb = qkv[:, T - wb:, (6 + g) * GW_B:(7 + g) * GW_B]
                    bufs.append(jnp.concatenate([kb, vb], axis=-1).reshape(G, wb, 2, H_B, DH_B))
                new_even = (C, n, m[:, 0, :H_A], bufs[0], bufs[1], bufs[2])
            else:
                B = R
                C0, n0, m0 = st_even[:3]
                m0p = jnp.pad(m0, ((0, 0), (0, LANES - H_A))).reshape(B, 1, LANES)
                tok = lambda t: t.reshape(B, 1, t.shape[-1])
                ha, C, n, m = mlstm_step(tok(a), tok(gg), bg, C0, n0, m0p)
                swa, bufs = [], []
                ng = len(SWA_GROUPS)
                qkv5 = qkv.reshape(B, 3, ng, H_B, DH_B, 1)
                for g, (_, dil) in enumerate(SWA_GROUPS):
                    buf_t = jnp.transpose(st_even[3 + g], (0, 2, 3, 4, 1))
                    o, lse, nb = swa_step(qkv5[:, 0, g], qkv5[:, 1:, g], buf_t, dil)
                    o, lse = o.reshape(1, B, GW_B), lse.reshape(1, B, GW_B)
                    swa.append(([o[..., p * LANES:(p + 1) * LANES] for p in range(NPAIR_B)],
                                [lse[..., p * LANES:(p + 1) * LANES] for p in range(NPAIR_B)]))
                    bufs.append(jnp.transpose(nb, (0, 4, 1, 2, 3)))
                ha = ha.reshape(1, B, -1)
                new_even = (C, n, m[:, 0, :H_A], bufs[0], bufs[1], bufs[2])
            swa_os = [swa[g][0][p] for p in range(NPAIR_B) for g in range(len(SWA_GROUPS))]
            swa_ls = [swa[g][1][p] for p in range(NPAIR_B) for g in range(len(SWA_GROUPS))]
            x = mix_out(x, g1, P['g_post_mix'][l], [ha], (swa_os, swa_ls), P['w_out_even'].astype(wdt), hi, tm_out)
        else:
            w = P['w_in_odd'].astype(wdt)
            if prompt:
                d, q_t, kvt = norm_mod_proj(x, P['g_pre_mix'][l], sc1, sh1, w[:, 3 * N_C:], (N_D,), hi, tm_lin,
                                            wt=w[:, :3 * N_C].T, t_splits=(N_C, 2 * N_C))
                hc = moba_prompt(q_t, kvt)
                od, S = hgrn_prompt(d, P['w_lb'], P['g_out_d'], l)
                new_odd = (jnp.transpose(kvt.reshape(G, 2, H_C, DH_C, R), (0, 4, 1, 2, 3)), S)
            else:
                B = R
                outs, off = [], 0
                for n in (N_C, 2 * N_C, N_D):
                    outs += norm_mod_proj(x, P['g_pre_mix'][l], sc1, sh1, w[:, off:off + n], (n,), hi, tm_lin)
                    off += n
                qc, kvc, d = outs
                pool, page_table, S0 = st_odd
                hc = moba_step(qc.reshape(B, N_C, 1), kvc.reshape(B, 2, H_C, DH_C, 1),
                               jnp.transpose(pool, (0, 2, 3, 4, 1)), page_table)
                od, S = hgrn_step(d.reshape(B, 1, N_D), P['w_lb'], P['g_out_d'], S0, l)
                hc = hc.reshape(1, B, -1)
                od = od.reshape(1, B, -1)
                new_odd = (kvc.reshape(B, 1, 2, H_C, DH_C), S)
            x = mix_out(x, g1, P['g_post_mix'][l], [hc, od], None, P['w_out_odd'].astype(wdt), hi, tm_out)
        x = ffn(x, P['g_pre_ffn'][l], sc2, sh2, g2, P['g_post_ffn'][l],
                P['w_ffn_gate'][l].astype(wdt), P['w_ffn_up'][l].astype(wdt), P['w_ffn_down'][l].astype(wdt),
                hi, tm=512, tf=256 if hi else P['w_ffn_gate'].shape[-1])
    return x, new_even, new_odd


def kernel(x_prompt, x_sample, state_mlstm_C, state_mlstm_n, state_mlstm_m, cache_swa_w128, cache_swa_w512, cache_swa_w2048, cache_moba_kv, state_hgrn_S, page_table, c_prompt, c_sample, w_ada, b_ada, g_pre_mix, g_post_mix, g_pre_ffn, g_post_ffn, w_in_even, b_gate_a, w_out_even, w_in_odd, w_lb, g_out_d, w_out_odd, w_ffn_gate, w_ffn_up, w_ffn_down):
    P = {'g_pre_mix': g_pre_mix, 'g_post_mix': g_post_mix, 'g_pre_ffn': g_pre_ffn, 'g_post_ffn': g_post_ffn,
         'w_in_even': w_in_even, 'b_gate_a': b_gate_a, 'w_out_even': w_out_even, 'w_in_odd': w_in_odd,
         'w_lb': w_lb, 'g_out_d': g_out_d, 'w_out_odd': w_out_odd, 'w_ffn_gate': w_ffn_gate,
         'w_ffn_up': w_ffn_up, 'w_ffn_down': w_ffn_down}
    Bp = x_prompt.shape[0]
    Bs = x_sample.shape[0]
    D = x_prompt.shape[-1]
    c_all = jnp.concatenate([c_prompt, c_sample], axis=0)
    c_all = jnp.pad(c_all, ((0, -(Bp + Bs) % 8), (0, 0)))
    ada = ada_all(c_all, w_ada, b_ada)[:, :Bp + Bs]
    ada_p = ada[:, :Bp].reshape(ada.shape[0], Bp, 1, 6 * D)
    ada_s = ada[:, Bp:].reshape(ada.shape[0], 1, Bs, 6 * D)
    y_p, ev_p, od_p = _trunk(x_prompt, ada_p, True, None, None, P)
    y_s, ev_s, od_s = _trunk(
        x_sample.reshape(1, Bs, D), ada_s, False,
        (state_mlstm_C, state_mlstm_n, state_mlstm_m, cache_swa_w128, cache_swa_w512, cache_swa_w2048),
        (cache_moba_kv, page_table, state_hgrn_S), P)
    return ((y_p, y_s.reshape(x_sample.shape)) + tuple(ev_p[:3]) + tuple(ev_s[:3]) + tuple(ev_p[3:])
            + tuple(ev_s[3:]) + (od_p[0], od_s[0], od_p[1], od_s[1]))
```

```python
import functools

import jax
import jax.numpy as jnp
from jax import lax
from jax.experimental import pallas as pl
from jax.experimental.pallas import tpu as pltpu

F32 = jnp.float32
BF16 = jnp.bfloat16
HI = lax.Precision.HIGHEST

EPS = 1e-6
NEG = -1e30

H_A, DK_A, DV_A, CHUNK_A = 4, 128, 128, 128
SWA_GROUPS = ((128, 1), (512, 4), (2048, 16))
H_B, DH_B, SWA_BAND = 4, 64, 128
GW_B = H_B * DH_B
H_C, DH_C, MOBA_BLOCK, MOBA_TOPK = 8, 64, 256, 3
H_D, DK_D, DV_D, CHUNK_D, SUB_D = 4, 128, 128, 64, 8

LANES = 128
VMEM_LIMIT = 56 << 20


def _params(sem, big=False):
    return pltpu.CompilerParams(dimension_semantics=sem,
                                vmem_limit_bytes=VMEM_LIMIT if big else None)


def _dot(a, b, hi=False):
    return jnp.dot(a, b, preferred_element_type=F32, precision=HI if hi else None)


def _dot_nt(a, b, hi=False):
    return lax.dot_general(a, b, (((1,), (1,)), ((), ())), preferred_element_type=F32,
                           precision=HI if hi else None)


def _dot_tn(a, b, hi=False):
    return lax.dot_general(a, b, (((0,), (0,)), ((), ())), preferred_element_type=F32,
                           precision=HI if hi else None)


def _rms(x, g):
    return x * lax.rsqrt(jnp.mean(x * x, axis=-1, keepdims=True) + EPS) * g


def _sigmoid(x):
    return 1.0 / (1.0 + jnp.exp(-x))


def _silu(x):
    return x * _sigmoid(x)


def _log_sigmoid(x):
    return jnp.minimum(x, 0.0) - jnp.log(1.0 + jnp.exp(-jnp.abs(x)))


def _col_of_row(row):
    n = row.shape[-1]
    eye = lax.broadcasted_iota(jnp.int32, (n, n), 0) == lax.broadcasted_iota(jnp.int32, (n, n), 1)
    return jnp.sum(jnp.where(eye, row, 0.0), axis=1, keepdims=True)


def _ada_body(c_ref, w_ref, b_ref, o_ref):
    o_ref[0] = _dot(_silu(c_ref[...]), w_ref[0], hi=True) + b_ref[0]


def ada_all(c, w_ada, b_ada, tn=768):
    depth, d, n = w_ada.shape
    r = c.shape[0]
    return pl.pallas_call(
        _ada_body,
        grid=(depth, n // tn),
        in_specs=[pl.BlockSpec((r, d), lambda l, j: (0, 0)),
                  pl.BlockSpec((1, d, tn), lambda l, j: (l, 0, j)),
                  pl.BlockSpec((1, 1, tn), lambda l, j: (l, 0, j))],
        out_specs=pl.BlockSpec((1, r, tn), lambda l, j: (l, 0, j)),
        out_shape=jax.ShapeDtypeStruct((depth, r, n), F32),
        compiler_params=_params(("parallel", "parallel")),
        name="ada",
    )(c, w_ada, b_ada.reshape(depth, 1, n))


def _proj_body(x_ref, g_ref, sc_ref, sh_ref, w_ref, *refs, splits, hi, cw, t_splits):
    h = _rms(x_ref[0], g_ref[...]) * (1.0 + sc_ref[0]) + sh_ref[0]
    hc = h if hi else h.astype(BF16)
    o_refs = refs[1:] if t_splits else refs
    off = 0
    for o_ref, n in zip(o_refs, splits):
        for c0 in range(0, n, cw):
            c1 = min(c0 + cw, n)
            o_ref[0, :, c0:c1] = _dot(hc, w_ref[:, off + c0:off + c1], hi)
        off += n
    if t_splits:
        wt_ref, off = refs[0], 0
        for ot_ref, n in zip(refs[1 + len(splits):], t_splits):
            for r0 in range(0, n, cw):
                ot_ref[0, r0:r0 + cw, :] = _dot_nt(wt_ref[off + r0:off + r0 + cw, :], hc, hi)
            off += n


def _mod_spec(mod, tm):
    if mod.shape[1] == 1:
        return pl.BlockSpec((1, 1, mod.shape[2]), lambda gi, ri: (gi, 0, 0))
    return pl.BlockSpec((1, tm, mod.shape[2]), lambda gi, ri: (gi, ri, 0))


def norm_mod_proj(x, g, sc, sh, w, splits, hi, tm, wt=None, t_splits=()):
    G, R, D = x.shape
    tm = min(tm, R)
    n_all = sum(splits)
    n_t = sum(t_splits)
    ops = [x, g.reshape(1, D), sc, sh, w]
    in_specs = [pl.BlockSpec((1, tm, D), lambda gi, ri: (gi, ri, 0)),
                pl.BlockSpec((1, D), lambda gi, ri: (0, 0)),
                _mod_spec(sc, tm), _mod_spec(sh, tm),
                pl.BlockSpec((D, n_all), lambda gi, ri: (0, 0), pipeline_mode=pl.Buffered(1))]
    out_specs = [pl.BlockSpec((1, tm, n), lambda gi, ri: (gi, ri, 0)) for n in splits]
    out_shape = [jax.ShapeDtypeStruct((G, R, n), F32) for n in splits]
    if n_t:
        ops.append(wt)
        in_specs.append(pl.BlockSpec((n_t, D), lambda gi, ri: (0, 0), pipeline_mode=pl.Buffered(1)))
        out_specs += [pl.BlockSpec((1, n, tm), lambda gi, ri: (gi, 0, ri)) for n in t_splits]
        out_shape += [jax.ShapeDtypeStruct((G, n, R), F32) for n in t_splits]
    return pl.pallas_call(
        functools.partial(_proj_body, splits=tuple(splits), hi=hi, cw=512, t_splits=tuple(t_splits)),
        grid=(G, R // tm),
        in_specs=in_specs,
        out_specs=out_specs,
        out_shape=out_shape,
        compiler_params=_params(("parallel", "parallel"), big=True),
        name="proj",
    )(*ops)


FFN_CHUNK = 512


def _tail_body(*refs, n_parts, swa, hi):
    x_ref, g1_ref, gpm_ref = refs[:3]
    part_refs = refs[3:3 + n_parts]
    pos = 3 + n_parts
    ng = len(SWA_GROUPS)
    o_refs = refs[pos:pos + swa * ng]
    l_refs = refs[pos + swa * ng:pos + 2 * swa * ng]
    pos += 2 * swa * ng
    (wo_ref, gpre_ref, sc_ref, sh_ref, g2_ref, gpf_ref, wg_ref, wu_ref, wd_ref, out_ref,
     x1_scr, h_scr, acc_scr) = refs[pos:]
    k = pl.program_id(2)

    def cast(a):
        return a if hi else a.astype(BF16)

    @pl.when(k == 0)
    def _():
        acc = None
        off = 0
        for p_ref in part_refs:
            a = p_ref[0]
            n = a.shape[-1]
            t = _dot(cast(a), wo_ref[off:off + n, :], hi)
            acc = t if acc is None else acc + t
            off += n
        for c in range(swa):
            ls = [l_ref[0] for l_ref in l_refs[c * ng:(c + 1) * ng]]
            os_ = [o_ref[0] for o_ref in o_refs[c * ng:(c + 1) * ng]]
            mx = functools.reduce(jnp.maximum, ls)
            es = [jnp.exp(l - mx) for l in ls]
            hb = sum(e * o for e, o in zip(es, os_)) / sum(es)
            acc = acc + _dot(cast(hb), wo_ref[off:off + hb.shape[-1], :], hi)
            off += hb.shape[-1]
        x1 = x_ref[0] + g1_ref[0] * _rms(acc, gpm_ref[...])
        x1_scr[...] = x1
        h = _rms(x1, gpre_ref[...]) * (1.0 + sc_ref[0]) + sh_ref[0]
        h_scr[...] = h.astype(h_scr.dtype)
        acc_scr[...] = jnp.zeros_like(acc_scr)

    h = h_scr[...]
    tf = wg_ref.shape[1]
    cuts = list(range(0, tf, FFN_CHUNK)) + [tf]
    chunks = list(zip(cuts[:-1], cuts[1:]))
    gu = [(_dot(h, wg_ref[:, c0:c1], hi), _dot(h, wu_ref[:, c0:c1], hi)) for c0, c1 in chunks[:1]]
    acc = None
    for i, (c0, c1) in enumerate(chunks):
        if i + 1 < len(chunks):
            n0, n1 = chunks[i + 1]
            gu.append((_dot(h, wg_ref[:, n0:n1], hi), _dot(h, wu_ref[:, n0:n1], hi)))
        a = _silu(gu[i][0]) * gu[i][1]
        d = _dot(a.astype(h.dtype), wd_ref[c0:c1, :], hi)
        acc = d if acc is None else acc + d
    acc_scr[...] += acc

    @pl.when(k == pl.num_programs(2) - 1)
    def _():
        out_ref[0] = x1_scr[...] + g2_ref[0] * _rms(acc_scr[...], gpf_ref[...])


def layer_tail(x, g1, gpost_mix, parts, swa_parts, w_out, gpre, sc, sh, g2, gpost_ffn, wg, wu, wd, layer, hi, tm, tf):
    G, R, D = x.shape
    FF = wg.shape[2]
    tm = min(tm, R)
    row = lambda a: pl.BlockSpec((1, tm, a.shape[2]), lambda gi, ri, k: (gi, ri, 0))
    mspec = lambda m: (pl.BlockSpec((1, 1, D), lambda gi, ri, k: (gi, 0, 0)) if m.shape[1] == 1 else row(m))
    vec = pl.BlockSpec((1, D), lambda gi, ri, k: (0, 0))
    one = dict(pipeline_mode=pl.Buffered(1))
    wmode = one if tf == FF else {}
    ops = [x, g1, gpost_mix.reshape(1, D)] + list(parts)
    specs = [row(x), mspec(g1), vec] + [row(p) for p in parts]
    swa = 0
    if swa_parts is not None:
        swa = len(swa_parts[0]) // len(SWA_GROUPS)
        ops += list(swa_parts[0]) + list(swa_parts[1])
        specs += [row(a) for a in ops[-2 * len(swa_parts[0]):]]
    ops += [w_out, gpre.reshape(1, D), sc, sh, g2, gpost_ffn.reshape(1, D), wg, wu, wd]
    specs += [pl.BlockSpec(w_out.shape, lambda gi, ri, k: (0, 0), **one), vec, mspec(sc), mspec(sh), mspec(g2), vec,
              pl.BlockSpec((None, D, tf), lambda gi, ri, k: (layer, 0, k), **wmode),
              pl.BlockSpec((None, D, tf), lambda gi, ri, k: (layer, 0, k), **wmode),
              pl.BlockSpec((None, tf, D), lambda gi, ri, k: (layer, k, 0), **wmode)]
    return pl.pallas_call(
        functools.partial(_tail_body, n_parts=len(parts), swa=swa, hi=hi),
        grid=(G, R // tm, FF // tf),
        in_specs=specs,
        out_specs=row(x),
        out_shape=jax.ShapeDtypeStruct(x.shape, F32),
        scratch_shapes=[pltpu.VMEM((tm, D), F32), pltpu.VMEM((tm, D), F32 if hi else BF16), pltpu.VMEM((tm, D), F32)],
        compiler_params=_params(("parallel", "parallel", "arbitrary"), big=True),
        name="layer_tail",
    )(*ops)


def _mlstm_body(a_ref, g_ref, bg_ref, h_ref, C_ref, n_ref, m_ref):
    L = CHUNK_A
    HD = H_A * DK_A

    @pl.when(pl.program_id(0) == 0)
    def _():
        C_ref[...] = jnp.zeros_like(C_ref)
        n_ref[...] = jnp.zeros_like(n_ref)
        m_ref[...] = jnp.zeros_like(m_ref)

    nb = a_ref.shape[0]
    lane = lax.broadcasted_iota(jnp.int32, (L, LANES), 1)
    ri = lax.broadcasted_iota(jnp.int32, (L, L), 0)
    ci = lax.broadcasted_iota(jnp.int32, (L, L), 1)
    causal = ci <= ri
    tril = causal.astype(F32)
    m_lane = lax.broadcasted_iota(jnp.int32, (1, LANES), 1)
    gates_t, bcum, bcum_t, m_all = [], [], [], []
    for b in range(nb):
        pre = g_ref[b] + bg_ref[...]
        gates = jnp.where(lane >= H_A, _log_sigmoid(pre), pre)
        bcum.append(_dot(tril, gates, hi=True))
        gates_t.append(gates.T)
        bcum_t.append(bcum[b].T)
        m_all.append(m_ref[b])
    m_out = list(m_all)
    chains = [(b, h) for b in range(nb) for h in range(H_A)]
    q = [a_ref[b, :, h * DK_A:(h + 1) * DK_A] for b, h in chains]
    k = [a_ref[b, :, HD + h * DK_A:HD + (h + 1) * DK_A] * (DK_A ** -0.5) for b, h in chains]
    vb = [a_ref[b, :, 2 * HD + h * DV_A:2 * HD + (h + 1) * DV_A].astype(BF16) for b, h in chains]
    qb = [x.astype(BF16) for x in q]
    C = [C_ref[b, h] for b, h in chains]
    n_row = [n_ref[b, h:h + 1, :] for b, h in chains]
    qk = [_dot_nt(qb[c], k[c].astype(BF16)) for c in range(len(chains))]
    qC = [_dot(qb[c], C[c].astype(BF16)) for c in range(len(chains))]
    s, winter, mt, kw, ws, wc = [], [], [], [], [], []
    for c, (b, h) in enumerate(chains):
        ig_row = gates_t[b][h:h + 1, :]
        b_col = bcum[b][:, H_A + h:H_A + h + 1]
        b_row = bcum_t[b][H_A + h:H_A + h + 1, :]
        m = m_all[b][:, h:h + 1]
        dmat = jnp.where(causal, b_col - b_row + ig_row, NEG)
        inter = b_col + m
        mt.append(jnp.maximum(inter, jnp.max(dmat, axis=1, keepdims=True)))
        winter.append(jnp.exp(inter - mt[c]))
        s.append(qk[c] * jnp.exp(dmat - mt[c]))
        bl = b_row[:, L - 1:L]
        gl = bl - b_row + ig_row
        m_new = jnp.maximum(bl + m, jnp.max(gl, axis=1, keepdims=True))
        ws.append(jnp.exp(gl - m_new))
        wc.append(jnp.exp(bl + m - m_new))
        kw.append((k[c].T * ws[c]).astype(BF16))
        m_out[b] = jnp.where(m_lane == h, m_new, m_out[b])
    sv = [_dot(s[c].astype(BF16), vb[c]) for c in range(len(chains))]
    kv = [_dot(kw[c], vb[c]) for c in range(len(chains))]
    kn = [_dot(jnp.broadcast_to(ws[c], (8, L)), k[c], hi=True)[0:1] for c in range(len(chains))]
    for c, (b, h) in enumerate(chains):
        og = a_ref[b, :, 3 * HD + h * DV_A:3 * HD + (h + 1) * DV_A]
        num = sv[c] + winter[c] * qC[c]
        den = jnp.sum(s[c], axis=1, keepdims=True) + winter[c] * jnp.sum(q[c] * n_row[c], axis=1, keepdims=True)
        hh = _sigmoid(og) * (num / jnp.maximum(jnp.abs(den), jnp.exp(-mt[c])))
        h_ref[b, :, h * DV_A:(h + 1) * DV_A] = hh.astype(h_ref.dtype)
        C_ref[b, h] = wc[c] * C[c] + kv[c]
        n_ref[b, h:h + 1, :] = wc[c] * n_row[c] + kn[c]
    for b in range(nb):
        m_ref[b] = m_out[b]


def mlstm_prompt(a, g, bg):
    B, T, _ = a.shape
    L = CHUNK_A
    return pl.pallas_call(
        _mlstm_body,
        grid=(T // L,),
        in_specs=[pl.BlockSpec((B, L, a.shape[2]), lambda c: (0, c, 0)),
                  pl.BlockSpec((B, L, LANES), lambda c: (0, c, 0)),
                  pl.BlockSpec((1, LANES), lambda c: (0, 0))],
        out_specs=[pl.BlockSpec((B, L, H_A * DV_A), lambda c: (0, c, 0)),
                   pl.BlockSpec((B, H_A, DK_A, DV_A), lambda c: (0, 0, 0, 0)),
                   pl.BlockSpec((B, H_A, DK_A), lambda c: (0, 0, 0)),
                   pl.BlockSpec((B, 1, LANES), lambda c: (0, 0, 0))],
        out_shape=[jax.ShapeDtypeStruct((B, T, H_A * DV_A), BF16),
                   jax.ShapeDtypeStruct((B, H_A, DK_A, DV_A), F32),
                   jax.ShapeDtypeStruct((B, H_A, DK_A), F32),
                   jax.ShapeDtypeStruct((B, 1, LANES), F32)],
        compiler_params=_params(("arbitrary",), big=True),
        name="mlstm_prompt",
    )(a, g, bg)


def _mlstm_step_body(a_ref, g_ref, bg_ref, C0_ref, n0_ref, m0_ref, h_ref, C_ref, n_ref, m_ref):
    HD = H_A * DK_A
    pre = g_ref[0] + bg_ref[...]
    lane = lax.broadcasted_iota(jnp.int32, (1, LANES), 1)
    gates = jnp.where(lane >= H_A, _log_sigmoid(pre), pre)
    m_all = m0_ref[0]
    m_out = m_all
    for h in range(H_A):
        q = a_ref[0, :, h * DK_A:(h + 1) * DK_A]
        k = a_ref[0, :, HD + h * DK_A:HD + (h + 1) * DK_A] * (DK_A ** -0.5)
        v = a_ref[0, :, 2 * HD + h * DV_A:2 * HD + (h + 1) * DV_A]
        og = a_ref[0, :, 3 * HD + h * DV_A:3 * HD + (h + 1) * DV_A]
        ig = gates[:, h:h + 1]
        lf = gates[:, H_A + h:H_A + h + 1]
        m = m_all[:, h:h + 1]
        C = C0_ref[0, h]
        n_row = n0_ref[0, h:h + 1, :]
        inter = lf + m
        mt = jnp.maximum(inter, ig)
        s = jnp.sum(q * k, axis=1, keepdims=True) * jnp.exp(ig - mt)
        winter = jnp.exp(inter - mt)
        qC = _dot(jnp.broadcast_to(q, (8, DK_A)), C, hi=True)[0:1]
        num = s * v + winter * qC
        den = s + winter * jnp.sum(q * n_row, axis=1, keepdims=True)
        hh = num / jnp.maximum(jnp.abs(den), jnp.exp(-mt))
        h_ref[0, :, h * DV_A:(h + 1) * DV_A] = _sigmoid(og) * hh
        m_new = jnp.maximum(inter, ig)
        ws = jnp.exp(ig - m_new)
        wc = jnp.exp(inter - m_new)
        C_ref[0, h] = wc * C + (ws * _col_of_row(k)) * v
        n_ref[0, h:h + 1, :] = wc * n_row + ws * k
        m_out = jnp.where(lane == h, m_new, m_out)
    m_ref[0] = m_out


def mlstm_step(a, g, bg, C0, n0, m0):
    B = a.shape[0]
    r3 = lambda w: pl.BlockSpec((1, 1, w), lambda b: (b, 0, 0))
    cs = pl.BlockSpec((1, H_A, DK_A, DV_A), lambda b: (b, 0, 0, 0))
    ns = pl.BlockSpec((1, H_A, DK_A), lambda b: (b, 0, 0))
    return pl.pallas_call(
        _mlstm_step_body,
        grid=(B,),
        in_specs=[r3(a.shape[2]), r3(LANES), pl.BlockSpec((1, LANES), lambda b: (0, 0)), cs, ns, r3(LANES)],
        out_specs=[r3(H_A * DV_A), cs, ns, r3(LANES)],
        out_shape=[jax.ShapeDtypeStruct((B, 1, H_A * DV_A), F32),
                   jax.ShapeDtypeStruct(C0.shape, F32), jax.ShapeDtypeStruct(n0.shape, F32),
                   jax.ShapeDtypeStruct((B, 1, LANES), F32)],
        compiler_params=_params(("parallel",)),
        name="mlstm_step",
    )(a, g, bg, C0, n0, m0)


NPAIR_B = GW_B // LANES
SWA_ROWS = 512


def _swa_body(*refs, dil):
    Q = SWA_BAND
    q_refs, kc_refs, kp_refs, vc_refs, vp_refs = [refs[i * NPAIR_B:(i + 1) * NPAIR_B] for i in range(5)]
    o_refs = refs[5 * NPAIR_B:6 * NPAIR_B]
    l_refs = refs[6 * NPAIR_B:7 * NPAIR_B]
    qi = lax.broadcasted_iota(jnp.int32, (Q, Q), 0)
    kj = lax.broadcasted_iota(jnp.int32, (Q, Q), 1)
    mask_c = kj <= qi
    mask_in = kj >= qi
    mask_first = jnp.logical_and(mask_in, pl.program_id(1) > 0)
    lane = lax.broadcasted_iota(jnp.int32, (Q, LANES), 1)
    first = lane < DH_B
    chains = [(p, hh) for p in range(NPAIR_B) for hh in range(2)]
    nbs = q_refs[0].shape[1] // (Q * dil)
    for j, r in [(j, r) for j in range(nbs) for r in range(dil)]:
        at = lambda start: (pl.ds(0, 1), pl.ds(start, Q, stride=dil) if dil > 1 else pl.ds(start, Q), slice(None))
        rows = at(j * Q * dil + r)
        ld = lambda ref, rw=rows: ref[rw][0]
        q = [ld(ref) * (DH_B ** -0.5) for ref in q_refs]
        kc = [ld(ref).astype(BF16) for ref in kc_refs]
        vc = [ld(ref).astype(BF16) for ref in vc_refs]
        if j == 0:
            prows, mask_p = at((nbs - 1) * Q * dil + r), mask_first
            kp = [ld(ref, prows).astype(BF16) for ref in kp_refs]
            vp = [ld(ref, prows).astype(BF16) for ref in vp_refs]
        else:
            prows, mask_p = at((j - 1) * Q * dil + r), mask_in
            kp = [ld(ref, prows).astype(BF16) for ref in kc_refs]
            vp = [ld(ref, prows).astype(BF16) for ref in vc_refs]
        qm = [jnp.where(first if hh == 0 else ~first, q[p], 0.0).astype(BF16) for p, hh in chains]
        sc = [_dot_nt(qm[c], kc[p]) for c, (p, hh) in enumerate(chains)]
        sp = [_dot_nt(qm[c], kp[p]) for c, (p, hh) in enumerate(chains)]
        pcs, pps, dens, lses = [], [], [], []
        for c in range(len(chains)):
            s_c = jnp.where(mask_c, sc[c], NEG)
            s_p = jnp.where(mask_p, sp[c], NEG)
            mx = jnp.maximum(jnp.max(s_c, axis=1, keepdims=True), jnp.max(s_p, axis=1, keepdims=True))
            pc = jnp.exp(s_c - mx)
            pp = jnp.exp(s_p - mx)
            den = jnp.sum(pc, axis=1, keepdims=True) + jnp.sum(pp, axis=1, keepdims=True)
            pcs.append(pc.astype(BF16))
            pps.append(pp.astype(BF16))
            dens.append(den)
            lses.append(mx + jnp.log(den))
        os_ = [(_dot(pcs[c], vc[p]) + _dot(pps[c], vp[p])) / dens[c] for c, (p, hh) in enumerate(chains)]
        for p in range(NPAIR_B):
            o_refs[p][rows] = jnp.where(first, os_[2 * p], os_[2 * p + 1])[None]
            l_refs[p][rows] = jnp.where(first, lses[2 * p], lses[2 * p + 1])[None]


def swa_prompt(qkv, g, dil):
    B, T, W = qkv.shape
    ng = W // GW_B // 3
    unit = max(SWA_BAND * dil, min(SWA_ROWS, T))
    col = lambda part, p: (part * ng + g) * NPAIR_B + p
    cur = lambda part: [pl.BlockSpec((1, unit, LANES), functools.partial(lambda b, n, c: (b, n, c), c=col(part, p)))
                        for p in range(NPAIR_B)]
    prev = lambda part: [pl.BlockSpec((1, unit, LANES),
                                      functools.partial(lambda b, n, c: (b, jnp.maximum(n - 1, 0), c), c=col(part, p)))
                         for p in range(NPAIR_B)]
    ospec = pl.BlockSpec((1, unit, LANES), lambda b, n: (b, n, 0))
    res = pl.pallas_call(
        functools.partial(_swa_body, dil=dil),
        grid=(B, T // unit),
        in_specs=cur(0) + cur(1) + prev(1) + cur(2) + prev(2),
        out_specs=[ospec] * (2 * NPAIR_B),
        out_shape=[jax.ShapeDtypeStruct((B, T, LANES), F32)] * (2 * NPAIR_B),
        compiler_params=_params(("parallel", "parallel"), big=True),
        name="swa_prompt_d%d" % dil,
    )(*([qkv] * (5 * NPAIR_B)))
    return res[:NPAIR_B], res[NPAIR_B:]


def _swa_step_body(q_ref, kvn_ref, buf_ref, o_ref, l_ref, nb_ref, *, dil):
    wb = buf_ref.shape[-1]
    lane = lax.broadcasted_iota(jnp.int32, (1, wb), 1)
    read = (lane % dil) == 0
    last = lane == wb - 1
    for h in range(H_B):
        q = q_ref[0, h]
        kn = kvn_ref[0, 0, h]
        vn = kvn_ref[0, 1, h]
        kt = buf_ref[0, 0, h]
        vt = buf_ref[0, 1, h]
        s = jnp.where(read, jnp.sum(kt * q, axis=0, keepdims=True) * (DH_B ** -0.5), NEG)
        s_self = jnp.sum(kn * q, axis=0, keepdims=True) * (DH_B ** -0.5)
        mx = jnp.maximum(jnp.max(s, axis=1, keepdims=True), s_self)
        p = jnp.exp(s - mx)
        p_self = jnp.exp(s_self - mx)
        den = jnp.sum(p, axis=1, keepdims=True) + p_self
        o_ref[0, h] = (jnp.sum(vt * p, axis=1, keepdims=True) + p_self * vn) / den
        l_ref[0, h] = jnp.broadcast_to(mx + jnp.log(den), (DH_B, 1))
        nb_ref[0, 0, h] = jnp.where(last, kn, pltpu.roll(kt, wb - 1, axis=1))
        nb_ref[0, 1, h] = jnp.where(last, vn, pltpu.roll(vt, wb - 1, axis=1))


def swa_step(q, kvn, buf_t, dil):
    B = q.shape[0]
    qs = pl.BlockSpec((1,) + q.shape[1:], lambda b: (b, 0, 0, 0))
    ks = pl.BlockSpec((1,) + kvn.shape[1:], lambda b: (b, 0, 0, 0, 0))
    bs = pl.BlockSpec((1,) + buf_t.shape[1:], lambda b: (b, 0, 0, 0, 0))
    return pl.pallas_call(
        functools.partial(_swa_step_body, dil=dil),
        grid=(B,),
        in_specs=[qs, ks, bs],
        out_specs=[qs, qs, bs],
        out_shape=[jax.ShapeDtypeStruct(q.shape, F32)] * 2 + [jax.ShapeDtypeStruct(buf_t.shape, F32)],
        compiler_params=_params(("parallel",), big=True),
        name="swa_step_d%d" % dil,
    )(q, kvn, buf_t)


AUG = 2 * DH_C


VROWS = DH_C + 16
TQ_C = 128
HEADS_C = 4
PREP_COLS = 512
LOG2E = 1.4426950408889634


def _moba_kprep_body(kt_ref, vt_ref, km_ref, ka_ref, vta_ref, *, nblk):
    nsub = kt_ref.shape[2] // MOBA_BLOCK
    lane = lax.broadcasted_iota(jnp.int32, (MOBA_BLOCK, DH_C), 1)
    ones = jnp.where(lax.broadcasted_iota(jnp.int32, (VROWS - DH_C, MOBA_BLOCK), 0) == 0, 1.0, 0.0).astype(BF16)
    for j in range(nsub):
        n = pl.program_id(1) * nsub + j
        cols = slice(j * MOBA_BLOCK, (j + 1) * MOBA_BLOCK)
        k = kt_ref[0, :, cols].T
        km_ref[0, pl.ds(n, 1), :] = jnp.mean(k, axis=0, keepdims=True)
        onehot = jnp.where(lane == n, 1.0, 0.0).astype(BF16)
        pieces = []
        for h in range(H_C):
            pieces += [k[:, h * DH_C:(h + 1) * DH_C].astype(BF16), onehot]
        ka_ref[0, cols, :] = jnp.concatenate(pieces, axis=1)
        vt = vt_ref[0, :, cols]
        pieces = []
        for h in range(H_C):
            pieces += [vt[h * DH_C:(h + 1) * DH_C].astype(BF16), ones]
        vta_ref[0, :, cols] = jnp.concatenate(pieces, axis=0)


def _moba_gate_body(qt_ref, km_ref, qa_ref, *, nblk):
    nq = qt_ref.shape[2]
    blk = lax.broadcasted_iota(jnp.int32, (nblk, nq), 0)
    blk_f = blk.astype(F32)
    own = pl.program_id(1) * (nq // MOBA_BLOCK) + lax.broadcasted_iota(jnp.int32, (nblk, nq), 1) // MOBA_BLOCK
    past = blk < own
    gs = [_dot(km_ref[0, :, h * DH_C:(h + 1) * DH_C], qt_ref[0, h * DH_C:(h + 1) * DH_C, :], hi=True)
          for h in range(H_C)]
    for h in range(H_C):
        g = jnp.where(past, gs[h], NEG)
        sel = blk == own
        for _ in range(MOBA_TOPK):
            mx = jnp.max(g, axis=0, keepdims=True)
            first = jnp.min(jnp.where(g == mx, blk_f, float(nblk)), axis=0, keepdims=True)
            pick = blk_f == first
            sel = jnp.logical_or(sel, jnp.logical_and(pick, past))
            g = jnp.where(pick, -jnp.inf, g)
        bias = jnp.where(sel, 0.0, NEG)
        aug = jnp.concatenate([qt_ref[0, h * DH_C:(h + 1) * DH_C, :] * (DH_C ** -0.5 * LOG2E), bias,
                               jnp.zeros((AUG - DH_C - nblk, nq), F32)], axis=0)
        qa_ref[0, h * AUG:(h + 1) * AUG, :] = aug.astype(BF16)


def _moba_attn_body(qt_ref, ka_ref, vta_ref, o_ref, sa_scr, sb_scr):
    own = pl.program_id(2)
    TK = MOBA_BLOCK
    nsub = TK // TQ_C
    chains = [(hh, qs) for hh in range(HEADS_C) for qs in range(nsub)]
    qts = [qt_ref[0, hh * AUG:(hh + 1) * AUG, qs * TQ_C:(qs + 1) * TQ_C] for hh, qs in chains]

    def block(n, hh):
        r0 = pl.multiple_of(n * TK, TK)
        return (ka_ref[0, pl.ds(r0, TK), hh * AUG:(hh + 1) * AUG],
                vta_ref[0, hh * VROWS:(hh + 1) * VROWS, pl.ds(r0, TK)])

    key = lax.broadcasted_iota(jnp.int32, (TK, TQ_C), 0)
    qry = lax.broadcasted_iota(jnp.int32, (TK, TQ_C), 1)

    def issue_scores(n, s_scr):
        ks = [block(n, hh)[0] for hh in range(HEADS_C)]
        for c, (hh, qs) in enumerate(chains):
            s_scr[c] = _dot(ks[hh], qts[c])

    def absorb(n, s_scr, ms_prev, accs, diagonal):
        vs = [block(n, hh)[1] for hh in range(HEADS_C)]
        ms, ps, alphas = [], [], []
        for c, (hh, qs) in enumerate(chains):
            s = s_scr[c]
            if diagonal:
                s = jnp.where(key <= qry + qs * TQ_C, s, NEG)
            mx = jnp.max(s, axis=0, keepdims=True)
            m_new = mx if ms_prev is None else jnp.maximum(ms_prev[c], mx)
            ms.append(m_new)
            ps.append(jnp.exp2(s - m_new).astype(BF16))
            alphas.append(None if ms_prev is None else jnp.exp2(ms_prev[c] - m_new))
        pv = [_dot(vs[hh], ps[c]) for c, (hh, qs) in enumerate(chains)]
        return tuple(ms), tuple(pv[c] if accs is None else alphas[c] * accs[c] + pv[c] for c in range(len(chains)))

    last = jnp.maximum(own - 1, 0)
    issue_scores(own, sa_scr)
    issue_scores(0, sb_scr)
    state = absorb(own, sa_scr, None, None, True)

    def body(i, state):
        issue_scores(2 * i + 1, sa_scr)
        state = absorb(2 * i, sb_scr, state[0], state[1], False)
        issue_scores(jnp.minimum(2 * i + 2, last), sb_scr)
        return absorb(2 * i + 1, sa_scr, state[0], state[1], False)

    state = lax.fori_loop(0, own // 2, body, state)
    ms, accs = lax.cond(own % 2 == 1, lambda st: absorb(last, sb_scr, st[0], st[1], False), lambda st: st, state)
    for qs in range(nsub):
        sub = [accs[c] for c, (hh, q2) in enumerate(chains) if q2 == qs]
        o = jnp.concatenate([a[:DH_C] / a[DH_C:DH_C + 1] for a in sub], axis=0)
        o_ref[0, qs * TQ_C:(qs + 1) * TQ_C, :] = o.T.astype(o_ref.dtype)


def moba_prompt(q_t, kvt):
    B, W, T = q_t.shape
    nblk = T // MOBA_BLOCK
    assert nblk <= AUG - DH_C and nblk % 8 == 0
    pc = min(PREP_COLS, T)
    km, ka, vta = pl.pallas_call(
        functools.partial(_moba_kprep_body, nblk=nblk),
        grid=(B, T // pc),
        in_specs=[pl.BlockSpec((1, W, pc), lambda b, n: (b, 0, n)),
                  pl.BlockSpec((1, W, pc), lambda b, n: (b, 1, n))],
        out_specs=[pl.BlockSpec((1, nblk, W), lambda b, n: (b, 0, 0)),
                   pl.BlockSpec((1, pc, H_C * AUG), lambda b, n: (b, n, 0)),
                   pl.BlockSpec((1, H_C * VROWS, pc), lambda b, n: (b, 0, n))],
        out_shape=[jax.ShapeDtypeStruct((B, nblk, W), F32),
                   jax.ShapeDtypeStruct((B, T, H_C * AUG), BF16),
                   jax.ShapeDtypeStruct((B, H_C * VROWS, T), BF16)],
        compiler_params=_params(("parallel", "arbitrary")),
        name="moba_kprep",
    )(kvt, kvt)
    qt = pl.pallas_call(
        functools.partial(_moba_gate_body, nblk=nblk),
        grid=(B, T // pc),
        in_specs=[pl.BlockSpec((1, W, pc), lambda b, i: (b, 0, i)),
                  pl.BlockSpec((1, nblk, W), lambda b, i: (b, 0, 0))],
        out_specs=pl.BlockSpec((1, H_C * AUG, pc), lambda b, i: (b, 0, i)),
        out_shape=jax.ShapeDtypeStruct((B, H_C * AUG, T), BF16),
        compiler_params=_params(("parallel", "parallel")),
        name="moba_gate",
    )(q_t, km)
    return pl.pallas_call(
        _moba_attn_body,
        grid=(B, H_C // HEADS_C, nblk),
        in_specs=[pl.BlockSpec((1, HEADS_C * AUG, MOBA_BLOCK), lambda b, p, j: (b, p, j)),
                  pl.BlockSpec((1, T, HEADS_C * AUG), lambda b, p, j: (b, 0, p)),
                  pl.BlockSpec((1, HEADS_C * VROWS, T), lambda b, p, j: (b, p, 0))],
        out_specs=pl.BlockSpec((1, MOBA_BLOCK, HEADS_C * DH_C), lambda b, p, j: (b, j, p)),
        out_shape=jax.ShapeDtypeStruct((B, T, W), BF16),
        scratch_shapes=[pltpu.VMEM((HEADS_C * (MOBA_BLOCK // TQ_C), MOBA_BLOCK, TQ_C), F32)] * 2,
        compiler_params=_params(("parallel", "parallel", "arbitrary"), big=True),
        name="moba_attn",
    )(qt, ka, vta)


PAGES_PER_STEP = 32
HEADS_STEP_C = 4


def _moba_step_gate_body(pt_ref, q_ref, *refs, pages_per_block, n_past):
    page_refs = refs[:PAGES_PER_STEP]
    sel_ref, g_scr = refs[PAGES_PER_STEP], refs[PAGES_PER_STEP + 1]
    s = pl.program_id(1)
    q = q_ref[0]
    W = q.shape[0]
    rows = page_refs[0].shape[-1]
    per_step = PAGES_PER_STEP // pages_per_block
    lane = lax.broadcasted_iota(jnp.int32, (W, LANES), 1)

    @pl.when(s == 0)
    def _():
        g_scr[...] = jnp.zeros_like(g_scr)

    g = g_scr[...]
    for j in range(per_step):
        tot = None
        for i in range(pages_per_block):
            t = page_refs[j * pages_per_block + i][0, 0].reshape(W, rows)
            tot = t if tot is None else tot + t
        kmean = jnp.sum(tot, axis=1, keepdims=True) / (pages_per_block * rows)
        g = jnp.where(lane == s * per_step + j, q * kmean, g)
    g_scr[...] = g

    @pl.when(s == pl.num_programs(1) - 1)
    def _():
        seg = (lax.broadcasted_iota(jnp.int32, (8, W), 1) // DH_C ==
               lax.broadcasted_iota(jnp.int32, (8, W), 0)).astype(F32)
        blk = lax.broadcasted_iota(jnp.int32, (8, LANES), 1)
        blk_f = blk.astype(F32)
        gate = jnp.where(blk < n_past, _dot(seg, g, hi=True), NEG)
        sel = jnp.zeros((8, LANES), F32)
        for r in range(MOBA_TOPK):
            mx = jnp.max(gate, axis=1, keepdims=True)
            first = jnp.min(jnp.where(gate == mx, blk_f, float(LANES)), axis=1, keepdims=True)
            sel = jnp.where(blk == r, first, sel)
            gate = jnp.where(blk_f == first, -jnp.inf, gate)
        sel_ref[0] = sel.astype(jnp.int32)


def _moba_step_attn_body(pt_ref, sel_ref, q_ref, kn_ref, vn_ref, *refs, n_pages):
    nh = HEADS_STEP_C
    o_ref = refs[2 * nh * n_pages]
    scale = DH_C ** -0.5
    for hh in range(nh):
        k_refs = refs[hh * n_pages:(hh + 1) * n_pages]
        v_refs = refs[(nh + hh) * n_pages:(nh + hh + 1) * n_pages]
        q = q_ref[0, hh]
        kn = kn_ref[0, 0, hh]
        vn = vn_ref[0, 0, hh]
        ss = [jnp.sum(k_ref[0, 0, 0] * q, axis=0, keepdims=True) * scale for k_ref in k_refs]
        s_self = jnp.sum(kn * q, axis=0, keepdims=True) * scale
        mx = s_self
        for sj in ss:
            mx = jnp.maximum(mx, jnp.max(sj, axis=1, keepdims=True))
        p_self = jnp.exp(s_self - mx)
        den = p_self
        o = p_self * vn
        for sj, v_ref in zip(ss, v_refs):
            p = jnp.exp(sj - mx)
            den = den + jnp.sum(p, axis=1, keepdims=True)
            o = o + jnp.sum(v_ref[0, 0, 0] * p, axis=1, keepdims=True)
        o_ref[0, hh] = o / den


def moba_step(q, kv_new, pool_t, page_table):
    B, W, _ = q.shape
    page = pool_t.shape[-1]
    n_pt = page_table.shape[1]
    ppb = MOBA_BLOCK // page
    n_past = n_pt // ppb
    assert MOBA_TOPK <= n_past <= LANES and PAGES_PER_STEP % ppb == 0 and n_pt % PAGES_PER_STEP == 0
    pt_flat = page_table.reshape(-1)
    kpage = (1, 1, H_C, DH_C, page)
    sel = pl.pallas_call(
        functools.partial(_moba_step_gate_body, pages_per_block=ppb, n_past=n_past),
        grid_spec=pltpu.PrefetchScalarGridSpec(
            num_scalar_prefetch=1,
            grid=(B, n_pt // PAGES_PER_STEP),
            in_specs=[pl.BlockSpec((1, W, 1), lambda b, s, pt: (b, 0, 0))] +
                     [pl.BlockSpec(kpage, functools.partial(
                         lambda b, s, pt, i: (pt[b * n_pt + s * PAGES_PER_STEP + i], 0, 0, 0, 0), i=i))
                      for i in range(PAGES_PER_STEP)],
            out_specs=pl.BlockSpec((1, 8, LANES), lambda b, s, pt: (b, 0, 0)),
            scratch_shapes=[pltpu.VMEM((W, LANES), F32)]),
        out_shape=jax.ShapeDtypeStruct((B, 8, LANES), jnp.int32),
        compiler_params=_params(("parallel", "arbitrary"), big=True),
        name="moba_step_gate",
    )(pt_flat, q, *([pool_t] * PAGES_PER_STEP))
    sel_flat = sel[:, :H_C, :MOBA_TOPK].reshape(-1)
    n_pages = MOBA_TOPK * ppb

    nh = HEADS_STEP_C

    def page_map(b, hq, pt, sl, hh, r, i, kv):
        h = hq * nh + hh
        blk = sl[(b * H_C + h) * MOBA_TOPK + r]
        return (pt[b * n_pt + blk * ppb + i], kv, h, 0, 0)

    def page_specs(kv):
        return [pl.BlockSpec((1, 1, 1, DH_C, page), functools.partial(page_map, hh=hh, r=r, i=i, kv=kv))
                for hh in range(nh) for r in range(MOBA_TOPK) for i in range(ppb)]

    q4 = q.reshape(B, H_C, DH_C, 1)
    head = pl.BlockSpec((1, nh, DH_C, 1), lambda b, hq, pt, sl: (b, hq, 0, 0))
    new = lambda kv: pl.BlockSpec((1, 1, nh, DH_C, 1), lambda b, hq, pt, sl: (b, kv, hq, 0, 0))
    return pl.pallas_call(
        functools.partial(_moba_step_attn_body, n_pages=n_pages),
        grid_spec=pltpu.PrefetchScalarGridSpec(
            num_scalar_prefetch=2,
            grid=(B, H_C // nh),
            in_specs=[head, new(0), new(1)] + page_specs(0) + page_specs(1),
            out_specs=head),
        out_shape=jax.ShapeDtypeStruct(q4.shape, F32),
        compiler_params=_params(("parallel", "parallel")),
        name="moba_step_attn",
    )(pt_flat, sel_flat, q4, kv_new, kv_new, *([pool_t] * (2 * nh * n_pages)))


def _hgrn_gates(f, wlb_ref, layer):
    w = wlb_ref[...]
    e = jnp.exp(w - jnp.max(w, axis=0, keepdims=True))
    sm = e / jnp.sum(e, axis=0, keepdims=True)
    lb = jnp.sum(sm[1:layer + 1], axis=0, keepdims=True)
    sg = _sigmoid(f)
    return jnp.log(lb + (1.0 - lb) * sg), (1.0 - lb) * (1.0 - sg)


def _hgrn_body(d_ref, wlb_ref, go_ref, o_ref, S_ref, *, layer):
    L, SB = CHUNK_D, SUB_D
    HD = H_D * DK_D

    @pl.when(pl.program_id(0) == 0)
    def _():
        S_ref[...] = jnp.zeros_like(S_ref)

    nb = d_ref.shape[0]
    ri = lax.broadcasted_iota(jnp.int32, (L, L), 0)
    ci = lax.broadcasted_iota(jnp.int32, (L, L), 1)
    tril = (ci <= ri).astype(F32)
    sub_row = lax.broadcasted_iota(jnp.int32, (SB, 1), 0)
    gates = [_hgrn_gates(d_ref[b, :, HD:2 * HD], wlb_ref, layer) for b in range(nb)]
    b_all = [_dot(tril, lf, hi=True) for lf, _ in gates]
    chains = [(b, h) for b in range(nb) for h in range(H_D)]
    nch = range(len(chains))
    blocks = range(SB, L, SB)
    q = [d_ref[b, :, h * DK_D:(h + 1) * DK_D] for b, h in chains]
    v = [d_ref[b, :, 2 * HD + h * DV_D:2 * HD + (h + 1) * DV_D] for b, h in chains]
    bc = [b_all[b][:, h * DK_D:(h + 1) * DK_D] for b, h in chains]
    k = [gates[b][1][:, h * DK_D:(h + 1) * DK_D] for b, h in chains]
    S = [S_ref[b, h] for b, h in chains]
    vb = [x.astype(BF16) for x in v]
    o_inter = [_dot((q[c] * jnp.exp(bc[c])).astype(BF16), S[c].astype(BF16)) for c in nch]
    rnd = lambda x: x.astype(BF16).astype(F32)
    vr = [rnd(x) for x in v]
    a_off = []
    for c in nch:
        for r0 in blocks:
            ref_b = bc[c][r0 - 1:r0]
            qs = rnd(q[c][r0:r0 + SB] * jnp.exp(bc[c][r0:r0 + SB] - ref_b))
            ks = rnd(k[c][:r0] * jnp.exp(ref_b - bc[c][:r0]))
            a_off.append(rnd(_dot_nt(qs, ks)))
    kw = [(k[c] * jnp.exp(bc[c][L - 1:L] - bc[c])).astype(BF16) for c in nch]
    kv = [_dot_tn(kw[c], vb[c]) for c in nch]
    o_off = [_dot(a_off[c * len(blocks) + i], vr[c][:r0]) for c in nch for i, r0 in enumerate(blocks)]
    for c, (b, h) in enumerate(chains):
        b2 = bc[c] * LOG2E
        o_blocks = []
        for i, r0 in enumerate(range(0, L, SB)):
            qI = q[c][r0:r0 + SB]
            bI = b2[r0:r0 + SB]
            oI = o_inter[c][r0:r0 + SB]
            if r0 > 0:
                oI = oI + o_off[c * len(blocks) + i - 1]
            for s in range(SB):
                row = r0 + s
                a = jnp.sum(qI * k[c][row:row + 1] * jnp.exp2(bI - b2[row:row + 1]), axis=1, keepdims=True)
                oI = oI + jnp.where(sub_row >= s, a, 0.0) * v[c][row:row + 1]
            o_blocks.append(oI)
        o = jnp.concatenate(o_blocks, axis=0)
        gd = d_ref[b, :, 3 * HD + h * DV_D:3 * HD + (h + 1) * DV_D]
        o_ref[b, :, h * DV_D:(h + 1) * DV_D] = (_rms(o, go_ref[...]) * _silu(gd)).astype(o_ref.dtype)
        S_ref[b, h] = _col_of_row(jnp.exp(bc[c][L - 1:L])) * S[c] + kv[c]


def hgrn_prompt(d, w_lb, g_out, layer):
    B, T, W = d.shape
    L = CHUNK_D
    HD = H_D * DK_D
    return pl.pallas_call(
        functools.partial(_hgrn_body, layer=layer),
        grid=(T // L,),
        in_specs=[pl.BlockSpec((B, L, W), lambda c: (0, c, 0)),
                  pl.BlockSpec(w_lb.shape, lambda c: (0, 0)),
                  pl.BlockSpec((1, DV_D), lambda c: (0, 0))],
        out_specs=[pl.BlockSpec((B, L, H_D * DV_D), lambda c: (0, c, 0)),
                   pl.BlockSpec((B, H_D, DK_D, DV_D), lambda c: (0, 0, 0, 0))],
        out_shape=[jax.ShapeDtypeStruct((B, T, H_D * DV_D), BF16),
                   jax.ShapeDtypeStruct((B, H_D, DK_D, DV_D), F32)],
        compiler_params=_params(("arbitrary",), big=True),
        name="hgrn_prompt",
    )(d, w_lb, g_out.reshape(1, DV_D))


def _hgrn_step_body(d_ref, wlb_ref, go_ref, S0_ref, o_ref, S_ref, *, layer):
    HD = H_D * DK_D
    lf, kd = _hgrn_gates(d_ref[0, :, HD:2 * HD], wlb_ref, layer)
    eb = jnp.exp(lf)
    for h in range(H_D):
        sl = slice(h * DK_D, (h + 1) * DK_D)
        q = d_ref[0, :, sl]
        v = d_ref[0, :, 2 * HD + h * DV_D:2 * HD + (h + 1) * DV_D]
        gd = d_ref[0, :, 3 * HD + h * DV_D:3 * HD + (h + 1) * DV_D]
        k = kd[:, sl]
        e = eb[:, sl]
        S = S0_ref[0, h]
        a = jnp.sum(q * k, axis=1, keepdims=True)
        o = a * v + _dot(jnp.broadcast_to(q * e, (8, DK_D)), S, hi=True)[0:1]
        o_ref[0, :, h * DV_D:(h + 1) * DV_D] = _rms(o, go_ref[...]) * _silu(gd)
        S_ref[0, h] = _col_of_row(e) * S + _col_of_row(k) * v


def hgrn_step(d, w_lb, g_out, S0, layer):
    B, _, W = d.shape
    ss = pl.BlockSpec((1, H_D, DK_D, DV_D), lambda b: (b, 0, 0, 0))
    return pl.pallas_call(
        functools.partial(_hgrn_step_body, layer=layer),
        grid=(B,),
        in_specs=[pl.BlockSpec((1, 1, W), lambda b: (b, 0, 0)),
                  pl.BlockSpec(w_lb.shape, lambda b: (0, 0)),
                  pl.BlockSpec((1, DV_D), lambda b: (0, 0)), ss],
        out_specs=[pl.BlockSpec((1, 1, H_D * DV_D), lambda b: (b, 0, 0)), ss],
        out_shape=[jax.ShapeDtypeStruct((B, 1, H_D * DV_D), F32), jax.ShapeDtypeStruct(S0.shape, F32)],
        compiler_params=_params(("parallel",)),
        name="hgrn_step",
    )(d, w_lb, g_out.reshape(1, DV_D), S0)


N_A = 4 * H_A * DK_A
N_B = 3 * len(SWA_GROUPS) * GW_B
N_C = H_C * DH_C
N_D = 4 * H_D * DK_D


def _even_weights(w_in):
    gates = jnp.pad(w_in[:, N_A:N_A + 2 * H_A], ((0, 0), (0, LANES - 2 * H_A)))
    return jnp.concatenate([w_in[:, :N_A], w_in[:, N_A + 2 * H_A:], gates], axis=1)


def _trunk(x, ada, prompt, st_even, st_odd, P):
    hi = not prompt
    G, R, D = x.shape
    wdt = F32 if hi else BF16
    tm_lin = 256
    tm_out = 512
    new_even = new_odd = None
    for l in range(2):
        sh1, sc1, g1, sh2, sc2, g2 = [ada[l][..., i * D:(i + 1) * D] for i in range(6)]
        if l == 0:
            w = _even_weights(P['w_in_even']).astype(wdt)
            splits = (N_A, N_B, LANES)
            a, qkv, gg = norm_mod_proj(x, P['g_pre_mix'][l], sc1, sh1, w, splits, hi, tm_lin)
            bg = jnp.pad(P['b_gate_a'].reshape(1, 2 * H_A), ((0, 0), (0, LANES - 2 * H_A)))
            if prompt:
                ha, C, n, m = mlstm_prompt(a, gg, bg)
                swa = [swa_prompt(qkv, g, dil) for g, (_, dil) in enumerate(SWA_GROUPS)]
                T = R
                bufs = []
                for g, (win, _) in enumerate(SWA_GROUPS):
                    wb = min(win, T)
                    kb = qkv[:, T - wb:, (3 + g) * GW_B:(4 + g) * GW_B]
                    vb = qkv[:, T - wb:, (6 + g) * GW_B:(7 + g) * GW_B]
                    bufs.append(jnp.concatenate([kb, vb], axis=-1).reshape(G, wb, 2, H_B, DH_B))
                new_even = (C, n, m[:, 0, :H_A], bufs[0], bufs[1], bufs[2])
            else:
                B = R
                C0, n0, m0 = st_even[:3]
                m0p = jnp.pad(m0, ((0, 0), (0, LANES - H_A))).reshape(B, 1, LANES)
                tok = lambda t: t.reshape(B, 1, t.shape[-1])
                ha, C, n, m = mlstm_step(tok(a), tok(gg), bg, C0, n0, m0p)
                swa, bufs = [], []
                ng = len(SWA_GROUPS)
                qkv5 = qkv.reshape(B, 3, ng, H_B, DH_B, 1)
                for g, (_, dil) in enumerate(SWA_GROUPS):
                    buf_t = jnp.transpose(st_even[3 + g], (0, 2, 3, 4, 1))
                    o, lse, nb = swa_step(qkv5[:, 0, g], qkv5[:, 1:, g], buf_t, dil)
                    o, lse = o.reshape(1, B, GW_B), lse.reshape(1, B, GW_B)
                    swa.append(([o[..., p * LANES:(p + 1) * LANES] for p in range(NPAIR_B)],
                                [lse[..., p * LANES:(p + 1) * LANES] for p in range(NPAIR_B)]))
                    bufs.append(jnp.transpose(nb, (0, 4, 1, 2, 3)))
                ha = ha.reshape(1, B, -1)
                new_even = (C, n, m[:, 0, :H_A], bufs[0], bufs[1], bufs[2])
            swa_os = [swa[g][0][p] for p in range(NPAIR_B) for g in range(len(SWA_GROUPS))]
            swa_ls = [swa[g][1][p] for p in range(NPAIR_B) for g in range(len(SWA_GROUPS))]
            parts, swa_parts, w_out = [ha], (swa_os, swa_ls), P['w_out_even']
        else:
            w = P['w_in_odd'].astype(wdt)
            if prompt:
                d, q_t, kvt = norm_mod_proj(x, P['g_pre_mix'][l], sc1, sh1, w[:, 3 * N_C:], (N_D,), hi, tm_lin,
                                            wt=w[:, :3 * N_C].T, t_splits=(N_C, 2 * N_C))
                hc = moba_prompt(q_t, kvt)
                od, S = hgrn_prompt(d, P['w_lb'], P['g_out_d'], l)
                new_odd = (jnp.transpose(kvt.reshape(G, 2, H_C, DH_C, R), (0, 4, 1, 2, 3)), S)
            else:
                B = R
                qc, kvc, d = norm_mod_proj(x, P['g_pre_mix'][l], sc1, sh1, w, (N_C, 2 * N_C, N_D), hi, tm_lin)
                pool, page_table, S0 = st_odd
                hc = moba_step(qc.reshape(B, N_C, 1), kvc.reshape(B, 2, H_C, DH_C, 1),
                               jnp.transpose(pool, (0, 2, 3, 4, 1)), page_table)
                od, S = hgrn_step(d.reshape(B, 1, N_D), P['w_lb'], P['g_out_d'], S0, l)
                hc = hc.reshape(1, B, -1)
                od = od.reshape(1, B, -1)
                new_odd = (kvc.reshape(B, 1, 2, H_C, DH_C), S)
            parts, swa_parts, w_out = [hc, od], None, P['w_out_odd']
        x = layer_tail(x, g1, P['g_post_mix'][l], parts, swa_parts, w_out.astype(wdt),
                       P['g_pre_ffn'][l], sc2, sh2, g2, P['g_post_ffn'][l],
                       P['w_ffn_gate'].astype(wdt), P['w_ffn_up'].astype(wdt), P['w_ffn_down'].astype(wdt), l,
                       hi, tm=tm_out, tf=256 if hi else P['w_ffn_gate'].shape[-1])
    return x, new_even, new_odd


def kernel(x_prompt, x_sample, state_mlstm_C, state_mlstm_n, state_mlstm_m, cache_swa_w128, cache_swa_w512, cache_swa_w2048, cache_moba_kv, state_hgrn_S, page_table, c_prompt, c_sample, w_ada, b_ada, g_pre_mix, g_post_mix, g_pre_ffn, g_post_ffn, w_in_even, b_gate_a, w_out_even, w_in_odd, w_lb, g_out_d, w_out_odd, w_ffn_gate, w_ffn_up, w_ffn_down):
    P = {'g_pre_mix': g_pre_mix, 'g_post_mix': g_post_mix, 'g_pre_ffn': g_pre_ffn, 'g_post_ffn': g_post_ffn,
         'w_in_even': w_in_even, 'b_gate_a': b_gate_a, 'w_out_even': w_out_even, 'w_in_odd': w_in_odd,
         'w_lb': w_lb, 'g_out_d': g_out_d, 'w_out_odd': w_out_odd, 'w_ffn_gate': w_ffn_gate,
         'w_ffn_up': w_ffn_up, 'w_ffn_down': w_ffn_down}
    Bp = x_prompt.shape[0]
    Bs = x_sample.shape[0]
    D = x_prompt.shape[-1]
    c_all = jnp.concatenate([c_prompt, c_sample], axis=0)
    c_all = jnp.pad(c_all, ((0, -(Bp + Bs) % 8), (0, 0)))
    ada = ada_all(c_all, w_ada, b_ada)[:, :Bp + Bs]
    ada_p = ada[:, :Bp].reshape(ada.shape[0], Bp, 1, 6 * D)
    ada_s = ada[:, Bp:].reshape(ada.shape[0], 1, Bs, 6 * D)
    y_p, ev_p, od_p = _trunk(x_prompt, ada_p, True, None, None, P)
    y_s, ev_s, od_s = _trunk(
        x_sample.reshape(1, Bs, D), ada_s, False,
        (state_mlstm_C, state_mlstm_n, state_mlstm_m, cache_swa_w128, cache_swa_w512, cache_swa_w2048),
        (cache_moba_kv, page_table, state_hgrn_S), P)
    return ((y_p, y_s.reshape(x_sample.shape)) + tuple(ev_p[:3]) + tuple(ev_s[:3]) + tuple(ev_p[3:])
            + tuple(ev_s[3:]) + (od_p[0], od_s[0], od_p[1], od_s[1]))
```

```python
import functools

import jax
import jax.numpy as jnp
from jax import lax
from jax.experimental import pallas as pl
from jax.experimental.pallas import tpu as pltpu

F32 = jnp.float32
BF16 = jnp.bfloat16
HI = lax.Precision.HIGHEST

EPS = 1e-6
NEG = -1e30

H_A, DK_A, DV_A, CHUNK_A = 4, 128, 128, 128
SWA_GROUPS = ((128, 1), (512, 4), (2048, 16))
H_B, DH_B, SWA_BAND = 4, 64, 128
GW_B = H_B * DH_B
H_C, DH_C, MOBA_BLOCK, MOBA_TOPK = 8, 64, 256, 3
H_D, DK_D, DV_D, CHUNK_D, SUB_D = 4, 128, 128, 64, 8

LANES = 128
VMEM_LIMIT = 56 << 20


def _params(sem, big=False):
    return pltpu.CompilerParams(dimension_semantics=sem,
                                vmem_limit_bytes=VMEM_LIMIT if big else None)


def _dot(a, b, hi=False):
    return jnp.dot(a, b, preferred_element_type=F32, precision=HI if hi else None)


def _dot_nt(a, b, hi=False):
    return lax.dot_general(a, b, (((1,), (1,)), ((), ())), preferred_element_type=F32,
                           precision=HI if hi else None)


def _dot_tn(a, b, hi=False):
    return lax.dot_general(a, b, (((0,), (0,)), ((), ())), preferred_element_type=F32,
                           precision=HI if hi else None)


def _rms(x, g):
    return x * lax.rsqrt(jnp.mean(x * x, axis=-1, keepdims=True) + EPS) * g


def _sigmoid(x):
    return 1.0 / (1.0 + jnp.exp(-x))


def _silu(x):
    return x * _sigmoid(x)


def _log_sigmoid(x):
    return jnp.minimum(x, 0.0) - jnp.log(1.0 + jnp.exp(-jnp.abs(x)))


def _col_of_row(row):
    n = row.shape[-1]
    eye = lax.broadcasted_iota(jnp.int32, (n, n), 0) == lax.broadcasted_iota(jnp.int32, (n, n), 1)
    return jnp.sum(jnp.where(eye, row, 0.0), axis=1, keepdims=True)


def _ada_body(c_ref, w_ref, b_ref, o_ref):
    o_ref[0] = _dot(_silu(c_ref[...]), w_ref[0], hi=True) + b_ref[0]


def ada_all(c, w_ada, b_ada, tn=768):
    depth, d, n = w_ada.shape
    r = c.shape[0]
    return pl.pallas_call(
        _ada_body,
        grid=(depth, n // tn),
        in_specs=[pl.BlockSpec((r, d), lambda l, j: (0, 0)),
                  pl.BlockSpec((1, d, tn), lambda l, j: (l, 0, j)),
                  pl.BlockSpec((1, 1, tn), lambda l, j: (l, 0, j))],
        out_specs=pl.BlockSpec((1, r, tn), lambda l, j: (l, 0, j)),
        out_shape=jax.ShapeDtypeStruct((depth, r, n), F32),
        compiler_params=_params(("parallel", "parallel")),
        name="ada",
    )(c, w_ada, b_ada.reshape(depth, 1, n))


def _proj_body(x_ref, g_ref, sc_ref, sh_ref, w_ref, *refs, splits, hi, cw, t_splits):
    h = _rms(x_ref[0], g_ref[...]) * (1.0 + sc_ref[0]) + sh_ref[0]
    hc = h if hi else h.astype(BF16)
    o_refs = refs[1:] if t_splits else refs
    off = 0
    for o_ref, n in zip(o_refs, splits):
        for c0 in range(0, n, cw):
            c1 = min(c0 + cw, n)
            o_ref[0, :, c0:c1] = _dot(hc, w_ref[:, off + c0:off + c1], hi)
        off += n
    if t_splits:
        wt_ref, off = refs[0], 0
        for ot_ref, n in zip(refs[1 + len(splits):], t_splits):
            for r0 in range(0, n, cw):
                ot_ref[0, r0:r0 + cw, :] = _dot_nt(wt_ref[off + r0:off + r0 + cw, :], hc, hi)
            off += n


def _mod_spec(mod, tm):
    if mod.shape[1] == 1:
        return pl.BlockSpec((1, 1, mod.shape[2]), lambda gi, ri: (gi, 0, 0))
    return pl.BlockSpec((1, tm, mod.shape[2]), lambda gi, ri: (gi, ri, 0))


def norm_mod_proj(x, g, sc, sh, w, splits, hi, tm, wt=None, t_splits=()):
    G, R, D = x.shape
    tm = min(tm, R)
    n_all = sum(splits)
    n_t = sum(t_splits)
    ops = [x, g.reshape(1, D), sc, sh, w]
    in_specs = [pl.BlockSpec((1, tm, D), lambda gi, ri: (gi, ri, 0)),
                pl.BlockSpec((1, D), lambda gi, ri: (0, 0)),
                _mod_spec(sc, tm), _mod_spec(sh, tm),
                pl.BlockSpec((D, n_all), lambda gi, ri: (0, 0), pipeline_mode=pl.Buffered(1))]
    out_specs = [pl.BlockSpec((1, tm, n), lambda gi, ri: (gi, ri, 0)) for n in splits]
    out_shape = [jax.ShapeDtypeStruct((G, R, n), F32) for n in splits]
    if n_t:
        ops.append(wt)
        in_specs.append(pl.BlockSpec((n_t, D), lambda gi, ri: (0, 0), pipeline_mode=pl.Buffered(1)))
        out_specs += [pl.BlockSpec((1, n, tm), lambda gi, ri: (gi, 0, ri)) for n in t_splits]
        out_shape += [jax.ShapeDtypeStruct((G, n, R), F32) for n in t_splits]
    return pl.pallas_call(
        functools.partial(_proj_body, splits=tuple(splits), hi=hi, cw=512, t_splits=tuple(t_splits)),
        grid=(G, R // tm),
        in_specs=in_specs,
        out_specs=out_specs,
        out_shape=out_shape,
        compiler_params=_params(("parallel", "parallel"), big=True),
        name="proj",
    )(*ops)


FFN_CHUNK = 512


def _tail_body(*refs, n_parts, swa, hi):
    x_ref, g1_ref, gpm_ref = refs[:3]
    part_refs = refs[3:3 + n_parts]
    pos = 3 + n_parts
    ng = len(SWA_GROUPS)
    o_refs = refs[pos:pos + swa * ng]
    l_refs = refs[pos + swa * ng:pos + 2 * swa * ng]
    pos += 2 * swa * ng
    (wo_ref, gpre_ref, sc_ref, sh_ref, g2_ref, gpf_ref, wg_ref, wu_ref, wd_ref, out_ref,
     x1_scr, h_scr, acc_scr) = refs[pos:]
    k = pl.program_id(2)

    def cast(a):
        return a if hi else a.astype(BF16)

    @pl.when(k == 0)
    def _():
        acc = None
        off = 0
        for p_ref in part_refs:
            a = p_ref[0]
            n = a.shape[-1]
            t = _dot(cast(a), wo_ref[off:off + n, :], hi)
            acc = t if acc is None else acc + t
            off += n
        for c in range(swa):
            ls = [l_ref[0] for l_ref in l_refs[c * ng:(c + 1) * ng]]
            os_ = [o_ref[0] for o_ref in o_refs[c * ng:(c + 1) * ng]]
            mx = functools.reduce(jnp.maximum, ls)
            es = [jnp.exp(l - mx) for l in ls]
            hb = sum(e * o for e, o in zip(es, os_)) / sum(es)
            acc = acc + _dot(cast(hb), wo_ref[off:off + hb.shape[-1], :], hi)
            off += hb.shape[-1]
        x1 = x_ref[0] + g1_ref[0] * _rms(acc, gpm_ref[...])
        x1_scr[...] = x1
        h = _rms(x1, gpre_ref[...]) * (1.0 + sc_ref[0]) + sh_ref[0]
        h_scr[...] = h.astype(h_scr.dtype)
        acc_scr[...] = jnp.zeros_like(acc_scr)

    h = h_scr[...]
    tf = wg_ref.shape[1]
    cuts = list(range(0, tf, FFN_CHUNK)) + [tf]
    chunks = list(zip(cuts[:-1], cuts[1:]))
    gu = [(_dot(h, wg_ref[:, c0:c1], hi), _dot(h, wu_ref[:, c0:c1], hi)) for c0, c1 in chunks[:1]]
    acc = None
    for i, (c0, c1) in enumerate(chunks):
        if i + 1 < len(chunks):
            n0, n1 = chunks[i + 1]
            gu.append((_dot(h, wg_ref[:, n0:n1], hi), _dot(h, wu_ref[:, n0:n1], hi)))
        a = _silu(gu[i][0]) * gu[i][1]
        d = _dot(a.astype(h.dtype), wd_ref[c0:c1, :], hi)
        acc = d if acc is None else acc + d
    acc_scr[...] += acc

    @pl.when(k == pl.num_programs(2) - 1)
    def _():
        out_ref[0] = x1_scr[...] + g2_ref[0] * _rms(acc_scr[...], gpf_ref[...])


def layer_tail(x, g1, gpost_mix, parts, swa_parts, w_out, gpre, sc, sh, g2, gpost_ffn, wg, wu, wd, layer, hi, tm, tf):
    G, R, D = x.shape
    FF = wg.shape[2]
    tm = min(tm, R)
    row = lambda a: pl.BlockSpec((1, tm, a.shape[2]), lambda gi, ri, k: (gi, ri, 0))
    mspec = lambda m: (pl.BlockSpec((1, 1, D), lambda gi, ri, k: (gi, 0, 0)) if m.shape[1] == 1 else row(m))
    vec = pl.BlockSpec((1, D), lambda gi, ri, k: (0, 0))
    one = dict(pipeline_mode=pl.Buffered(1))
    wmode = one if tf == FF else {}
    ops = [x, g1, gpost_mix.reshape(1, D)] + list(parts)
    specs = [row(x), mspec(g1), vec] + [row(p) for p in parts]
    swa = 0
    if swa_parts is not None:
        swa = len(swa_parts[0]) // len(SWA_GROUPS)
        ops += list(swa_parts[0]) + list(swa_parts[1])
        specs += [row(a) for a in ops[-2 * len(swa_parts[0]):]]
    ops += [w_out, gpre.reshape(1, D), sc, sh, g2, gpost_ffn.reshape(1, D), wg, wu, wd]
    specs += [pl.BlockSpec(w_out.shape, lambda gi, ri, k: (0, 0), **one), vec, mspec(sc), mspec(sh), mspec(g2), vec,
              pl.BlockSpec((None, D, tf), lambda gi, ri, k: (layer, 0, k), **wmode),
              pl.BlockSpec((None, D, tf), lambda gi, ri, k: (layer, 0, k), **wmode),
              pl.BlockSpec((None, tf, D), lambda gi, ri, k: (layer, k, 0), **wmode)]
    return pl.pallas_call(
        functools.partial(_tail_body, n_parts=len(parts), swa=swa, hi=hi),
        grid=(G, R // tm, FF // tf),
        in_specs=specs,
        out_specs=row(x),
        out_shape=jax.ShapeDtypeStruct(x.shape, F32),
        scratch_shapes=[pltpu.VMEM((tm, D), F32), pltpu.VMEM((tm, D), F32 if hi else BF16), pltpu.VMEM((tm, D), F32)],
        compiler_params=_params(("parallel", "parallel", "arbitrary"), big=True),
        name="layer_tail",
    )(*ops)


def _mlstm_body(a_ref, g_ref, bg_ref, h_ref, C_ref, n_ref, m_ref):
    L = CHUNK_A
    HD = H_A * DK_A

    @pl.when(pl.program_id(0) == 0)
    def _():
        C_ref[...] = jnp.zeros_like(C_ref)
        n_ref[...] = jnp.zeros_like(n_ref)
        m_ref[...] = jnp.zeros_like(m_ref)

    nb = a_ref.shape[0]
    lane = lax.broadcasted_iota(jnp.int32, (L, LANES), 1)
    ri = lax.broadcasted_iota(jnp.int32, (L, L), 0)
    ci = lax.broadcasted_iota(jnp.int32, (L, L), 1)
    causal = ci <= ri
    tril = causal.astype(F32)
    m_lane = lax.broadcasted_iota(jnp.int32, (1, LANES), 1)
    gates_t, bcum, bcum_t, m_all = [], [], [], []
    for b in range(nb):
        pre = g_ref[b] + bg_ref[...]
        gates = jnp.where(lane >= H_A, _log_sigmoid(pre), pre)
        bcum.append(_dot(tril, gates, hi=True))
        gates_t.append(gates.T)
        bcum_t.append(bcum[b].T)
        m_all.append(m_ref[b])
    m_out = list(m_all)
    chains = [(b, h) for b in range(nb) for h in range(H_A)]
    q = [a_ref[b, :, h * DK_A:(h + 1) * DK_A] for b, h in chains]
    k = [a_ref[b, :, HD + h * DK_A:HD + (h + 1) * DK_A] * (DK_A ** -0.5) for b, h in chains]
    vb = [a_ref[b, :, 2 * HD + h * DV_A:2 * HD + (h + 1) * DV_A].astype(BF16) for b, h in chains]
    qb = [x.astype(BF16) for x in q]
    C = [C_ref[b, h] for b, h in chains]
    n_row = [n_ref[b, h:h + 1, :] for b, h in chains]
    qk = [_dot_nt(qb[c], k[c].astype(BF16)) for c in range(len(chains))]
    qC = [_dot(qb[c], C[c].astype(BF16)) for c in range(len(chains))]
    s, winter, mt, kw, ws, wc = [], [], [], [], [], []
    for c, (b, h) in enumerate(chains):
        ig_row = gates_t[b][h:h + 1, :]
        b_col = bcum[b][:, H_A + h:H_A + h + 1]
        b_row = bcum_t[b][H_A + h:H_A + h + 1, :]
        m = m_all[b][:, h:h + 1]
        dmat = jnp.where(causal, b_col - b_row + ig_row, NEG)
        inter = b_col + m
        mt.append(jnp.maximum(inter, jnp.max(dmat, axis=1, keepdims=True)))
        winter.append(jnp.exp(inter - mt[c]))
        s.append(qk[c] * jnp.exp(dmat - mt[c]))
        bl = b_row[:, L - 1:L]
        gl = bl - b_row + ig_row
        m_new = jnp.maximum(bl + m, jnp.max(gl, axis=1, keepdims=True))
        ws.append(jnp.exp(gl - m_new))
        wc.append(jnp.exp(bl + m - m_new))
        kw.append((k[c].T * ws[c]).astype(BF16))
        m_out[b] = jnp.where(m_lane == h, m_new, m_out[b])
    sv = [_dot(s[c].astype(BF16), vb[c]) for c in range(len(chains))]
    kv = [_dot(kw[c], vb[c]) for c in range(len(chains))]
    kn = [_dot(jnp.broadcast_to(ws[c], (8, L)), k[c], hi=True)[0:1] for c in range(len(chains))]
    for c, (b, h) in enumerate(chains):
        og = a_ref[b, :, 3 * HD + h * DV_A:3 * HD + (h + 1) * DV_A]
        num = sv[c] + winter[c] * qC[c]
        den = jnp.sum(s[c], axis=1, keepdims=True) + winter[c] * jnp.sum(q[c] * n_row[c], axis=1, keepdims=True)
        hh = _sigmoid(og) * (num / jnp.maximum(jnp.abs(den), jnp.exp(-mt[c])))
        h_ref[b, :, h * DV_A:(h + 1) * DV_A] = hh.astype(h_ref.dtype)
        C_ref[b, h] = wc[c] * C[c] + kv[c]
        n_ref[b, h:h + 1, :] = wc[c] * n_row[c] + kn[c]
    for b in range(nb):
        m_ref[b] = m_out[b]


def mlstm_prompt(a, g, bg):
    B, T, _ = a.shape
    L = CHUNK_A
    return pl.pallas_call(
        _mlstm_body,
        grid=(T // L,),
        in_specs=[pl.BlockSpec((B, L, a.shape[2]), lambda c: (0, c, 0)),
                  pl.BlockSpec((B, L, LANES), lambda c: (0, c, 0)),
                  pl.BlockSpec((1, LANES), lambda c: (0, 0))],
        out_specs=[pl.BlockSpec((B, L, H_A * DV_A), lambda c: (0, c, 0)),
                   pl.BlockSpec((B, H_A, DK_A, DV_A), lambda c: (0, 0, 0, 0)),
                   pl.BlockSpec((B, H_A, DK_A), lambda c: (0, 0, 0)),
                   pl.BlockSpec((B, 1, LANES), lambda c: (0, 0, 0))],
        out_shape=[jax.ShapeDtypeStruct((B, T, H_A * DV_A), BF16),
                   jax.ShapeDtypeStruct((B, H_A, DK_A, DV_A), F32),
                   jax.ShapeDtypeStruct((B, H_A, DK_A), F32),
                   jax.ShapeDtypeStruct((B, 1, LANES), F32)],
        compiler_params=_params(("arbitrary",), big=True),
        name="mlstm_prompt",
    )(a, g, bg)


def _mlstm_step_body(a_ref, g_ref, bg_ref, C0_ref, n0_ref, m0_ref, h_ref, C_ref, n_ref, m_ref):
    HD = H_A * DK_A
    pre = g_ref[0] + bg_ref[...]
    lane = lax.broadcasted_iota(jnp.int32, (1, LANES), 1)
    gates = jnp.where(lane >= H_A, _log_sigmoid(pre), pre)
    m_all = m0_ref[0]
    m_out = m_all
    for h in range(H_A):
        q = a_ref[0, :, h * DK_A:(h + 1) * DK_A]
        k = a_ref[0, :, HD + h * DK_A:HD + (h + 1) * DK_A] * (DK_A ** -0.5)
        v = a_ref[0, :, 2 * HD + h * DV_A:2 * HD + (h + 1) * DV_A]
        og = a_ref[0, :, 3 * HD + h * DV_A:3 * HD + (h + 1) * DV_A]
        ig = gates[:, h:h + 1]
        lf = gates[:, H_A + h:H_A + h + 1]
        m = m_all[:, h:h + 1]
        C = C0_ref[0, h]
        n_row = n0_ref[0, h:h + 1, :]
        inter = lf + m
        mt = jnp.maximum(inter, ig)
        s = jnp.sum(q * k, axis=1, keepdims=True) * jnp.exp(ig - mt)
        winter = jnp.exp(inter - mt)
        qC = _dot(jnp.broadcast_to(q, (8, DK_A)), C, hi=True)[0:1]
        num = s * v + winter * qC
        den = s + winter * jnp.sum(q * n_row, axis=1, keepdims=True)
        hh = num / jnp.maximum(jnp.abs(den), jnp.exp(-mt))
        h_ref[0, :, h * DV_A:(h + 1) * DV_A] = _sigmoid(og) * hh
        m_new = jnp.maximum(inter, ig)
        ws = jnp.exp(ig - m_new)
        wc = jnp.exp(inter - m_new)
        C_ref[0, h] = wc * C + (ws * _col_of_row(k)) * v
        n_ref[0, h:h + 1, :] = wc * n_row + ws * k
        m_out = jnp.where(lane == h, m_new, m_out)
    m_ref[0] = m_out


def mlstm_step(a, g, bg, C0, n0, m0):
    B = a.shape[0]
    r3 = lambda w: pl.BlockSpec((1, 1, w), lambda b: (b, 0, 0))
    cs = pl.BlockSpec((1, H_A, DK_A, DV_A), lambda b: (b, 0, 0, 0))
    ns = pl.BlockSpec((1, H_A, DK_A), lambda b: (b, 0, 0))
    return pl.pallas_call(
        _mlstm_step_body,
        grid=(B,),
        in_specs=[r3(a.shape[2]), r3(LANES), pl.BlockSpec((1, LANES), lambda b: (0, 0)), cs, ns, r3(LANES)],
        out_specs=[r3(H_A * DV_A), cs, ns, r3(LANES)],
        out_shape=[jax.ShapeDtypeStruct((B, 1, H_A * DV_A), F32),
                   jax.ShapeDtypeStruct(C0.shape, F32), jax.ShapeDtypeStruct(n0.shape, F32),
                   jax.ShapeDtypeStruct((B, 1, LANES), F32)],
        compiler_params=_params(("parallel",)),
        name="mlstm_step",
    )(a, g, bg, C0, n0, m0)


NPAIR_B = GW_B // LANES
SWA_ROWS = 512


def _swa_body(*refs, dil):
    Q = SWA_BAND
    q_refs, kc_refs, kp_refs, vc_refs, vp_refs = [refs[i * NPAIR_B:(i + 1) * NPAIR_B] for i in range(5)]
    o_refs = refs[5 * NPAIR_B:6 * NPAIR_B]
    l_refs = refs[6 * NPAIR_B:7 * NPAIR_B]
    qi = lax.broadcasted_iota(jnp.int32, (Q, Q), 0)
    kj = lax.broadcasted_iota(jnp.int32, (Q, Q), 1)
    mask_c = kj <= qi
    mask_in = kj >= qi
    mask_first = jnp.logical_and(mask_in, pl.program_id(1) > 0)
    lane = lax.broadcasted_iota(jnp.int32, (Q, LANES), 1)
    first = lane < DH_B
    chains = [(p, hh) for p in range(NPAIR_B) for hh in range(2)]
    nbs = q_refs[0].shape[1] // (Q * dil)
    for j, r in [(j, r) for j in range(nbs) for r in range(dil)]:
        at = lambda start: (pl.ds(0, 1), pl.ds(start, Q, stride=dil) if dil > 1 else pl.ds(start, Q), slice(None))
        rows = at(j * Q * dil + r)
        ld = lambda ref, rw=rows: ref[rw][0]
        q = [ld(ref) * (DH_B ** -0.5) for ref in q_refs]
        kc = [ld(ref).astype(BF16) for ref in kc_refs]
        vc = [ld(ref).astype(BF16) for ref in vc_refs]
        if j == 0:
            prows, mask_p = at((nbs - 1) * Q * dil + r), mask_first
            kp = [ld(ref, prows).astype(BF16) for ref in kp_refs]
            vp = [ld(ref, prows).astype(BF16) for ref in vp_refs]
        else:
            prows, mask_p = at((j - 1) * Q * dil + r), mask_in
            kp = [ld(ref, prows).astype(BF16) for ref in kc_refs]
            vp = [ld(ref, prows).astype(BF16) for ref in vc_refs]
        qm = [jnp.where(first if hh == 0 else ~first, q[p], 0.0).astype(BF16) for p, hh in chains]
        sc = [_dot_nt(qm[c], kc[p]) for c, (p, hh) in enumerate(chains)]
        sp = [_dot_nt(qm[c], kp[p]) for c, (p, hh) in enumerate(chains)]
        pcs, pps, dens, lses = [], [], [], []
        for c in range(len(chains)):
            s_c = jnp.where(mask_c, sc[c], NEG)
            s_p = jnp.where(mask_p, sp[c], NEG)
            mx = jnp.maximum(jnp.max(s_c, axis=1, keepdims=True), jnp.max(s_p, axis=1, keepdims=True))
            pc = jnp.exp(s_c - mx)
            pp = jnp.exp(s_p - mx)
            den = jnp.sum(pc, axis=1, keepdims=True) + jnp.sum(pp, axis=1, keepdims=True)
            pcs.append(pc.astype(BF16))
            pps.append(pp.astype(BF16))
            dens.append(den)
            lses.append(mx + jnp.log(den))
        os_ = [(_dot(pcs[c], vc[p]) + _dot(pps[c], vp[p])) / dens[c] for c, (p, hh) in enumerate(chains)]
        for p in range(NPAIR_B):
            o_refs[p][rows] = jnp.where(first, os_[2 * p], os_[2 * p + 1])[None]
            l_refs[p][rows] = jnp.where(first, lses[2 * p], lses[2 * p + 1])[None]


def swa_prompt(qkv, g, dil):
    B, T, W = qkv.shape
    ng = W // GW_B // 3
    unit = max(SWA_BAND * dil, min(SWA_ROWS, T))
    col = lambda part, p: (part * ng + g) * NPAIR_B + p
    cur = lambda part: [pl.BlockSpec((1, unit, LANES), functools.partial(lambda b, n, c: (b, n, c), c=col(part, p)))
                        for p in range(NPAIR_B)]
    prev = lambda part: [pl.BlockSpec((1, unit, LANES),
                                      functools.partial(lambda b, n, c: (b, jnp.maximum(n - 1, 0), c), c=col(part, p)))
                         for p in range(NPAIR_B)]
    ospec = pl.BlockSpec((1, unit, LANES), lambda b, n: (b, n, 0))
    res = pl.pallas_call(
        functools.partial(_swa_body, dil=dil),
        grid=(B, T // unit),
        in_specs=cur(0) + cur(1) + prev(1) + cur(2) + prev(2),
        out_specs=[ospec] * (2 * NPAIR_B),
        out_shape=[jax.ShapeDtypeStruct((B, T, LANES), F32)] * (2 * NPAIR_B),
        compiler_params=_params(("parallel", "parallel"), big=True),
        name="swa_prompt_d%d" % dil,
    )(*([qkv] * (5 * NPAIR_B)))
    return res[:NPAIR_B], res[NPAIR_B:]


def _swa_step_body(q_ref, kvn_ref, buf_ref, o_ref, l_ref, nb_ref, *, dil):
    wb = buf_ref.shape[-1]
    lane = lax.broadcasted_iota(jnp.int32, (1, wb), 1)
    read = (lane % dil) == 0
    last = lane == wb - 1
    for h in range(H_B):
        q = q_ref[0, h]
        kn = kvn_ref[0, 0, h]
        vn = kvn_ref[0, 1, h]
        kt = buf_ref[0, 0, h]
        vt = buf_ref[0, 1, h]
        s = jnp.where(read, jnp.sum(kt * q, axis=0, keepdims=True) * (DH_B ** -0.5), NEG)
        s_self = jnp.sum(kn * q, axis=0, keepdims=True) * (DH_B ** -0.5)
        mx = jnp.maximum(jnp.max(s, axis=1, keepdims=True), s_self)
        p = jnp.exp(s - mx)
        p_self = jnp.exp(s_self - mx)
        den = jnp.sum(p, axis=1, keepdims=True) + p_self
        o_ref[0, h] = (jnp.sum(vt * p, axis=1, keepdims=True) + p_self * vn) / den
        l_ref[0, h] = jnp.broadcast_to(mx + jnp.log(den), (DH_B, 1))
        nb_ref[0, 0, h] = jnp.where(last, kn, pltpu.roll(kt, wb - 1, axis=1))
        nb_ref[0, 1, h] = jnp.where(last, vn, pltpu.roll(vt, wb - 1, axis=1))


def swa_step(q, kvn, buf_t, dil):
    B = q.shape[0]
    qs = pl.BlockSpec((1,) + q.shape[1:], lambda b: (b, 0, 0, 0))
    ks = pl.BlockSpec((1,) + kvn.shape[1:], lambda b: (b, 0, 0, 0, 0))
    bs = pl.BlockSpec((1,) + buf_t.shape[1:], lambda b: (b, 0, 0, 0, 0))
    return pl.pallas_call(
        functools.partial(_swa_step_body, dil=dil),
        grid=(B,),
        in_specs=[qs, ks, bs],
        out_specs=[qs, qs, bs],
        out_shape=[jax.ShapeDtypeStruct(q.shape, F32)] * 2 + [jax.ShapeDtypeStruct(buf_t.shape, F32)],
        compiler_params=_params(("parallel",), big=True),
        name="swa_step_d%d" % dil,
    )(q, kvn, buf_t)


AUG = 2 * DH_C


VROWS = DH_C + 16
TQ_C = 128
HEADS_C = 4
PREP_COLS = 512
LOG2E = 1.4426950408889634


def _moba_kprep_body(kt_ref, vt_ref, km_ref, ka_ref, vta_ref, *, nblk):
    nsub = kt_ref.shape[2] // MOBA_BLOCK
    lane = lax.broadcasted_iota(jnp.int32, (MOBA_BLOCK, DH_C), 1)
    ones = jnp.where(lax.broadcasted_iota(jnp.int32, (VROWS - DH_C, MOBA_BLOCK), 0) == 0, 1.0, 0.0).astype(BF16)
    for j in range(nsub):
        n = pl.program_id(1) * nsub + j
        cols = slice(j * MOBA_BLOCK, (j + 1) * MOBA_BLOCK)
        k = kt_ref[0, :, cols].T
        km_ref[0, pl.ds(n, 1), :] = jnp.mean(k, axis=0, keepdims=True)
        onehot = jnp.where(lane == n, 1.0, 0.0).astype(BF16)
        pieces = []
        for h in range(H_C):
            pieces += [k[:, h * DH_C:(h + 1) * DH_C].astype(BF16), onehot]
        ka_ref[0, cols, :] = jnp.concatenate(pieces, axis=1)
        vt = vt_ref[0, :, cols]
        pieces = []
        for h in range(H_C):
            pieces += [vt[h * DH_C:(h + 1) * DH_C].astype(BF16), ones]
        vta_ref[0, :, cols] = jnp.concatenate(pieces, axis=0)


def _moba_gate_body(qt_ref, km_ref, qa_ref, *, nblk):
    nq = qt_ref.shape[2]
    blk = lax.broadcasted_iota(jnp.int32, (nblk, nq), 0)
    blk_f = blk.astype(F32)
    own = pl.program_id(1) * (nq // MOBA_BLOCK) + lax.broadcasted_iota(jnp.int32, (nblk, nq), 1) // MOBA_BLOCK
    past = blk < own
    gs = [_dot(km_ref[0, :, h * DH_C:(h + 1) * DH_C], qt_ref[0, h * DH_C:(h + 1) * DH_C, :], hi=True)
          for h in range(H_C)]
    for h in range(H_C):
        g = jnp.where(past, gs[h], NEG)
        sel = blk == own
        for _ in range(MOBA_TOPK):
            mx = jnp.max(g, axis=0, keepdims=True)
            first = jnp.min(jnp.where(g == mx, blk_f, float(nblk)), axis=0, keepdims=True)
            pick = blk_f == first
            sel = jnp.logical_or(sel, jnp.logical_and(pick, past))
            g = jnp.where(pick, -jnp.inf, g)
        bias = jnp.where(sel, 0.0, NEG)
        aug = jnp.concatenate([qt_ref[0, h * DH_C:(h + 1) * DH_C, :] * (DH_C ** -0.5 * LOG2E), bias,
                               jnp.zeros((AUG - DH_C - nblk, nq), F32)], axis=0)
        qa_ref[0, h * AUG:(h + 1) * AUG, :] = aug.astype(BF16)


def _moba_attn_body(qt_ref, ka_ref, vta_ref, o_ref, sa_scr, sb_scr):
    own = pl.program_id(2)
    TK = MOBA_BLOCK
    nsub = TK // TQ_C
    chains = [(hh, qs) for hh in range(HEADS_C) for qs in range(nsub)]
    qts = [qt_ref[0, hh * AUG:(hh + 1) * AUG, qs * TQ_C:(qs + 1) * TQ_C] for hh, qs in chains]

    def block(n, hh):
        r0 = pl.multiple_of(n * TK, TK)
        return (ka_ref[0, pl.ds(r0, TK), hh * AUG:(hh + 1) * AUG],
                vta_ref[0, hh * VROWS:(hh + 1) * VROWS, pl.ds(r0, TK)])

    key = lax.broadcasted_iota(jnp.int32, (TK, TQ_C), 0)
    qry = lax.broadcasted_iota(jnp.int32, (TK, TQ_C), 1)

    def issue_scores(n, s_scr):
        ks = [block(n, hh)[0] for hh in range(HEADS_C)]
        for c, (hh, qs) in enumerate(chains):
            s_scr[c] = _dot(ks[hh], qts[c])

    def absorb(n, s_scr, ms_prev, accs, diagonal):
        vs = [block(n, hh)[1] for hh in range(HEADS_C)]
        ms, ps, alphas = [], [], []
        for c, (hh, qs) in enumerate(chains):
            s = s_scr[c]
            if diagonal:
                s = jnp.where(key <= qry + qs * TQ_C, s, NEG)
            mx = jnp.max(s, axis=0, keepdims=True)
            m_new = mx if ms_prev is None else jnp.maximum(ms_prev[c], mx)
            ms.append(m_new)
            ps.append(jnp.exp2(s - m_new).astype(BF16))
            alphas.append(None if ms_prev is None else jnp.exp2(ms_prev[c] - m_new))
        pv = [_dot(vs[hh], ps[c]) for c, (hh, qs) in enumerate(chains)]
        return tuple(ms), tuple(pv[c] if accs is None else alphas[c] * accs[c] + pv[c] for c in range(len(chains)))

    last = jnp.maximum(own - 1, 0)
    issue_scores(own, sa_scr)
    issue_scores(0, sb_scr)
    state = absorb(own, sa_scr, None, None, True)

    def body(i, state):
        issue_scores(2 * i + 1, sa_scr)
        state = absorb(2 * i, sb_scr, state[0], state[1], False)
        issue_scores(jnp.minimum(2 * i + 2, last), sb_scr)
        return absorb(2 * i + 1, sa_scr, state[0], state[1], False)

    state = lax.fori_loop(0, own // 2, body, state)
    ms, accs = lax.cond(own % 2 == 1, lambda st: absorb(last, sb_scr, st[0], st[1], False), lambda st: st, state)
    for qs in range(nsub):
        sub = [accs[c] for c, (hh, q2) in enumerate(chains) if q2 == qs]
        o = jnp.concatenate([a[:DH_C] / a[DH_C:DH_C + 1] for a in sub], axis=0)
        o_ref[0, qs * TQ_C:(qs + 1) * TQ_C, :] = o.T.astype(o_ref.dtype)


def moba_prompt(q_t, kvt):
    B, W, T = q_t.shape
    nblk = T // MOBA_BLOCK
    assert nblk <= AUG - DH_C and nblk % 8 == 0
    pc = min(PREP_COLS, T)
    km, ka, vta = pl.pallas_call(
        functools.partial(_moba_kprep_body, nblk=nblk),
        grid=(B, T // pc),
        in_specs=[pl.BlockSpec((1, W, pc), lambda b, n: (b, 0, n)),
                  pl.BlockSpec((1, W, pc), lambda b, n: (b, 1, n))],
        out_specs=[pl.BlockSpec((1, nblk, W), lambda b, n: (b, 0, 0)),
                   pl.BlockSpec((1, pc, H_C * AUG), lambda b, n: (b, n, 0)),
                   pl.BlockSpec((1, H_C * VROWS, pc), lambda b, n: (b, 0, n))],
        out_shape=[jax.ShapeDtypeStruct((B, nblk, W), F32),
                   jax.ShapeDtypeStruct((B, T, H_C * AUG), BF16),
                   jax.ShapeDtypeStruct((B, H_C * VROWS, T), BF16)],
        compiler_params=_params(("parallel", "arbitrary")),
        name="moba_kprep",
    )(kvt, kvt)
    qt = pl.pallas_call(
        functools.partial(_moba_gate_body, nblk=nblk),
        grid=(B, T // pc),
        in_specs=[pl.BlockSpec((1, W, pc), lambda b, i: (b, 0, i)),
                  pl.BlockSpec((1, nblk, W), lambda b, i: (b, 0, 0))],
        out_specs=pl.BlockSpec((1, H_C * AUG, pc), lambda b, i: (b, 0, i)),
        out_shape=jax.ShapeDtypeStruct((B, H_C * AUG, T), BF16),
        compiler_params=_params(("parallel", "parallel")),
        name="moba_gate",
    )(q_t, km)
    return pl.pallas_call(
        _moba_attn_body,
        grid=(B, H_C // HEADS_C, nblk),
        in_specs=[pl.BlockSpec((1, HEADS_C * AUG, MOBA_BLOCK), lambda b, p, j: (b, p, j)),
                  pl.BlockSpec((1, T, HEADS_C * AUG), lambda b, p, j: (b, 0, p)),
                  pl.BlockSpec((1, HEADS_C * VROWS, T), lambda b, p, j: (b, p, 0))],
        out_specs=pl.BlockSpec((1, MOBA_BLOCK, HEADS_C * DH_C), lambda b, p, j: (b, j, p)),
        out_shape=jax.ShapeDtypeStruct((B, T, W), BF16),
        scratch_shapes=[pltpu.VMEM((HEADS_C * (MOBA_BLOCK // TQ_C), MOBA_BLOCK, TQ_C), F32)] * 2,
        compiler_params=_params(("parallel", "parallel", "arbitrary"), big=True),
        name="moba_attn",
    )(qt, ka, vta)


PAGES_PER_STEP = 32
HEADS_STEP_C = 8


def _moba_step_gate_body(pt_ref, q_ref, *refs, pages_per_block, n_past):
    page_refs = refs[:PAGES_PER_STEP]
    sel_ref, g_scr = refs[PAGES_PER_STEP], refs[PAGES_PER_STEP + 1]
    s = pl.program_id(1)
    q = q_ref[0]
    W = q.shape[0]
    rows = page_refs[0].shape[-1]
    per_step = PAGES_PER_STEP // pages_per_block
    lane = lax.broadcasted_iota(jnp.int32, (W, LANES), 1)

    @pl.when(s == 0)
    def _():
        g_scr[...] = jnp.zeros_like(g_scr)

    g = g_scr[...]
    for j in range(per_step):
        tot = None
        for i in range(pages_per_block):
            t = page_refs[j * pages_per_block + i][0, 0].reshape(W, rows)
            tot = t if tot is None else tot + t
        kmean = jnp.sum(tot, axis=1, keepdims=True) / (pages_per_block * rows)
        g = jnp.where(lane == s * per_step + j, q * kmean, g)
    g_scr[...] = g

    @pl.when(s == pl.num_programs(1) - 1)
    def _():
        seg = (lax.broadcasted_iota(jnp.int32, (8, W), 1) // DH_C ==
               lax.broadcasted_iota(jnp.int32, (8, W), 0)).astype(F32)
        blk = lax.broadcasted_iota(jnp.int32, (8, LANES), 1)
        blk_f = blk.astype(F32)
        gate = jnp.where(blk < n_past, _dot(seg, g, hi=True), NEG)
        sel = jnp.zeros((8, LANES), F32)
        for r in range(MOBA_TOPK):
            mx = jnp.max(gate, axis=1, keepdims=True)
            first = jnp.min(jnp.where(gate == mx, blk_f, float(LANES)), axis=1, keepdims=True)
            sel = jnp.where(blk == r, first, sel)
            gate = jnp.where(blk_f == first, -jnp.inf, gate)
        sel_ref[0] = sel.astype(jnp.int32)


def _moba_step_attn_body(pt_ref, sel_ref, q_ref, kn_ref, vn_ref, *refs, n_pages):
    nh = HEADS_STEP_C
    o_ref = refs[2 * nh * n_pages]
    scale = DH_C ** -0.5
    for hh in range(nh):
        k_refs = refs[hh * n_pages:(hh + 1) * n_pages]
        v_refs = refs[(nh + hh) * n_pages:(nh + hh + 1) * n_pages]
        q = q_ref[0, hh]
        kn = kn_ref[0, 0, hh]
        vn = vn_ref[0, 0, hh]
        ss = [jnp.sum(k_ref[0, 0, 0] * q, axis=0, keepdims=True) * scale for k_ref in k_refs]
        s_self = jnp.sum(kn * q, axis=0, keepdims=True) * scale
        mx = s_self
        for sj in ss:
            mx = jnp.maximum(mx, jnp.max(sj, axis=1, keepdims=True))
        p_self = jnp.exp(s_self - mx)
        den = p_self
        o = p_self * vn
        for sj, v_ref in zip(ss, v_refs):
            p = jnp.exp(sj - mx)
            den = den + jnp.sum(p, axis=1, keepdims=True)
            o = o + jnp.sum(v_ref[0, 0, 0] * p, axis=1, keepdims=True)
        o_ref[0, hh] = o / den


def moba_step(q, kv_new, pool_t, page_table):
    B, W, _ = q.shape
    page = pool_t.shape[-1]
    n_pt = page_table.shape[1]
    ppb = MOBA_BLOCK // page
    n_past = n_pt // ppb
    assert MOBA_TOPK <= n_past <= LANES and PAGES_PER_STEP % ppb == 0 and n_pt % PAGES_PER_STEP == 0
    pt_flat = page_table.reshape(-1)
    kpage = (1, 1, H_C, DH_C, page)
    sel = pl.pallas_call(
        functools.partial(_moba_step_gate_body, pages_per_block=ppb, n_past=n_past),
        grid_spec=pltpu.PrefetchScalarGridSpec(
            num_scalar_prefetch=1,
            grid=(B, n_pt // PAGES_PER_STEP),
            in_specs=[pl.BlockSpec((1, W, 1), lambda b, s, pt: (b, 0, 0))] +
                     [pl.BlockSpec(kpage, functools.partial(
                         lambda b, s, pt, i: (pt[b * n_pt + s * PAGES_PER_STEP + i], 0, 0, 0, 0), i=i))
                      for i in range(PAGES_PER_STEP)],
            out_specs=pl.BlockSpec((1, 8, LANES), lambda b, s, pt: (b, 0, 0)),
            scratch_shapes=[pltpu.VMEM((W, LANES), F32)]),
        out_shape=jax.ShapeDtypeStruct((B, 8, LANES), jnp.int32),
        compiler_params=_params(("parallel", "arbitrary"), big=True),
        name="moba_step_gate",
    )(pt_flat, q, *([pool_t] * PAGES_PER_STEP))
    sel_flat = sel[:, :H_C, :MOBA_TOPK].reshape(-1)
    n_pages = MOBA_TOPK * ppb

    nh = HEADS_STEP_C

    def page_map(b, hq, pt, sl, hh, r, i, kv):
        h = hq * nh + hh
        blk = sl[(b * H_C + h) * MOBA_TOPK + r]
        return (pt[b * n_pt + blk * ppb + i], kv, h, 0, 0)

    def page_specs(kv):
        return [pl.BlockSpec((1, 1, 1, DH_C, page), functools.partial(page_map, hh=hh, r=r, i=i, kv=kv))
                for hh in range(nh) for r in range(MOBA_TOPK) for i in range(ppb)]

    q4 = q.reshape(B, H_C, DH_C, 1)
    head = pl.BlockSpec((1, nh, DH_C, 1), lambda b, hq, pt, sl: (b, hq, 0, 0))
    new = lambda kv: pl.BlockSpec((1, 1, nh, DH_C, 1), lambda b, hq, pt, sl: (b, kv, hq, 0, 0))
    return pl.pallas_call(
        functools.partial(_moba_step_attn_body, n_pages=n_pages),
        grid_spec=pltpu.PrefetchScalarGridSpec(
            num_scalar_prefetch=2,
            grid=(B, H_C // nh),
            in_specs=[head, new(0), new(1)] + page_specs(0) + page_specs(1),
            out_specs=head),
        out_shape=jax.ShapeDtypeStruct(q4.shape, F32),
        compiler_params=_params(("parallel", "parallel")),
        name="moba_step_attn",
    )(pt_flat, sel_flat, q4, kv_new, kv_new, *([pool_t] * (2 * nh * n_pages)))


def _hgrn_gates(f, wlb_ref, layer):
    w = wlb_ref[...]
    e = jnp.exp(w - jnp.max(w, axis=0, keepdims=True))
    sm = e / jnp.sum(e, axis=0, keepdims=True)
    lb = jnp.sum(sm[1:layer + 1], axis=0, keepdims=True)
    sg = _sigmoid(f)
    return jnp.log(lb + (1.0 - lb) * sg), (1.0 - lb) * (1.0 - sg)


def _hgrn_body(d_ref, wlb_ref, go_ref, o_ref, S_ref, *, layer):
    L, SB = CHUNK_D, SUB_D
    HD = H_D * DK_D

    @pl.when(pl.program_id(0) == 0)
    def _():
        S_ref[...] = jnp.zeros_like(S_ref)

    nb = d_ref.shape[0]
    ri = lax.broadcasted_iota(jnp.int32, (L, L), 0)
    ci = lax.broadcasted_iota(jnp.int32, (L, L), 1)
    tril = (ci <= ri).astype(F32)
    sub_row = lax.broadcasted_iota(jnp.int32, (SB, 1), 0)
    gates = [_hgrn_gates(d_ref[b, :, HD:2 * HD], wlb_ref, layer) for b in range(nb)]
    b_all = [_dot(tril, lf, hi=True) for lf, _ in gates]
    chains = [(b, h) for b in range(nb) for h in range(H_D)]
    nch = range(len(chains))
    blocks = range(SB, L, SB)
    q = [d_ref[b, :, h * DK_D:(h + 1) * DK_D] for b, h in chains]
    v = [d_ref[b, :, 2 * HD + h * DV_D:2 * HD + (h + 1) * DV_D] for b, h in chains]
    bc = [b_all[b][:, h * DK_D:(h + 1) * DK_D] for b, h in chains]
    k = [gates[b][1][:, h * DK_D:(h + 1) * DK_D] for b, h in chains]
    S = [S_ref[b, h] for b, h in chains]
    vb = [x.astype(BF16) for x in v]
    o_inter = [_dot((q[c] * jnp.exp(bc[c])).astype(BF16), S[c].astype(BF16)) for c in nch]
    rnd = lambda x: x.astype(BF16).astype(F32)
    vr = [x.astype(F32) for x in vb]
    a_off = []
    for c in nch:
        for r0 in blocks:
            ref_b = bc[c][r0 - 1:r0]
            qs = rnd(q[c][r0:r0 + SB] * jnp.exp(bc[c][r0:r0 + SB] - ref_b))
            ks = rnd(k[c][:r0] * jnp.exp(ref_b - bc[c][:r0]))
            a_off.append(rnd(_dot_nt(qs, ks)))
    kw = [(k[c] * jnp.exp(bc[c][L - 1:L] - bc[c])).astype(BF16) for c in nch]
    kv = [_dot_tn(kw[c], vb[c]) for c in nch]
    o_off = [_dot(a_off[c * len(blocks) + i], vr[c][:r0]) for c in nch for i, r0 in enumerate(blocks)]
    for c, (b, h) in enumerate(chains):
        b2 = bc[c] * LOG2E
        o_blocks = []
        for i, r0 in enumerate(range(0, L, SB)):
            qI = q[c][r0:r0 + SB]
            bI = b2[r0:r0 + SB]
            oI = o_inter[c][r0:r0 + SB]
            if r0 > 0:
                oI = oI + o_off[c * len(blocks) + i - 1]
            for s in range(SB):
                row = r0 + s
                a = jnp.sum(qI * k[c][row:row + 1] * jnp.exp2(bI - b2[row:row + 1]), axis=1, keepdims=True)
                oI = oI + jnp.where(sub_row >= s, a, 0.0) * v[c][row:row + 1]
            o_blocks.append(oI)
        o = jnp.concatenate(o_blocks, axis=0)
        gd = d_ref[b, :, 3 * HD + h * DV_D:3 * HD + (h + 1) * DV_D]
        o_ref[b, :, h * DV_D:(h + 1) * DV_D] = (_rms(o, go_ref[...]) * _silu(gd)).astype(o_ref.dtype)
        S_ref[b, h] = _col_of_row(jnp.exp(bc[c][L - 1:L])) * S[c] + kv[c]


def hgrn_prompt(d, w_lb, g_out, layer):
    B, T, W = d.shape
    L = CHUNK_D
    HD = H_D * DK_D
    return pl.pallas_call(
        functools.partial(_hgrn_body, layer=layer),
        grid=(T // L,),
        in_specs=[pl.BlockSpec((B, L, W), lambda c: (0, c, 0)),
                  pl.BlockSpec(w_lb.shape, lambda c: (0, 0)),
                  pl.BlockSpec((1, DV_D), lambda c: (0, 0))],
        out_specs=[pl.BlockSpec((B, L, H_D * DV_D), lambda c: (0, c, 0)),
                   pl.BlockSpec((B, H_D, DK_D, DV_D), lambda c: (0, 0, 0, 0))],
        out_shape=[jax.ShapeDtypeStruct((B, T, H_D * DV_D), BF16),
                   jax.ShapeDtypeStruct((B, H_D, DK_D, DV_D), F32)],
        compiler_params=_params(("arbitrary",), big=True),
        name="hgrn_prompt",
    )(d, w_lb, g_out.reshape(1, DV_D))


def _hgrn_step_body(d_ref, wlb_ref, go_ref, S0_ref, o_ref, S_ref, *, layer):
    HD = H_D * DK_D
    lf, kd = _hgrn_gates(d_ref[0, :, HD:2 * HD], wlb_ref, layer)
    eb = jnp.exp(lf)
    for h in range(H_D):
        sl = slice(h * DK_D, (h + 1) * DK_D)
        q = d_ref[0, :, sl]
        v = d_ref[0, :, 2 * HD + h * DV_D:2 * HD + (h + 1) * DV_D]
        gd = d_ref[0, :, 3 * HD + h * DV_D:3 * HD + (h + 1) * DV_D]
        k = kd[:, sl]
        e = eb[:, sl]
        S = S0_ref[0, h]
        a = jnp.sum(q * k, axis=1, keepdims=True)
        o = a * v + _dot(jnp.broadcast_to(q * e, (8, DK_D)), S, hi=True)[0:1]
        o_ref[0, :, h * DV_D:(h + 1) * DV_D] = _rms(o, go_ref[...]) * _silu(gd)
        S_ref[0, h] = _col_of_row(e) * S + _col_of_row(k) * v


def hgrn_step(d, w_lb, g_out, S0, layer):
    B, _, W = d.shape
    ss = pl.BlockSpec((1, H_D, DK_D, DV_D), lambda b: (b, 0, 0, 0))
    return pl.pallas_call(
        functools.partial(_hgrn_step_body, layer=layer),
        grid=(B,),
        in_specs=[pl.BlockSpec((1, 1, W), lambda b: (b, 0, 0)),
                  pl.BlockSpec(w_lb.shape, lambda b: (0, 0)),
                  pl.BlockSpec((1, DV_D), lambda b: (0, 0)), ss],
        out_specs=[pl.BlockSpec((1, 1, H_D * DV_D), lambda b: (b, 0, 0)), ss],
        out_shape=[jax.ShapeDtypeStruct((B, 1, H_D * DV_D), F32), jax.ShapeDtypeStruct(S0.shape, F32)],
        compiler_params=_params(("parallel",)),
        name="hgrn_step",
    )(d, w_lb, g_out.reshape(1, DV_D), S0)


N_A = 4 * H_A * DK_A
N_B = 3 * len(SWA_GROUPS) * GW_B
N_C = H_C * DH_C
N_D = 4 * H_D * DK_D


def _even_weights(w_in):
    gates = jnp.pad(w_in[:, N_A:N_A + 2 * H_A], ((0, 0), (0, LANES - 2 * H_A)))
    return jnp.concatenate([w_in[:, :N_A], w_in[:, N_A + 2 * H_A:], gates], axis=1)


def _trunk(x, ada, prompt, st_even, st_odd, P):
    hi = not prompt
    G, R, D = x.shape
    wdt = F32 if hi else BF16
    tm_lin = 512
    tm_out = 512
    new_even = new_odd = None
    for l in range(2):
        sh1, sc1, g1, sh2, sc2, g2 = [ada[l][..., i * D:(i + 1) * D] for i in range(6)]
        if l == 0:
            w = _even_weights(P['w_in_even']).astype(wdt)
            splits = (N_A, N_B, LANES)
            a, qkv, gg = norm_mod_proj(x, P['g_pre_mix'][l], sc1, sh1, w, splits, hi, tm_lin)
            bg = jnp.pad(P['b_gate_a'].reshape(1, 2 * H_A), ((0, 0), (0, LANES - 2 * H_A)))
            if prompt:
                ha, C, n, m = mlstm_prompt(a, gg, bg)
                swa = [swa_prompt(qkv, g, dil) for g, (_, dil) in enumerate(SWA_GROUPS)]
                T = R
                bufs = []
                for g, (win, _) in enumerate(SWA_GROUPS):
                    wb = min(win, T)
                    kb = qkv[:, T - wb:, (3 + g) * GW_B:(4 + g) * GW_B]
                    vb = qkv[:, T - wb:, (6 + g) * GW_B:(7 + g) * GW_B]
                    bufs.append(jnp.concatenate([kb, vb], axis=-1).reshape(G, wb, 2, H_B, DH_B))
                new_even = (C, n, m[:, 0, :H_A], bufs[0], bufs[1], bufs[2])
            else:
                B = R
                C0, n0, m0 = st_even[:3]
                m0p = jnp.pad(m0, ((0, 0), (0, LANES - H_A))).reshape(B, 1, LANES)
                tok = lambda t: t.reshape(B, 1, t.shape[-1])
                ha, C, n, m = mlstm_step(tok(a), tok(gg), bg, C0, n0, m0p)
                swa, bufs = [], []
                ng = len(SWA_GROUPS)
                qkv5 = qkv.reshape(B, 3, ng, H_B, DH_B, 1)
                for g, (_, dil) in enumerate(SWA_GROUPS):
                    buf_t = jnp.transpose(st_even[3 + g], (0, 2, 3, 4, 1))
                    o, lse, nb = swa_step(qkv5[:, 0, g], qkv5[:, 1:, g], buf_t, dil)
                    o, lse = o.reshape(1, B, GW_B), lse.reshape(1, B, GW_B)
                    swa.append(([o[..., p * LANES:(p + 1) * LANES] for p in range(NPAIR_B)],
                                [lse[..., p * LANES:(p + 1) * LANES] for p in range(NPAIR_B)]))
                    bufs.append(jnp.transpose(nb, (0, 4, 1, 2, 3)))
                ha = ha.reshape(1, B, -1)
                new_even = (C, n, m[:, 0, :H_A], bufs[0], bufs[1], bufs[2])
            swa_os = [swa[g][0][p] for p in range(NPAIR_B) for g in range(len(SWA_GROUPS))]
            swa_ls = [swa[g][1][p] for p in range(NPAIR_B) for g in range(len(SWA_GROUPS))]
            parts, swa_parts, w_out = [ha], (swa_os, swa_ls), P['w_out_even']
        else:
            w = P['w_in_odd'].astype(wdt)
            if prompt:
                d, q_t, kvt = norm_mod_proj(x, P['g_pre_mix'][l], sc1, sh1, w[:, 3 * N_C:], (N_D,), hi, tm_lin,
                                            wt=w[:, :3 * N_C].T, t_splits=(N_C, 2 * N_C))
                hc = moba_prompt(q_t, kvt)
                od, S = hgrn_prompt(d, P['w_lb'], P['g_out_d'], l)
                new_odd = (jnp.transpose(kvt.reshape(G, 2, H_C, DH_C, R), (0, 4, 1, 2, 3)), S)
            else:
                B = R
                qc, kvc, d = norm_mod_proj(x, P['g_pre_mix'][l], sc1, sh1, w, (N_C, 2 * N_C, N_D), hi, tm_lin)
                pool, page_table, S0 = st_odd
                hc = moba_step(qc.reshape(B, N_C, 1), kvc.reshape(B, 2, H_C, DH_C, 1),
                               jnp.transpose(pool, (0, 2, 3, 4, 1)), page_table)
                od, S = hgrn_step(d.reshape(B, 1, N_D), P['w_lb'], P['g_out_d'], S0, l)
                hc = hc.reshape(1, B, -1)
                od = od.reshape(1, B, -1)
                new_odd = (kvc.reshape(B, 1, 2, H_C, DH_C), S)
            parts, swa_parts, w_out = [hc, od], None, P['w_out_odd']
        x = layer_tail(x, g1, P['g_post_mix'][l], parts, swa_parts, w_out.astype(wdt),
                       P['g_pre_ffn'][l], sc2, sh2, g2, P['g_post_ffn'][l],
                       P['w_ffn_gate'].astype(wdt), P['w_ffn_up'].astype(wdt), P['w_ffn_down'].astype(wdt), l,
                       hi, tm=tm_out, tf=256 if hi else P['w_ffn_gate'].shape[-1])
    return x, new_even, new_odd


def kernel(x_prompt, x_sample, state_mlstm_C, state_mlstm_n, state_mlstm_m, cache_swa_w128, cache_swa_w512, cache_swa_w2048, cache_moba_kv, state_hgrn_S, page_table, c_prompt, c_sample, w_ada, b_ada, g_pre_mix, g_post_mix, g_pre_ffn, g_post_ffn, w_in_even, b_gate_a, w_out_even, w_in_odd, w_lb, g_out_d, w_out_odd, w_ffn_gate, w_ffn_up, w_ffn_down):
    P = {'g_pre_mix': g_pre_mix, 'g_post_mix': g_post_mix, 'g_pre_ffn': g_pre_ffn, 'g_post_ffn': g_post_ffn,
         'w_in_even': w_in_even, 'b_gate_a': b_gate_a, 'w_out_even': w_out_even, 'w_in_odd': w_in_odd,
         'w_lb': w_lb, 'g_out_d': g_out_d, 'w_out_odd': w_out_odd, 'w_ffn_gate': w_ffn_gate,
         'w_ffn_up': w_ffn_up, 'w_ffn_down': w_ffn_down}
    Bp = x_prompt.shape[0]
    Bs = x_sample.shape[0]
    D = x_prompt.shape[-1]
    c_all = jnp.concatenate([c_prompt, c_sample], axis=0)
    c_all = jnp.pad(c_all, ((0, -(Bp + Bs) % 8), (0, 0)))
    ada = ada_all(c_all, w_ada, b_ada)[:, :Bp + Bs]
    ada_p = ada[:, :Bp].reshape(ada.shape[0], Bp, 1, 6 * D)
    ada_s = ada[:, Bp:].reshape(ada.shape[0], 1, Bs, 6 * D)
    y_p, ev_p, od_p = _trunk(x_prompt, ada_p, True, None, None, P)
    y_s, ev_s, od_s = _trunk(
        x_sample.reshape(1, Bs, D), ada_s, False,
        (state_mlstm_C, state_mlstm_n, state_mlstm_m, cache_swa_w128, cache_swa_w512, cache_swa_w2048),
        (cache_moba_kv, page_table, state_hgrn_S), P)
    return ((y_p, y_s.reshape(x_sample.shape)) + tuple(ev_p[:3]) + tuple(ev_s[:3]) + tuple(ev_p[3:])
            + tuple(ev_s[3:]) + (od_p[0], od_s[0], od_p[1], od_s[1]))
```

```python
import functools

import jax
import jax.numpy as jnp
from jax import lax
from jax.experimental import pallas as pl
from jax.experimental.pallas import tpu as pltpu

F32 = jnp.float32
BF16 = jnp.bfloat16
HI = lax.Precision.HIGHEST

EPS = 1e-6
NEG = -1e30

H_A, DK_A, DV_A, CHUNK_A = 4, 128, 128, 128
SWA_GROUPS = ((128, 1), (512, 4), (2048, 16))
H_B, DH_B, SWA_BAND = 4, 64, 128
GW_B = H_B * DH_B
H_C, DH_C, MOBA_BLOCK, MOBA_TOPK = 8, 64, 256, 3
H_D, DK_D, DV_D, CHUNK_D, SUB_D = 4, 128, 128, 64, 8

LANES = 128
VMEM_LIMIT = 56 << 20


def _params(sem, big=False):
    return pltpu.CompilerParams(dimension_semantics=sem,
                                vmem_limit_bytes=VMEM_LIMIT if big else None)


def _dot(a, b, hi=False):
    return jnp.dot(a, b, preferred_element_type=F32, precision=HI if hi else None)


def _dot_nt(a, b, hi=False):
    return lax.dot_general(a, b, (((1,), (1,)), ((), ())), preferred_element_type=F32,
                           precision=HI if hi else None)


def _dot_tn(a, b, hi=False):
    return lax.dot_general(a, b, (((0,), (0,)), ((), ())), preferred_element_type=F32,
                           precision=HI if hi else None)


def _rms(x, g):
    return x * lax.rsqrt(jnp.mean(x * x, axis=-1, keepdims=True) + EPS) * g


def _sigmoid(x):
    return 1.0 / (1.0 + jnp.exp(-x))


def _silu(x):
    return x * _sigmoid(x)


def _log_sigmoid(x):
    return jnp.minimum(x, 0.0) - jnp.log(1.0 + jnp.exp(-jnp.abs(x)))


def _col_of_row(row):
    n = row.shape[-1]
    eye = lax.broadcasted_iota(jnp.int32, (n, n), 0) == lax.broadcasted_iota(jnp.int32, (n, n), 1)
    return jnp.sum(jnp.where(eye, row, 0.0), axis=1, keepdims=True)


def _ada_body(c_ref, w_ref, b_ref, o_ref):
    o_ref[0] = _dot(_silu(c_ref[...]), w_ref[0], hi=True) + b_ref[0]


def ada_all(c, w_ada, b_ada, tn=768):
    depth, d, n = w_ada.shape
    r = c.shape[0]
    return pl.pallas_call(
        _ada_body,
        grid=(depth, n // tn),
        in_specs=[pl.BlockSpec((r, d), lambda l, j: (0, 0)),
                  pl.BlockSpec((1, d, tn), lambda l, j: (l, 0, j)),
                  pl.BlockSpec((1, 1, tn), lambda l, j: (l, 0, j))],
        out_specs=pl.BlockSpec((1, r, tn), lambda l, j: (l, 0, j)),
        out_shape=jax.ShapeDtypeStruct((depth, r, n), F32),
        compiler_params=_params(("parallel", "parallel")),
        name="ada",
    )(c, w_ada, b_ada.reshape(depth, 1, n))


def _proj_body(x_ref, g_ref, sc_ref, sh_ref, w_ref, *refs, splits, hi, cw, t_splits):
    h = _rms(x_ref[0], g_ref[...]) * (1.0 + sc_ref[0]) + sh_ref[0]
    hc = h if hi else h.astype(BF16)
    o_refs = refs[1:] if t_splits else refs
    off = 0
    for o_ref, n in zip(o_refs, splits):
        for c0 in range(0, n, cw):
            c1 = min(c0 + cw, n)
            o_ref[0, :, c0:c1] = _dot(hc, w_ref[:, off + c0:off + c1], hi)
        off += n
    if t_splits:
        wt_ref, off = refs[0], 0
        for ot_ref, n in zip(refs[1 + len(splits):], t_splits):
            for r0 in range(0, n, cw):
                ot_ref[0, r0:r0 + cw, :] = _dot_nt(wt_ref[off + r0:off + r0 + cw, :], hc, hi)
            off += n


def _mod_spec(mod, tm):
    if mod.shape[1] == 1:
        return pl.BlockSpec((1, 1, mod.shape[2]), lambda gi, ri: (gi, 0, 0))
    return pl.BlockSpec((1, tm, mod.shape[2]), lambda gi, ri: (gi, ri, 0))


def norm_mod_proj(x, g, sc, sh, w, splits, hi, tm, wt=None, t_splits=()):
    G, R, D = x.shape
    tm = min(tm, R)
    n_all = sum(splits)
    n_t = sum(t_splits)
    ops = [x, g.reshape(1, D), sc, sh, w]
    in_specs = [pl.BlockSpec((1, tm, D), lambda gi, ri: (gi, ri, 0)),
                pl.BlockSpec((1, D), lambda gi, ri: (0, 0)),
                _mod_spec(sc, tm), _mod_spec(sh, tm),
                pl.BlockSpec((D, n_all), lambda gi, ri: (0, 0), pipeline_mode=pl.Buffered(1))]
    out_specs = [pl.BlockSpec((1, tm, n), lambda gi, ri: (gi, ri, 0)) for n in splits]
    out_shape = [jax.ShapeDtypeStruct((G, R, n), F32) for n in splits]
    if n_t:
        ops.append(wt)
        in_specs.append(pl.BlockSpec((n_t, D), lambda gi, ri: (0, 0), pipeline_mode=pl.Buffered(1)))
        out_specs += [pl.BlockSpec((1, n, tm), lambda gi, ri: (gi, 0, ri)) for n in t_splits]
        out_shape += [jax.ShapeDtypeStruct((G, n, R), F32) for n in t_splits]
    return pl.pallas_call(
        functools.partial(_proj_body, splits=tuple(splits), hi=hi, cw=512, t_splits=tuple(t_splits)),
        grid=(G, R // tm),
        in_specs=in_specs,
        out_specs=out_specs,
        out_shape=out_shape,
        compiler_params=_params(("parallel", "parallel"), big=True),
        name="proj",
    )(*ops)


FFN_CHUNK = 512


def _tail_body(*refs, n_parts, swa, hi):
    x_ref, g1_ref, gpm_ref = refs[:3]
    part_refs = refs[3:3 + n_parts]
    pos = 3 + n_parts
    ng = len(SWA_GROUPS)
    o_refs = refs[pos:pos + swa * ng]
    l_refs = refs[pos + swa * ng:pos + 2 * swa * ng]
    pos += 2 * swa * ng
    (wo_ref, gpre_ref, sc_ref, sh_ref, g2_ref, gpf_ref, wg_ref, wu_ref, wd_ref, out_ref,
     x1_scr, h_scr, acc_scr) = refs[pos:]
    k = pl.program_id(2)

    def cast(a):
        return a if hi else a.astype(BF16)

    @pl.when(k == 0)
    def _():
        acc = None
        off = 0
        for p_ref in part_refs:
            a = p_ref[0]
            n = a.shape[-1]
            t = _dot(cast(a), wo_ref[off:off + n, :], hi)
            acc = t if acc is None else acc + t
            off += n
        for c in range(swa):
            ls = [l_ref[0] for l_ref in l_refs[c * ng:(c + 1) * ng]]
            os_ = [o_ref[0] for o_ref in o_refs[c * ng:(c + 1) * ng]]
            mx = functools.reduce(jnp.maximum, ls)
            es = [jnp.exp(l - mx) for l in ls]
            hb = sum(e * o for e, o in zip(es, os_)) / sum(es)
            acc = acc + _dot(cast(hb), wo_ref[off:off + hb.shape[-1], :], hi)
            off += hb.shape[-1]
        x1 = x_ref[0] + g1_ref[0] * _rms(acc, gpm_ref[...])
        x1_scr[...] = x1
        h = _rms(x1, gpre_ref[...]) * (1.0 + sc_ref[0]) + sh_ref[0]
        h_scr[...] = h.astype(h_scr.dtype)
        acc_scr[...] = jnp.zeros_like(acc_scr)

    h = h_scr[...]
    tf = wg_ref.shape[1]
    cuts = list(range(0, tf, FFN_CHUNK)) + [tf]
    chunks = list(zip(cuts[:-1], cuts[1:]))
    gu = [(_dot(h, wg_ref[:, c0:c1], hi), _dot(h, wu_ref[:, c0:c1], hi)) for c0, c1 in chunks[:1]]
    acc = None
    for i, (c0, c1) in enumerate(chunks):
        if i + 1 < len(chunks):
            n0, n1 = chunks[i + 1]
            gu.append((_dot(h, wg_ref[:, n0:n1], hi), _dot(h, wu_ref[:, n0:n1], hi)))
        a = _silu(gu[i][0]) * gu[i][1]
        d = _dot(a.astype(h.dtype), wd_ref[c0:c1, :], hi)
        acc = d if acc is None else acc + d
    acc_scr[...] += acc

    @pl.when(k == pl.num_programs(2) - 1)
    def _():
        out_ref[0] = x1_scr[...] + g2_ref[0] * _rms(acc_scr[...], gpf_ref[...])


def layer_tail(x, g1, gpost_mix, parts, swa_parts, w_out, gpre, sc, sh, g2, gpost_ffn, wg, wu, wd, layer, hi, tm, tf):
    G, R, D = x.shape
    FF = wg.shape[2]
    tm = min(tm, R)
    row = lambda a: pl.BlockSpec((1, tm, a.shape[2]), lambda gi, ri, k: (gi, ri, 0))
    mspec = lambda m: (pl.BlockSpec((1, 1, D), lambda gi, ri, k: (gi, 0, 0)) if m.shape[1] == 1 else row(m))
    vec = pl.BlockSpec((1, D), lambda gi, ri, k: (0, 0))
    one = dict(pipeline_mode=pl.Buffered(1))
    wmode = one if tf == FF else {}
    ops = [x, g1, gpost_mix.reshape(1, D)] + list(parts)
    specs = [row(x), mspec(g1), vec] + [row(p) for p in parts]
    swa = 0
    if swa_parts is not None:
        swa = len(swa_parts[0]) // len(SWA_GROUPS)
        ops += list(swa_parts[0]) + list(swa_parts[1])
        specs += [row(a) for a in ops[-2 * len(swa_parts[0]):]]
    ops += [w_out, gpre.reshape(1, D), sc, sh, g2, gpost_ffn.reshape(1, D), wg, wu, wd]
    specs += [pl.BlockSpec(w_out.shape, lambda gi, ri, k: (0, 0), **one), vec, mspec(sc), mspec(sh), mspec(g2), vec,
              pl.BlockSpec((None, D, tf), lambda gi, ri, k: (layer, 0, k), **wmode),
              pl.BlockSpec((None, D, tf), lambda gi, ri, k: (layer, 0, k), **wmode),
              pl.BlockSpec((None, tf, D), lambda gi, ri, k: (layer, k, 0), **wmode)]
    return pl.pallas_call(
        functools.partial(_tail_body, n_parts=len(parts), swa=swa, hi=hi),
        grid=(G, R // tm, FF // tf),
        in_specs=specs,
        out_specs=row(x),
        out_shape=jax.ShapeDtypeStruct(x.shape, F32),
        scratch_shapes=[pltpu.VMEM((tm, D), F32), pltpu.VMEM((tm, D), F32 if hi else BF16), pltpu.VMEM((tm, D), F32)],
        compiler_params=_params(("parallel", "parallel", "arbitrary"), big=True),
        name="layer_tail",
    )(*ops)


def _mlstm_body(a_ref, g_ref, bg_ref, h_ref, C_ref, n_ref, m_ref):
    L = CHUNK_A
    HD = H_A * DK_A

    @pl.when(pl.program_id(0) == 0)
    def _():
        C_ref[...] = jnp.zeros_like(C_ref)
        n_ref[...] = jnp.zeros_like(n_ref)
        m_ref[...] = jnp.zeros_like(m_ref)

    nb = a_ref.shape[0]
    lane = lax.broadcasted_iota(jnp.int32, (L, LANES), 1)
    ri = lax.broadcasted_iota(jnp.int32, (L, L), 0)
    ci = lax.broadcasted_iota(jnp.int32, (L, L), 1)
    causal = ci <= ri
    tril = causal.astype(F32)
    m_lane = lax.broadcasted_iota(jnp.int32, (1, LANES), 1)
    gates_t, bcum, bcum_t, m_all = [], [], [], []
    for b in range(nb):
        pre = g_ref[b] + bg_ref[...]
        gates = jnp.where(lane >= H_A, _log_sigmoid(pre), pre)
        bcum.append(_dot(tril, gates, hi=True))
        gates_t.append(gates.T)
        bcum_t.append(bcum[b].T)
        m_all.append(m_ref[b])
    m_out = list(m_all)
    chains = [(b, h) for b in range(nb) for h in range(H_A)]
    q = [a_ref[b, :, h * DK_A:(h + 1) * DK_A] for b, h in chains]
    k = [a_ref[b, :, HD + h * DK_A:HD + (h + 1) * DK_A] * (DK_A ** -0.5) for b, h in chains]
    vb = [a_ref[b, :, 2 * HD + h * DV_A:2 * HD + (h + 1) * DV_A].astype(BF16) for b, h in chains]
    qb = [x.astype(BF16) for x in q]
    C = [C_ref[b, h] for b, h in chains]
    n_row = [n_ref[b, h:h + 1, :] for b, h in chains]
    qk = [_dot_nt(qb[c], k[c].astype(BF16)) for c in range(len(chains))]
    qC = [_dot(qb[c], C[c].astype(BF16)) for c in range(len(chains))]
    s, winter, mt, kw, ws, wc = [], [], [], [], [], []
    for c, (b, h) in enumerate(chains):
        ig_row = gates_t[b][h:h + 1, :]
        b_col = bcum[b][:, H_A + h:H_A + h + 1]
        b_row = bcum_t[b][H_A + h:H_A + h + 1, :]
        m = m_all[b][:, h:h + 1]
        dmat = jnp.where(causal, b_col - b_row + ig_row, NEG)
        inter = b_col + m
        mt.append(jnp.maximum(inter, jnp.max(dmat, axis=1, keepdims=True)))
        winter.append(jnp.exp(inter - mt[c]))
        s.append(qk[c] * jnp.exp(dmat - mt[c]))
        bl = b_row[:, L - 1:L]
        gl = bl - b_row + ig_row
        m_new = jnp.maximum(bl + m, jnp.max(gl, axis=1, keepdims=True))
        ws.append(jnp.exp(gl - m_new))
        wc.append(jnp.exp(bl + m - m_new))
        kw.append((k[c].T * ws[c]).astype(BF16))
        m_out[b] = jnp.where(m_lane == h, m_new, m_out[b])
    sv = [_dot(s[c].astype(BF16), vb[c]) for c in range(len(chains))]
    kv = [_dot(kw[c], vb[c]) for c in range(len(chains))]
    kn = [_dot(jnp.broadcast_to(ws[c], (8, L)), k[c], hi=True)[0:1] for c in range(len(chains))]
    for c, (b, h) in enumerate(chains):
        og = a_ref[b, :, 3 * HD + h * DV_A:3 * HD + (h + 1) * DV_A]
        num = sv[c] + winter[c] * qC[c]
        den = jnp.sum(s[c], axis=1, keepdims=True) + winter[c] * jnp.sum(q[c] * n_row[c], axis=1, keepdims=True)
        hh = _sigmoid(og) * (num / jnp.maximum(jnp.abs(den), jnp.exp(-mt[c])))
        h_ref[b, :, h * DV_A:(h + 1) * DV_A] = hh.astype(h_ref.dtype)
        C_ref[b, h] = wc[c] * C[c] + kv[c]
        n_ref[b, h:h + 1, :] = wc[c] * n_row[c] + kn[c]
    for b in range(nb):
        m_ref[b] = m_out[b]


def mlstm_prompt(a, g, bg):
    B, T, _ = a.shape
    L = CHUNK_A
    return pl.pallas_call(
        _mlstm_body,
        grid=(T // L,),
        in_specs=[pl.BlockSpec((B, L, a.shape[2]), lambda c: (0, c, 0)),
                  pl.BlockSpec((B, L, LANES), lambda c: (0, c, 0)),
                  pl.BlockSpec((1, LANES), lambda c: (0, 0))],
        out_specs=[pl.BlockSpec((B, L, H_A * DV_A), lambda c: (0, c, 0)),
                   pl.BlockSpec((B, H_A, DK_A, DV_A), lambda c: (0, 0, 0, 0)),
                   pl.BlockSpec((B, H_A, DK_A), lambda c: (0, 0, 0)),
                   pl.BlockSpec((B, 1, LANES), lambda c: (0, 0, 0))],
        out_shape=[jax.ShapeDtypeStruct((B, T, H_A * DV_A), BF16),
                   jax.ShapeDtypeStruct((B, H_A, DK_A, DV_A), F32),
                   jax.ShapeDtypeStruct((B, H_A, DK_A), F32),
                   jax.ShapeDtypeStruct((B, 1, LANES), F32)],
        compiler_params=_params(("arbitrary",), big=True),
        name="mlstm_prompt",
    )(a, g, bg)


def _mlstm_step_body(a_ref, g_ref, bg_ref, C0_ref, n0_ref, m0_ref, h_ref, C_ref, n_ref, m_ref):
    HD = H_A * DK_A
    pre = g_ref[0] + bg_ref[...]
    lane = lax.broadcasted_iota(jnp.int32, (1, LANES), 1)
    gates = jnp.where(lane >= H_A, _log_sigmoid(pre), pre)
    m_all = m0_ref[0]
    m_out = m_all
    for h in range(H_A):
        q = a_ref[0, :, h * DK_A:(h + 1) * DK_A]
        k = a_ref[0, :, HD + h * DK_A:HD + (h + 1) * DK_A] * (DK_A ** -0.5)
        v = a_ref[0, :, 2 * HD + h * DV_A:2 * HD + (h + 1) * DV_A]
        og = a_ref[0, :, 3 * HD + h * DV_A:3 * HD + (h + 1) * DV_A]
        ig = gates[:, h:h + 1]
        lf = gates[:, H_A + h:H_A + h + 1]
        m = m_all[:, h:h + 1]
        C = C0_ref[0, h]
        n_row = n0_ref[0, h:h + 1, :]
        inter = lf + m
        mt = jnp.maximum(inter, ig)
        s = jnp.sum(q * k, axis=1, keepdims=True) * jnp.exp(ig - mt)
        winter = jnp.exp(inter - mt)
        qC = _dot(jnp.broadcast_to(q, (8, DK_A)), C, hi=True)[0:1]
        num = s * v + winter * qC
        den = s + winter * jnp.sum(q * n_row, axis=1, keepdims=True)
        hh = num / jnp.maximum(jnp.abs(den), jnp.exp(-mt))
        h_ref[0, :, h * DV_A:(h + 1) * DV_A] = _sigmoid(og) * hh
        m_new = jnp.maximum(inter, ig)
        ws = jnp.exp(ig - m_new)
        wc = jnp.exp(inter - m_new)
        C_ref[0, h] = wc * C + (ws * _col_of_row(k)) * v
        n_ref[0, h:h + 1, :] = wc * n_row + ws * k
        m_out = jnp.where(lane == h, m_new, m_out)
    m_ref[0] = m_out


def mlstm_step(a, g, bg, C0, n0, m0):
    B = a.shape[0]
    r3 = lambda w: pl.BlockSpec((1, 1, w), lambda b: (b, 0, 0))
    cs = pl.BlockSpec((1, H_A, DK_A, DV_A), lambda b: (b, 0, 0, 0))
    ns = pl.BlockSpec((1, H_A, DK_A), lambda b: (b, 0, 0))
    return pl.pallas_call(
        _mlstm_step_body,
        grid=(B,),
        in_specs=[r3(a.shape[2]), r3(LANES), pl.BlockSpec((1, LANES), lambda b: (0, 0)), cs, ns, r3(LANES)],
        out_specs=[r3(H_A * DV_A), cs, ns, r3(LANES)],
        out_shape=[jax.ShapeDtypeStruct((B, 1, H_A * DV_A), F32),
                   jax.ShapeDtypeStruct(C0.shape, F32), jax.ShapeDtypeStruct(n0.shape, F32),
                   jax.ShapeDtypeStruct((B, 1, LANES), F32)],
        compiler_params=_params(("parallel",)),
        name="mlstm_step",
    )(a, g, bg, C0, n0, m0)


NPAIR_B = GW_B // LANES
SWA_ROWS = 1024


def _swa_body(*refs, dil):
    Q = SWA_BAND
    q_refs, kc_refs, kp_refs, vc_refs, vp_refs = [refs[i * NPAIR_B:(i + 1) * NPAIR_B] for i in range(5)]
    o_refs = refs[5 * NPAIR_B:6 * NPAIR_B]
    l_refs = refs[6 * NPAIR_B:7 * NPAIR_B]
    qi = lax.broadcasted_iota(jnp.int32, (Q, Q), 0)
    kj = lax.broadcasted_iota(jnp.int32, (Q, Q), 1)
    mask_c = kj <= qi
    mask_in = kj >= qi
    mask_first = jnp.logical_and(mask_in, pl.program_id(1) > 0)
    lane = lax.broadcasted_iota(jnp.int32, (Q, LANES), 1)
    first = lane < DH_B
    chains = [(p, hh) for p in range(NPAIR_B) for hh in range(2)]
    nbs = q_refs[0].shape[1] // (Q * dil)
    for j, r in [(j, r) for j in range(nbs) for r in range(dil)]:
        at = lambda start: (pl.ds(0, 1), pl.ds(start, Q, stride=dil) if dil > 1 else pl.ds(start, Q), slice(None))
        rows = at(j * Q * dil + r)
        ld = lambda ref, rw=rows: ref[rw][0]
        q = [ld(ref) * (DH_B ** -0.5) for ref in q_refs]
        kc = [ld(ref).astype(BF16) for ref in kc_refs]
        vc = [ld(ref).astype(BF16) for ref in vc_refs]
        if j == 0:
            prows, mask_p = at((nbs - 1) * Q * dil + r), mask_first
            kp = [ld(ref, prows).astype(BF16) for ref in kp_refs]
            vp = [ld(ref, prows).astype(BF16) for ref in vp_refs]
        else:
            prows, mask_p = at((j - 1) * Q * dil + r), mask_in
            kp = [ld(ref, prows).astype(BF16) for ref in kc_refs]
            vp = [ld(ref, prows).astype(BF16) for ref in vc_refs]
        qm = [jnp.where(first if hh == 0 else ~first, q[p], 0.0).astype(BF16) for p, hh in chains]
        sc = [_dot_nt(qm[c], kc[p]) for c, (p, hh) in enumerate(chains)]
        sp = [_dot_nt(qm[c], kp[p]) for c, (p, hh) in enumerate(chains)]
        pcs, pps, dens, lses = [], [], [], []
        for c in range(len(chains)):
            s_c = jnp.where(mask_c, sc[c], NEG)
            s_p = jnp.where(mask_p, sp[c], NEG)
            mx = jnp.maximum(jnp.max(s_c, axis=1, keepdims=True), jnp.max(s_p, axis=1, keepdims=True))
            pc = jnp.exp(s_c - mx)
            pp = jnp.exp(s_p - mx)
            den = jnp.sum(pc, axis=1, keepdims=True) + jnp.sum(pp, axis=1, keepdims=True)
            pcs.append(pc.astype(BF16))
            pps.append(pp.astype(BF16))
            dens.append(den)
            lses.append(mx + jnp.log(den))
        os_ = [(_dot(pcs[c], vc[p]) + _dot(pps[c], vp[p])) / dens[c] for c, (p, hh) in enumerate(chains)]
        for p in range(NPAIR_B):
            o_refs[p][rows] = jnp.where(first, os_[2 * p], os_[2 * p + 1])[None]
            l_refs[p][rows] = jnp.where(first, lses[2 * p], lses[2 * p + 1])[None]


def swa_prompt(qkv, g, dil):
    B, T, W = qkv.shape
    ng = W // GW_B // 3
    unit = max(SWA_BAND * dil, min(SWA_ROWS, T))
    col = lambda part, p: (part * ng + g) * NPAIR_B + p
    cur = lambda part: [pl.BlockSpec((1, unit, LANES), functools.partial(lambda b, n, c: (b, n, c), c=col(part, p)))
                        for p in range(NPAIR_B)]
    prev = lambda part: [pl.BlockSpec((1, unit, LANES),
                                      functools.partial(lambda b, n, c: (b, jnp.maximum(n - 1, 0), c), c=col(part, p)))
                         for p in range(NPAIR_B)]
    ospec = pl.BlockSpec((1, unit, LANES), lambda b, n: (b, n, 0))
    res = pl.pallas_call(
        functools.partial(_swa_body, dil=dil),
        grid=(B, T // unit),
        in_specs=cur(0) + cur(1) + prev(1) + cur(2) + prev(2),
        out_specs=[ospec] * (2 * NPAIR_B),
        out_shape=[jax.ShapeDtypeStruct((B, T, LANES), F32)] * (2 * NPAIR_B),
        compiler_params=_params(("parallel", "parallel"), big=True),
        name="swa_prompt_d%d" % dil,
    )(*([qkv] * (5 * NPAIR_B)))
    return res[:NPAIR_B], res[NPAIR_B:]


def _swa_step_body(q_ref, kvn_ref, buf_ref, o_ref, l_ref, nb_ref, *, dil):
    wb = buf_ref.shape[-1]
    lane = lax.broadcasted_iota(jnp.int32, (1, wb), 1)
    read = (lane % dil) == 0
    last = lane == wb - 1
    for h in range(H_B):
        q = q_ref[0, h]
        kn = kvn_ref[0, 0, h]
        vn = kvn_ref[0, 1, h]
        kt = buf_ref[0, 0, h]
        vt = buf_ref[0, 1, h]
        s = jnp.where(read, jnp.sum(kt * q, axis=0, keepdims=True) * (DH_B ** -0.5), NEG)
        s_self = jnp.sum(kn * q, axis=0, keepdims=True) * (DH_B ** -0.5)
        mx = jnp.maximum(jnp.max(s, axis=1, keepdims=True), s_self)
        p = jnp.exp(s - mx)
        p_self = jnp.exp(s_self - mx)
        den = jnp.sum(p, axis=1, keepdims=True) + p_self
        o_ref[0, h] = (jnp.sum(vt * p, axis=1, keepdims=True) + p_self * vn) / den
        l_ref[0, h] = jnp.broadcast_to(mx + jnp.log(den), (DH_B, 1))
        nb_ref[0, 0, h] = jnp.where(last, kn, pltpu.roll(kt, wb - 1, axis=1))
        nb_ref[0, 1, h] = jnp.where(last, vn, pltpu.roll(vt, wb - 1, axis=1))


def swa_step(q, kvn, buf_t, dil):
    B = q.shape[0]
    qs = pl.BlockSpec((1,) + q.shape[1:], lambda b: (b, 0, 0, 0))
    ks = pl.BlockSpec((1,) + kvn.shape[1:], lambda b: (b, 0, 0, 0, 0))
    bs = pl.BlockSpec((1,) + buf_t.shape[1:], lambda b: (b, 0, 0, 0, 0))
    return pl.pallas_call(
        functools.partial(_swa_step_body, dil=dil),
        grid=(B,),
        in_specs=[qs, ks, bs],
        out_specs=[qs, qs, bs],
        out_shape=[jax.ShapeDtypeStruct(q.shape, F32)] * 2 + [jax.ShapeDtypeStruct(buf_t.shape, F32)],
        compiler_params=_params(("parallel",), big=True),
        name="swa_step_d%d" % dil,
    )(q, kvn, buf_t)


AUG = 2 * DH_C


VROWS = DH_C + 16
TQ_C = 128
HEADS_C = 4
PREP_COLS = 1024
LOG2E = 1.4426950408889634


def _moba_kprep_body(kt_ref, vt_ref, km_ref, ka_ref, vta_ref, *, nblk):
    nsub = kt_ref.shape[2] // MOBA_BLOCK
    lane = lax.broadcasted_iota(jnp.int32, (MOBA_BLOCK, DH_C), 1)
    ones = jnp.where(lax.broadcasted_iota(jnp.int32, (VROWS - DH_C, MOBA_BLOCK), 0) == 0, 1.0, 0.0).astype(BF16)
    for j in range(nsub):
        n = pl.program_id(1) * nsub + j
        cols = slice(j * MOBA_BLOCK, (j + 1) * MOBA_BLOCK)
        k = kt_ref[0, :, cols].T
        km_ref[0, pl.ds(n, 1), :] = jnp.mean(k, axis=0, keepdims=True)
        onehot = jnp.where(lane == n, 1.0, 0.0).astype(BF16)
        pieces = []
        for h in range(H_C):
            pieces += [k[:, h * DH_C:(h + 1) * DH_C].astype(BF16), onehot]
        ka_ref[0, cols, :] = jnp.concatenate(pieces, axis=1)
        vt = vt_ref[0, :, cols]
        pieces = []
        for h in range(H_C):
            pieces += [vt[h * DH_C:(h + 1) * DH_C].astype(BF16), ones]
        vta_ref[0, :, cols] = jnp.concatenate(pieces, axis=0)


def _moba_gate_body(qt_ref, km_ref, qa_ref, *, nblk):
    nq = qt_ref.shape[2]
    blk = lax.broadcasted_iota(jnp.int32, (nblk, nq), 0)
    blk_f = blk.astype(F32)
    own = pl.program_id(1) * (nq // MOBA_BLOCK) + lax.broadcasted_iota(jnp.int32, (nblk, nq), 1) // MOBA_BLOCK
    past = blk < own
    gs = [_dot(km_ref[0, :, h * DH_C:(h + 1) * DH_C], qt_ref[0, h * DH_C:(h + 1) * DH_C, :], hi=True)
          for h in range(H_C)]
    for h in range(H_C):
        g = jnp.where(past, gs[h], NEG)
        sel = blk == own
        for _ in range(MOBA_TOPK):
            mx = jnp.max(g, axis=0, keepdims=True)
            first = jnp.min(jnp.where(g == mx, blk_f, float(nblk)), axis=0, keepdims=True)
            pick = blk_f == first
            sel = jnp.logical_or(sel, jnp.logical_and(pick, past))
            g = jnp.where(pick, -jnp.inf, g)
        bias = jnp.where(sel, 0.0, NEG)
        aug = jnp.concatenate([qt_ref[0, h * DH_C:(h + 1) * DH_C, :] * (DH_C ** -0.5 * LOG2E), bias,
                               jnp.zeros((AUG - DH_C - nblk, nq), F32)], axis=0)
        qa_ref[0, h * AUG:(h + 1) * AUG, :] = aug.astype(BF16)


def _moba_attn_body(qt_ref, ka_ref, vta_ref, o_ref, sa_scr, sb_scr):
    own = pl.program_id(2)
    TK = MOBA_BLOCK
    nsub = TK // TQ_C
    chains = [(hh, qs) for hh in range(HEADS_C) for qs in range(nsub)]
    qts = [qt_ref[0, hh * AUG:(hh + 1) * AUG, qs * TQ_C:(qs + 1) * TQ_C] for hh, qs in chains]

    def block(n, hh):
        r0 = pl.multiple_of(n * TK, TK)
        return (ka_ref[0, pl.ds(r0, TK), hh * AUG:(hh + 1) * AUG],
                vta_ref[0, hh * VROWS:(hh + 1) * VROWS, pl.ds(r0, TK)])

    key = lax.broadcasted_iota(jnp.int32, (TK, TQ_C), 0)
    qry = lax.broadcasted_iota(jnp.int32, (TK, TQ_C), 1)

    def issue_scores(n, s_scr):
        ks = [block(n, hh)[0] for hh in range(HEADS_C)]
        for c, (hh, qs) in enumerate(chains):
            s_scr[c] = _dot(ks[hh], qts[c])

    def absorb(n, s_scr, ms_prev, accs, diagonal):
        vs = [block(n, hh)[1] for hh in range(HEADS_C)]
        ms, ps, alphas = [], [], []
        for c, (hh, qs) in enumerate(chains):
            s = s_scr[c]
            if diagonal:
                s = jnp.where(key <= qry + qs * TQ_C, s, NEG)
            mx = jnp.max(s, axis=0, keepdims=True)
            m_new = mx if ms_prev is None else jnp.maximum(ms_prev[c], mx)
            ms.append(m_new)
            ps.append(jnp.exp2(s - m_new).astype(BF16))
            alphas.append(None if ms_prev is None else jnp.exp2(ms_prev[c] - m_new))
        pv = [_dot(vs[hh], ps[c]) for c, (hh, qs) in enumerate(chains)]
        return tuple(ms), tuple(pv[c] if accs is None else alphas[c] * accs[c] + pv[c] for c in range(len(chains)))

    last = jnp.maximum(own - 1, 0)
    issue_scores(own, sa_scr)
    issue_scores(0, sb_scr)
    state = absorb(own, sa_scr, None, None, True)

    def body(i, state):
        issue_scores(2 * i + 1, sa_scr)
        state = absorb(2 * i, sb_scr, state[0], state[1], False)
        issue_scores(jnp.minimum(2 * i + 2, last), sb_scr)
        return absorb(2 * i + 1, sa_scr, state[0], state[1], False)

    state = lax.fori_loop(0, own // 2, body, state)
    ms, accs = lax.cond(own % 2 == 1, lambda st: absorb(last, sb_scr, st[0], st[1], False), lambda st: st, state)
    for qs in range(nsub):
        sub = [accs[c] for c, (hh, q2) in enumerate(chains) if q2 == qs]
        o = jnp.concatenate([a[:DH_C] / a[DH_C:DH_C + 1] for a in sub], axis=0)
        o_ref[0, qs * TQ_C:(qs + 1) * TQ_C, :] = o.T.astype(o_ref.dtype)


def moba_prompt(q_t, kvt):
    B, W, T = q_t.shape
    nblk = T // MOBA_BLOCK
    assert nblk <= AUG - DH_C and nblk % 8 == 0
    pc = min(PREP_COLS, T)
    km, ka, vta = pl.pallas_call(
        functools.partial(_moba_kprep_body, nblk=nblk),
        grid=(B, T // pc),
        in_specs=[pl.BlockSpec((1, W, pc), lambda b, n: (b, 0, n)),
                  pl.BlockSpec((1, W, pc), lambda b, n: (b, 1, n))],
        out_specs=[pl.BlockSpec((1, nblk, W), lambda b, n: (b, 0, 0)),
                   pl.BlockSpec((1, pc, H_C * AUG), lambda b, n: (b, n, 0)),
                   pl.BlockSpec((1, H_C * VROWS, pc), lambda b, n: (b, 0, n))],
        out_shape=[jax.ShapeDtypeStruct((B, nblk, W), F32),
                   jax.ShapeDtypeStruct((B, T, H_C * AUG), BF16),
                   jax.ShapeDtypeStruct((B, H_C * VROWS, T), BF16)],
        compiler_params=_params(("parallel", "arbitrary")),
        name="moba_kprep",
    )(kvt, kvt)
    qt = pl.pallas_call(
        functools.partial(_moba_gate_body, nblk=nblk),
        grid=(B, T // pc),
        in_specs=[pl.BlockSpec((1, W, pc), lambda b, i: (b, 0, i)),
                  pl.BlockSpec((1, nblk, W), lambda b, i: (b, 0, 0))],
        out_specs=pl.BlockSpec((1, H_C * AUG, pc), lambda b, i: (b, 0, i)),
        out_shape=jax.ShapeDtypeStruct((B, H_C * AUG, T), BF16),
        compiler_params=_params(("parallel", "parallel")),
        name="moba_gate",
    )(q_t, km)
    return pl.pallas_call(
        _moba_attn_body,
        grid=(B, H_C // HEADS_C, nblk),
        in_specs=[pl.BlockSpec((1, HEADS_C * AUG, MOBA_BLOCK), lambda b, p, j: (b, p, j)),
                  pl.BlockSpec((1, T, HEADS_C * AUG), lambda b, p, j: (b, 0, p)),
                  pl.BlockSpec((1, HEADS_C * VROWS, T), lambda b, p, j: (b, p, 0))],
        out_specs=pl.BlockSpec((1, MOBA_BLOCK, HEADS_C * DH_C), lambda b, p, j: (b, j, p)),
        out_shape=jax.ShapeDtypeStruct((B, T, W), BF16),
        scratch_shapes=[pltpu.VMEM((HEADS_C * (MOBA_BLOCK // TQ_C), MOBA_BLOCK, TQ_C), F32)] * 2,
        compiler_params=_params(("parallel", "parallel", "arbitrary"), big=True),
        name="moba_attn",
    )(qt, ka, vta)


PAGES_PER_STEP = 32
HEADS_STEP_C = 4


def _moba_step_gate_body(pt_ref, q_ref, *refs, pages_per_block, n_past):
    page_refs = refs[:PAGES_PER_STEP]
    sel_ref, g_scr = refs[PAGES_PER_STEP], refs[PAGES_PER_STEP + 1]
    s = pl.program_id(1)
    q = q_ref[0]
    W = q.shape[0]
    rows = page_refs[0].shape[-1]
    per_step = PAGES_PER_STEP // pages_per_block
    lane = lax.broadcasted_iota(jnp.int32, (W, LANES), 1)

    @pl.when(s == 0)
    def _():
        g_scr[...] = jnp.zeros_like(g_scr)

    g = g_scr[...]
    for j in range(per_step):
        tot = None
        for i in range(pages_per_block):
            t = page_refs[j * pages_per_block + i][0, 0].reshape(W, rows)
            tot = t if tot is None else tot + t
        kmean = jnp.sum(tot, axis=1, keepdims=True) / (pages_per_block * rows)
        g = jnp.where(lane == s * per_step + j, q * kmean, g)
    g_scr[...] = g

    @pl.when(s == pl.num_programs(1) - 1)
    def _():
        seg = (lax.broadcasted_iota(jnp.int32, (8, W), 1) // DH_C ==
               lax.broadcasted_iota(jnp.int32, (8, W), 0)).astype(F32)
        blk = lax.broadcasted_iota(jnp.int32, (8, LANES), 1)
        blk_f = blk.astype(F32)
        gate = jnp.where(blk < n_past, _dot(seg, g, hi=True), NEG)
        sel = jnp.zeros((8, LANES), F32)
        for r in range(MOBA_TOPK):
            mx = jnp.max(gate, axis=1, keepdims=True)
            first = jnp.min(jnp.where(gate == mx, blk_f, float(LANES)), axis=1, keepdims=True)
            sel = jnp.where(blk == r, first, sel)
            gate = jnp.where(blk_f == first, -jnp.inf, gate)
        sel_ref[0] = sel.astype(jnp.int32)


def _moba_step_attn_body(pt_ref, sel_ref, q_ref, kn_ref, vn_ref, *refs, n_pages):
    nh = HEADS_STEP_C
    o_ref = refs[2 * nh * n_pages]
    scale = DH_C ** -0.5
    for hh in range(nh):
        k_refs = refs[hh * n_pages:(hh + 1) * n_pages]
        v_refs = refs[(nh + hh) * n_pages:(nh + hh + 1) * n_pages]
        q = q_ref[0, hh]
        kn = kn_ref[0, 0, hh]
        vn = vn_ref[0, 0, hh]
        ss = [jnp.sum(k_ref[0, 0, 0] * q, axis=0, keepdims=True) * scale for k_ref in k_refs]
        s_self = jnp.sum(kn * q, axis=0, keepdims=True) * scale
        mx = s_self
        for sj in ss:
            mx = jnp.maximum(mx, jnp.max(sj, axis=1, keepdims=True))
        p_self = jnp.exp(s_self - mx)
        den = p_self
        o = p_self * vn
        for sj, v_ref in zip(ss, v_refs):
            p = jnp.exp(sj - mx)
            den = den + jnp.sum(p, axis=1, keepdims=True)
            o = o + jnp.sum(v_ref[0, 0, 0] * p, axis=1, keepdims=True)
        o_ref[0, hh] = o / den


def moba_step(q, kv_new, pool_t, page_table):
    B, W, _ = q.shape
    page = pool_t.shape[-1]
    n_pt = page_table.shape[1]
    ppb = MOBA_BLOCK // page
    n_past = n_pt // ppb
    assert MOBA_TOPK <= n_past <= LANES and PAGES_PER_STEP % ppb == 0 and n_pt % PAGES_PER_STEP == 0
    pt_flat = page_table.reshape(-1)
    kpage = (1, 1, H_C, DH_C, page)
    sel = pl.pallas_call(
        functools.partial(_moba_step_gate_body, pages_per_block=ppb, n_past=n_past),
        grid_spec=pltpu.PrefetchScalarGridSpec(
            num_scalar_prefetch=1,
            grid=(B, n_pt // PAGES_PER_STEP),
            in_specs=[pl.BlockSpec((1, W, 1), lambda b, s, pt: (b, 0, 0))] +
                     [pl.BlockSpec(kpage, functools.partial(
                         lambda b, s, pt, i: (pt[b * n_pt + s * PAGES_PER_STEP + i], 0, 0, 0, 0), i=i))
                      for i in range(PAGES_PER_STEP)],
            out_specs=pl.BlockSpec((1, 8, LANES), lambda b, s, pt: (b, 0, 0)),
            scratch_shapes=[pltpu.VMEM((W, LANES), F32)]),
        out_shape=jax.ShapeDtypeStruct((B, 8, LANES), jnp.int32),
        compiler_params=_params(("parallel", "arbitrary"), big=True),
        name="moba_step_gate",
    )(pt_flat, q, *([pool_t] * PAGES_PER_STEP))
    sel_flat = sel[:, :H_C, :MOBA_TOPK].reshape(-1)
    n_pages = MOBA_TOPK * ppb

    nh = HEADS_STEP_C

    def page_map(b, hq, pt, sl, hh, r, i, kv):
        h = hq * nh + hh
        blk = sl[(b * H_C + h) * MOBA_TOPK + r]
        return (pt[b * n_pt + blk * ppb + i], kv, h, 0, 0)

    def page_specs(kv):
        return [pl.BlockSpec((1, 1, 1, DH_C, page), functools.partial(page_map, hh=hh, r=r, i=i, kv=kv))
                for hh in range(nh) for r in range(MOBA_TOPK) for i in range(ppb)]

    q4 = q.reshape(B, H_C, DH_C, 1)
    head = pl.BlockSpec((1, nh, DH_C, 1), lambda b, hq, pt, sl: (b, hq, 0, 0))
    new = lambda kv: pl.BlockSpec((1, 1, nh, DH_C, 1), lambda b, hq, pt, sl: (b, kv, hq, 0, 0))
    return pl.pallas_call(
        functools.partial(_moba_step_attn_body, n_pages=n_pages),
        grid_spec=pltpu.PrefetchScalarGridSpec(
            num_scalar_prefetch=2,
            grid=(B, H_C // nh),
            in_specs=[head, new(0), new(1)] + page_specs(0) + page_specs(1),
            out_specs=head),
        out_shape=jax.ShapeDtypeStruct(q4.shape, F32),
        compiler_params=_params(("parallel", "parallel")),
        name="moba_step_attn",
    )(pt_flat, sel_flat, q4, kv_new, kv_new, *([pool_t] * (2 * nh * n_pages)))


def _hgrn_gates(f, wlb_ref, layer):
    w = wlb_ref[...]
    e = jnp.exp(w - jnp.max(w, axis=0, keepdims=True))
    sm = e / jnp.sum(e, axis=0, keepdims=True)
    lb = jnp.sum(sm[1:layer + 1], axis=0, keepdims=True)
    sg = _sigmoid(f)
    return jnp.log(lb + (1.0 - lb) * sg), (1.0 - lb) * (1.0 - sg)


def _hgrn_body(d_ref, wlb_ref, go_ref, o_ref, S_ref, *, layer):
    L, SB = CHUNK_D, SUB_D
    HD = H_D * DK_D

    @pl.when(pl.program_id(0) == 0)
    def _():
        S_ref[...] = jnp.zeros_like(S_ref)

    nb = d_ref.shape[0]
    ri = lax.broadcasted_iota(jnp.int32, (L, L), 0)
    ci = lax.broadcasted_iota(jnp.int32, (L, L), 1)
    tril = (ci <= ri).astype(F32)
    sub_row = lax.broadcasted_iota(jnp.int32, (SB, 1), 0)
    gates = [_hgrn_gates(d_ref[b, :, HD:2 * HD], wlb_ref, layer) for b in range(nb)]
    b_all = [_dot(tril, lf, hi=True) for lf, _ in gates]
    chains = [(b, h) for b in range(nb) for h in range(H_D)]
    nch = range(len(chains))
    blocks = range(SB, L, SB)
    q = [d_ref[b, :, h * DK_D:(h + 1) * DK_D] for b, h in chains]
    v = [d_ref[b, :, 2 * HD + h * DV_D:2 * HD + (h + 1) * DV_D] for b, h in chains]
    bc = [b_all[b][:, h * DK_D:(h + 1) * DK_D] for b, h in chains]
    k = [gates[b][1][:, h * DK_D:(h + 1) * DK_D] for b, h in chains]
    S = [S_ref[b, h] for b, h in chains]
    vb = [x.astype(BF16) for x in v]
    o_inter = [_dot((q[c] * jnp.exp(bc[c])).astype(BF16), S[c].astype(BF16)) for c in nch]
    rnd = lambda x: x.astype(BF16).astype(F32)
    vr = [x.astype(F32) for x in vb]
    a_off = []
    for c in nch:
        for r0 in blocks:
            ref_b = bc[c][r0 - 1:r0]
            qs = rnd(q[c][r0:r0 + SB] * jnp.exp(bc[c][r0:r0 + SB] - ref_b))
            ks = rnd(k[c][:r0] * jnp.exp(ref_b - bc[c][:r0]))
            a_off.append(rnd(_dot_nt(qs, ks)))
    kw = [(k[c] * jnp.exp(bc[c][L - 1:L] - bc[c])).astype(BF16) for c in nch]
    kv = [_dot_tn(kw[c], vb[c]) for c in nch]
    o_off = [_dot(a_off[c * len(blocks) + i], vr[c][:r0]) for c in nch for i, r0 in enumerate(blocks)]
    for c, (b, h) in enumerate(chains):
        b2 = bc[c] * LOG2E
        o_blocks = []
        for i, r0 in enumerate(range(0, L, SB)):
            qI = q[c][r0:r0 + SB]
            bI = b2[r0:r0 + SB]
            oI = o_inter[c][r0:r0 + SB]
            if r0 > 0:
                oI = oI + o_off[c * len(blocks) + i - 1]
            for s in range(SB):
                row = r0 + s
                a = jnp.sum(qI * k[c][row:row + 1] * jnp.exp2(bI - b2[row:row + 1]), axis=1, keepdims=True)
                oI = oI + jnp.where(sub_row >= s, a, 0.0) * v[c][row:row + 1]
            o_blocks.append(oI)
        o = jnp.concatenate(o_blocks, axis=0)
        gd = d_ref[b, :, 3 * HD + h * DV_D:3 * HD + (h + 1) * DV_D]
        o_ref[b, :, h * DV_D:(h + 1) * DV_D] = (_rms(o, go_ref[...]) * _silu(gd)).astype(o_ref.dtype)
        S_ref[b, h] = _col_of_row(jnp.exp(bc[c][L - 1:L])) * S[c] + kv[c]


def hgrn_prompt(d, w_lb, g_out, layer):
    B, T, W = d.shape
    L = CHUNK_D
    HD = H_D * DK_D
    return pl.pallas_call(
        functools.partial(_hgrn_body, layer=layer),
        grid=(T // L,),
        in_specs=[pl.BlockSpec((B, L, W), lambda c: (0, c, 0)),
                  pl.BlockSpec(w_lb.shape, lambda c: (0, 0)),
                  pl.BlockSpec((1, DV_D), lambda c: (0, 0))],
        out_specs=[pl.BlockSpec((B, L, H_D * DV_D), lambda c: (0, c, 0)),
                   pl.BlockSpec((B, H_D, DK_D, DV_D), lambda c: (0, 0, 0, 0))],
        out_shape=[jax.ShapeDtypeStruct((B, T, H_D * DV_D), BF16),
                   jax.ShapeDtypeStruct((B, H_D, DK_D, DV_D), F32)],
        compiler_params=_params(("arbitrary",), big=True),
        name="hgrn_prompt",
    )(d, w_lb, g_out.reshape(1, DV_D))


def _hgrn_step_body(d_ref, wlb_ref, go_ref, S0_ref, o_ref, S_ref, *, layer):
    HD = H_D * DK_D
    lf, kd = _hgrn_gates(d_ref[0, :, HD:2 * HD], wlb_ref, layer)
    eb = jnp.exp(lf)
    for h in range(H_D):
        sl = slice(h * DK_D, (h + 1) * DK_D)
        q = d_ref[0, :, sl]
        v = d_ref[0, :, 2 * HD + h * DV_D:2 * HD + (h + 1) * DV_D]
        gd = d_ref[0, :, 3 * HD + h * DV_D:3 * HD + (h + 1) * DV_D]
        k = kd[:, sl]
        e = eb[:, sl]
        S = S0_ref[0, h]
        a = jnp.sum(q * k, axis=1, keepdims=True)
        o = a * v + _dot(jnp.broadcast_to(q * e, (8, DK_D)), S, hi=True)[0:1]
        o_ref[0, :, h * DV_D:(h + 1) * DV_D] = _rms(o, go_ref[...]) * _silu(gd)
        S_ref[0, h] = _col_of_row(e) * S + _col_of_row(k) * v


def hgrn_step(d, w_lb, g_out, S0, layer):
    B, _, W = d.shape
    ss = pl.BlockSpec((1, H_D, DK_D, DV_D), lambda b: (b, 0, 0, 0))
    return pl.pallas_call(
        functools.partial(_hgrn_step_body, layer=layer),
        grid=(B,),
        in_specs=[pl.BlockSpec((1, 1, W), lambda b: (b, 0, 0)),
                  pl.BlockSpec(w_lb.shape, lambda b: (0, 0)),
                  pl.BlockSpec((1, DV_D), lambda b: (0, 0)), ss],
        out_specs=[pl.BlockSpec((1, 1, H_D * DV_D), lambda b: (b, 0, 0)), ss],
        out_shape=[jax.ShapeDtypeStruct((B, 1, H_D * DV_D), F32), jax.ShapeDtypeStruct(S0.shape, F32)],
        compiler_params=_params(("parallel",)),
        name="hgrn_step",
    )(d, w_lb, g_out.reshape(1, DV_D), S0)


N_A = 4 * H_A * DK_A
N_B = 3 * len(SWA_GROUPS) * GW_B
N_C = H_C * DH_C
N_D = 4 * H_D * DK_D


def _even_weights(w_in):
    gates = jnp.pad(w_in[:, N_A:N_A + 2 * H_A], ((0, 0), (0, LANES - 2 * H_A)))
    return jnp.concatenate([w_in[:, :N_A], w_in[:, N_A + 2 * H_A:], gates], axis=1)


def _trunk(x, ada, prompt, st_even, st_odd, P):
    hi = not prompt
    G, R, D = x.shape
    wdt = F32 if hi else BF16
    tm_lin = 512
    tm_out = 512
    new_even = new_odd = None
    for l in range(2):
        sh1, sc1, g1, sh2, sc2, g2 = [ada[l][..., i * D:(i + 1) * D] for i in range(6)]
        if l == 0:
            w = _even_weights(P['w_in_even']).astype(wdt)
            splits = (N_A, N_B, LANES)
            a, qkv, gg = norm_mod_proj(x, P['g_pre_mix'][l], sc1, sh1, w, splits, hi, tm_lin)
            bg = jnp.pad(P['b_gate_a'].reshape(1, 2 * H_A), ((0, 0), (0, LANES - 2 * H_A)))
            if prompt:
                ha, C, n, m = mlstm_prompt(a, gg, bg)
                swa = [swa_prompt(qkv, g, dil) for g, (_, dil) in enumerate(SWA_GROUPS)]
                T = R
                bufs = []
                for g, (win, _) in enumerate(SWA_GROUPS):
                    wb = min(win, T)
                    kb = qkv[:, T - wb:, (3 + g) * GW_B:(4 + g) * GW_B]
                    vb = qkv[:, T - wb:, (6 + g) * GW_B:(7 + g) * GW_B]
                    bufs.append(jnp.concatenate([kb, vb], axis=-1).reshape(G, wb, 2, H_B, DH_B))
                new_even = (C, n, m[:, 0, :H_A], bufs[0], bufs[1], bufs[2])
            else:
                B = R
                C0, n0, m0 = st_even[:3]
                m0p = jnp.pad(m0, ((0, 0), (0, LANES - H_A))).reshape(B, 1, LANES)
                tok = lambda t: t.reshape(B, 1, t.shape[-1])
                ha, C, n, m = mlstm_step(tok(a), tok(gg), bg, C0, n0, m0p)
                swa, bufs = [], []
                ng = len(SWA_GROUPS)
                qkv5 = qkv.reshape(B, 3, ng, H_B, DH_B, 1)
                for g, (_, dil) in enumerate(SWA_GROUPS):
                    buf_t = jnp.transpose(st_even[3 + g], (0, 2, 3, 4, 1))
                    o, lse, nb = swa_step(qkv5[:, 0, g], qkv5[:, 1:, g], buf_t, dil)
                    o, lse = o.reshape(1, B, GW_B), lse.reshape(1, B, GW_B)
                    swa.append(([o[..., p * LANES:(p + 1) * LANES] for p in range(NPAIR_B)],
                                [lse[..., p * LANES:(p + 1) * LANES] for p in range(NPAIR_B)]))
                    bufs.append(jnp.transpose(nb, (0, 4, 1, 2, 3)))
                ha = ha.reshape(1, B, -1)
                new_even = (C, n, m[:, 0, :H_A], bufs[0], bufs[1], bufs[2])
            swa_os = [swa[g][0][p] for p in range(NPAIR_B) for g in range(len(SWA_GROUPS))]
            swa_ls = [swa[g][1][p] for p in range(NPAIR_B) for g in range(len(SWA_GROUPS))]
            parts, swa_parts, w_out = [ha], (swa_os, swa_ls), P['w_out_even']
        else:
            w = P['w_in_odd'].astype(wdt)
            if prompt:
                d, q_t, kvt = norm_mod_proj(x, P['g_pre_mix'][l], sc1, sh1, w[:, 3 * N_C:], (N_D,), hi, tm_lin,
                                            wt=w[:, :3 * N_C].T, t_splits=(N_C, 2 * N_C))
                hc = moba_prompt(q_t, kvt)
                od, S = hgrn_prompt(d, P['w_lb'], P['g_out_d'], l)
                new_odd = (jnp.transpose(kvt.reshape(G, 2, H_C, DH_C, R), (0, 4, 1, 2, 3)), S)
            else:
                B = R
                qc, kvc, d = norm_mod_proj(x, P['g_pre_mix'][l], sc1, sh1, w, (N_C, 2 * N_C, N_D), hi, tm_lin)
                pool, page_table, S0 = st_odd
                hc = moba_step(qc.reshape(B, N_C, 1), kvc.reshape(B, 2, H_C, DH_C, 1),
                               jnp.transpose(pool, (0, 2, 3, 4, 1)), page_table)
                od, S = hgrn_step(d.reshape(B, 1, N_D), P['w_lb'], P['g_out_d'], S0, l)
                hc = hc.reshape(1, B, -1)
                od = od.reshape(1, B, -1)
                new_odd = (kvc.reshape(B, 1, 2, H_C, DH_C), S)
            parts, swa_parts, w_out = [hc, od], None, P['w_out_odd']
        x = layer_tail(x, g1, P['g_post_mix'][l], parts, swa_parts, w_out.astype(wdt),
                       P['g_pre_ffn'][l], sc2, sh2, g2, P['g_post_ffn'][l],
                       P['w_ffn_gate'].astype(wdt), P['w_ffn_up'].astype(wdt), P['w_ffn_down'].astype(wdt), l,
                       hi, tm=tm_out, tf=256 if hi else P['w_ffn_gate'].shape[-1])
    return x, new_even, new_odd


def kernel(x_prompt, x_sample, state_mlstm_C, state_mlstm_n, state_mlstm_m, cache_swa_w128, cache_swa_w512, cache_swa_w2048, cache_moba_kv, state_hgrn_S, page_table, c_prompt, c_sample, w_ada, b_ada, g_pre_mix, g_post_mix, g_pre_ffn, g_post_ffn, w_in_even, b_gate_a, w_out_even, w_in_odd, w_lb, g_out_d, w_out_odd, w_ffn_gate, w_ffn_up, w_ffn_down):
    P = {'g_pre_mix': g_pre_mix, 'g_post_mix': g_post_mix, 'g_pre_ffn': g_pre_ffn, 'g_post_ffn': g_post_ffn,
         'w_in_even': w_in_even, 'b_gate_a': b_gate_a, 'w_out_even': w_out_even, 'w_in_odd': w_in_odd,
         'w_lb': w_lb, 'g_out_d': g_out_d, 'w_out_odd': w_out_odd, 'w_ffn_gate': w_ffn_gate,
         'w_ffn_up': w_ffn_up, 'w_ffn_down': w_ffn_down}
    Bp = x_prompt.shape[0]
    Bs = x_sample.shape[0]
    D = x_prompt.shape[-1]
    c_all = jnp.concatenate([c_prompt, c_sample], axis=0)
    c_all = jnp.pad(c_all, ((0, -(Bp + Bs) % 8), (0, 0)))
    ada = ada_all(c_all, w_ada, b_ada)[:, :Bp + Bs]
    ada_p = ada[:, :Bp].reshape(ada.shape[0], Bp, 1, 6 * D)
    ada_s = ada[:, Bp:].reshape(ada.shape[0], 1, Bs, 6 * D)
    y_p, ev_p, od_p = _trunk(x_prompt, ada_p, True, None, None, P)
    y_s, ev_s, od_s = _trunk(
        x_sample.reshape(1, Bs, D), ada_s, False,
        (state_mlstm_C, state_mlstm_n, state_mlstm_m, cache_swa_w128, cache_swa_w512, cache_swa_w2048),
        (cache_moba_kv, page_table, state_hgrn_S), P)
    return ((y_p, y_s.reshape(x_sample.shape)) + tuple(ev_p[:3]) + tuple(ev_s[:3]) + tuple(ev_p[3:])
            + tuple(ev_s[3:]) + (od_p[0], od_s[0], od_p[1], od_s[1]))
```
